```python
import math
import jax, jax.numpy as jnp
from jax import lax
import numpy as np

D_MODEL = 1024
BATCH = 8
SEQ = 2048
DEPTH = 1

CHUNK = 64
POOL_WINDOWS = (2, 4, 8, 16)
POOL_GROUPS = len(POOL_WINDOWS)
POOL_WIDTH = D_MODEL // 2
POOL_GROUP_DIM = POOL_WIDTH // POOL_GROUPS
RET_HEADS = 4
RET_WIDTH = D_MODEL // 2
RET_HEAD_DIM = RET_WIDTH // RET_HEADS
N_BRANCHES = 2
IN_WIDTH = POOL_WIDTH + 4 * RET_WIDTH + N_BRANCHES * D_MODEL
D_FF = 4 * D_MODEL
N_MOD = 6
ROPE_BASE = 10000.0
EPS = 1e-6

kernel_name = "chunk_causal_pool_retention_hybrid_block"


def _rmsnorm(x, g):
    xf = x.astype(jnp.float32)
    y = xf * lax.rsqrt(jnp.mean(xf * xf, axis=-1, keepdims=True) + EPS)
    return (y * g.astype(jnp.float32)).astype(x.dtype)


def _rope(x, positions):
    half = x.shape[-1] // 2
    inv_freq = ROPE_BASE ** (-jnp.arange(half, dtype=jnp.float32) / half)
    ang = positions.astype(jnp.float32)[..., None] * inv_freq
    cos = jnp.cos(ang)[:, :, None, :].astype(x.dtype)
    sin = jnp.sin(ang)[:, :, None, :].astype(x.dtype)
    x1, x2 = x[..., :half], x[..., half:]
    return jnp.concatenate([x1 * cos - x2 * sin, x2 * cos + x1 * sin], axis=-1)


def _pool_branch(u, w_group, scale):
    B, S, _ = u.shape
    uf = u.astype(jnp.float32)
    cs = jnp.cumsum(uf, axis=1)
    t = jnp.arange(1, S + 1, dtype=jnp.float32)[None, :, None]
    outs = []
    for gi, w in enumerate(POOL_WINDOWS):
        sl = slice(gi * POOL_GROUP_DIM, (gi + 1) * POOL_GROUP_DIM)
        cg = cs[..., sl]
        lag = jnp.pad(cg[:, :S - w], ((0, 0), (w, 0), (0, 0)))
        mean = (cg - lag) / jnp.minimum(t, float(w))
        outs.append(mean - uf[..., sl])
    pooled = jnp.stack(outs, axis=2).astype(u.dtype)
    mixed = jnp.einsum('bsgc,gcd->bsgd', pooled, w_group).reshape(B, S, POOL_WIDTH)
    return mixed * scale


def _retention_branch(q, k, v, g, positions):
    B, S, _ = q.shape
    N = S // CHUNK
    H, d = RET_HEADS, RET_HEAD_DIM
    q = _rope(q.reshape(B, S, H, d), positions) * (d ** -0.5)
    k = _rope(k.reshape(B, S, H, d), positions)
    v = v.reshape(B, S, H, d)
    log_gamma = jnp.log(1.0 - 2.0 ** (-5.0 - jnp.arange(H, dtype=jnp.float32)))
    idx = jnp.arange(CHUNK, dtype=jnp.float32)
    dist = jnp.abs(idx[:, None] - idx[None, :])
    d_intra = jnp.exp(log_gamma[:, None, None] * dist).astype(q.dtype)
    k_decay = jnp.exp(log_gamma[None, :] * (CHUNK - 1 - idx)[:, None]).astype(q.dtype)
    q_decay = jnp.exp(log_gamma[None, :] * (idx + 1)[:, None]).astype(q.dtype)
    chunk_decay = jnp.exp(log_gamma * CHUNK).astype(q.dtype)

    qc = q.reshape(B, N, CHUNK, H, d)
    kc = k.reshape(B, N, CHUNK, H, d)
    vc = v.reshape(B, N, CHUNK, H, d)

    scores = jnp.einsum('bnihd,bnjhd->bnhij', qc, kc) * d_intra
    o_intra = jnp.einsum('bnhij,bnjhe->bnihe', scores, vc)

    kv = jnp.einsum('bnjhd,bnjhe->nbhde', kc * k_decay[:, :, None], vc)

    def step(state, kv_n):
        return state * chunk_decay[None, :, None, None] + kv_n, state

    _, states = lax.scan(step, jnp.zeros_like(kv[0]), kv)
    o_cross = jnp.einsum('bnihd,nbhde->bnihe', qc * q_decay[:, :, None], states)

    o = (o_intra + o_cross).reshape(B, S, H, d).astype(jnp.float32)
    mu = jnp.mean(o, axis=-1, keepdims=True)
    var = jnp.mean(jnp.square(o - mu), axis=-1, keepdims=True)
    o_norm = ((o - mu) * lax.rsqrt(var + EPS)).reshape(B, S, RET_WIDTH).astype(g.dtype)
    return jax.nn.silu(g) * o_norm


def _mixer(h, positions, w_in, b_branch_gate, pool_w, pool_scale,
           w_branch_pool, w_branch_ret, w_out):
    B, S, _ = h.shape
    proj = h @ w_in
    cuts = [POOL_WIDTH + i * RET_WIDTH for i in range(5)]
    u_pool, q, k, v, g_ret, gate_logits = jnp.split(proj, cuts, axis=-1)
    y_pool = _pool_branch(u_pool, pool_w, pool_scale) @ w_branch_pool
    y_ret = _retention_branch(q, k, v, g_ret, positions) @ w_branch_ret
    gates = jax.nn.sigmoid(gate_logits + b_branch_gate).reshape(B, S, N_BRANCHES, D_MODEL)
    merged = gates[:, :, 0] * y_pool + gates[:, :, 1] * y_ret
    return merged @ w_out


def setup_inputs(seed: int = 0) -> dict:
    key = jax.random.key(seed)
    ks = jax.random.split(key, 20)
    f32 = jnp.float32

    def nrm(k, shape, scale):
        return jax.random.normal(k, shape, f32) * scale

    def gain(k, shape):
        return 1.0 + 0.05 * jax.random.normal(k, shape, f32)

    x = jax.random.normal(ks[0], (BATCH, SEQ, D_MODEL), f32)
    c = jax.random.normal(ks[1], (BATCH, D_MODEL), f32)
    offsets = jax.random.randint(ks[2], (BATCH, 1), 0, 4096, dtype=jnp.int32)
    positions = offsets + jnp.arange(SEQ, dtype=jnp.int32)[None, :]
    return {
        "x": x,
        "c": c,
        "positions": positions,
        "ada_w": nrm(ks[3], (DEPTH, D_MODEL, N_MOD * D_MODEL), 0.5 * D_MODEL ** -0.5),
        "ada_b": nrm(ks[4], (DEPTH, N_MOD * D_MODEL), 0.02),
        "mix_pre_g": gain(ks[5], (DEPTH, D_MODEL)),
        "mix_post_g": gain(ks[6], (DEPTH, D_MODEL)),
        "ffn_pre_g": gain(ks[7], (DEPTH, D_MODEL)),
        "ffn_post_g": gain(ks[8], (DEPTH, D_MODEL)),
        "w_in": nrm(ks[9], (DEPTH, D_MODEL, IN_WIDTH), D_MODEL ** -0.5),
        "b_branch_gate": nrm(ks[10], (DEPTH, N_BRANCHES * D_MODEL), 0.02),
        "pool_w": nrm(ks[11], (DEPTH, POOL_GROUPS, POOL_GROUP_DIM, POOL_GROUP_DIM), POOL_GROUP_DIM ** -0.5),
        "pool_scale": 1.0 + 0.1 * jax.random.normal(ks[12], (DEPTH, POOL_WIDTH), f32),
        "w_branch_pool": nrm(ks[13], (DEPTH, POOL_WIDTH, D_MODEL), POOL_WIDTH ** -0.5),
        "w_branch_ret": nrm(ks[14], (DEPTH, RET_WIDTH, D_MODEL), RET_WIDTH ** -0.5),
        "w_out": nrm(ks[15], (DEPTH, D_MODEL, D_MODEL), D_MODEL ** -0.5),
        "w_ff1": nrm(ks[16], (DEPTH, D_MODEL, D_FF), D_MODEL ** -0.5),
        "w_ff2": nrm(ks[17], (DEPTH, D_FF, D_MODEL), D_FF ** -0.5),
    }


def reference(x, c, positions, ada_w, ada_b, mix_pre_g, mix_post_g, ffn_pre_g, ffn_post_g,
              w_in, b_branch_gate, pool_w, pool_scale, w_branch_pool, w_branch_ret, w_out,
              w_ff1, w_ff2):
    for l in range(DEPTH):
        mod = jax.nn.silu(c) @ ada_w[l] + ada_b[l]
        sh_m, sc_m, gt_m, sh_f, sc_f, gt_f = [m[:, None, :] for m in jnp.split(mod, N_MOD, axis=-1)]

        h = _rmsnorm(x, mix_pre_g[l]) * (1.0 + sc_m) + sh_m
        y = _mixer(h, positions, w_in[l], b_branch_gate[l], pool_w[l], pool_scale[l],
                   w_branch_pool[l], w_branch_ret[l], w_out[l])
        x = x + gt_m * _rmsnorm(y, mix_post_g[l])

        h = _rmsnorm(x, ffn_pre_g[l]) * (1.0 + sc_f) + sh_f
        y = jnp.square(jax.nn.relu(h @ w_ff1[l])) @ w_ff2[l]
        x = x + gt_f * _rmsnorm(y, ffn_post_g[l])
    return x
```

```python
import functools

import jax
import jax.numpy as jnp
import numpy as np
from jax import lax
from jax.experimental import pallas as pl
from jax.experimental.pallas import tpu as pltpu

F32 = jnp.float32
BF16 = jnp.bfloat16

CHUNK = 64
POOL_WINDOWS = (2, 4, 8, 16)
POOL_HALO = 16
RET_HEADS = 4
N_BRANCHES = 2
N_MOD = 6
ROPE_BASE = 10000.0
EPS = 1e-6

LANES = 128
MIXER_TILE = 256
FFN_TILE = 512
ADALN_BLOCK_N = 1536
VMEM_LIMIT_BYTES = 56 * 1024 * 1024


def _rms(xf):
    return xf * lax.rsqrt(jnp.mean(xf * xf, axis=-1, keepdims=True) + EPS)


def _adaln_kernel(c_ref, w_ref, b_ref, o_ref):
    c = c_ref[...]
    a = (c * jax.nn.sigmoid(c)).astype(BF16)
    o_ref[...] = jnp.dot(a, w_ref[...].astype(BF16), preferred_element_type=F32) + b_ref[...]


def _adaln(c, w, b):
    bsz, d = c.shape
    n = w.shape[1]
    return pl.pallas_call(
        _adaln_kernel,
        out_shape=jax.ShapeDtypeStruct((bsz, n), F32),
        grid=(n // ADALN_BLOCK_N,),
        in_specs=[
            pl.BlockSpec((bsz, d), lambda j: (0, 0)),
            pl.BlockSpec((d, ADALN_BLOCK_N), lambda j: (0, j)),
            pl.BlockSpec((1, ADALN_BLOCK_N), lambda j: (0, j)),
        ],
        out_specs=pl.BlockSpec((bsz, ADALN_BLOCK_N), lambda j: (0, j)),
        compiler_params=pltpu.CompilerParams(dimension_semantics=("arbitrary",)),
        name="adaln",
    )(c, w, b.reshape(1, n))


def _mixer_kernel(x_ref, mod_ref, pos_ref, invf_ref, sgn_ref, g_pre_ref, g_post_ref,
                  w_in_ref, b_gate_ref, pool_w_ref, pool_scale_ref, w_bp_ref, w_br_ref,
                  w_out_ref, dmat_ref, qdec_ref, kdec_ref,
                  o_ref, state_ref, ubuf_ref, *, tile_decay):
    t_idx = pl.program_id(1)
    tile, d_model = x_ref.shape
    pool_width = pool_scale_ref.shape[1]
    ret_width = w_br_ref.shape[0]
    hd = ret_width // RET_HEADS

    @pl.when(t_idx == 0)
    def _():
        state_ref[...] = jnp.zeros_like(state_ref)
        ubuf_ref[0:POOL_HALO, :] = jnp.zeros((POOL_HALO, pool_width), F32)

    x = x_ref[...]
    sh_m, sc_m, gt_m = mod_ref[0:1, :], mod_ref[1:2, :], mod_ref[2:3, :]
    h = ((_rms(x) * g_pre_ref[...]) * (1.0 + sc_m) + sh_m).astype(BF16)

    def proj(lo, width):
        return jnp.dot(h, w_in_ref[:, lo:lo + width], preferred_element_type=F32)

    u = proj(0, pool_width)
    ubuf_ref[POOL_HALO:POOL_HALO + tile, :] = u
    t_pos = (lax.broadcasted_iota(jnp.int32, (tile, LANES), 0) + (t_idx * tile + 1)).astype(F32)
    gd = pool_width // len(POOL_WINDOWS)
    mixed = []
    for gi, w in enumerate(POOL_WINDOWS):
        cols = slice(gi * gd, (gi + 1) * gd)
        ug = u[:, cols]
        s = ug
        for k in range(1, w):
            s = s + ubuf_ref[POOL_HALO - k:POOL_HALO - k + tile, cols]
        pooled = s / jnp.minimum(t_pos, float(w)) - ug
        mixed.append(jnp.dot(pooled.astype(BF16), pool_w_ref[gi], preferred_element_type=F32))
    ubuf_ref[0:POOL_HALO, :] = ubuf_ref[tile:tile + POOL_HALO, :]
    mixed = jnp.concatenate(mixed, axis=-1) * pool_scale_ref[...]
    y_pool = jnp.dot(mixed.astype(BF16), w_bp_ref[...], preferred_element_type=F32)

    ang = invf_ref[...] * pos_ref[...].astype(F32)
    cos_t = jnp.transpose(jnp.cos(ang))
    sin_t = jnp.transpose(jnp.sin(ang) * sgn_ref[...])
    q_all = proj(pool_width, ret_width)
    k_all = proj(pool_width + ret_width, ret_width)
    v_all = proj(pool_width + 2 * ret_width, ret_width)
    g_all = proj(pool_width + 3 * ret_width, ret_width)
    gated = []
    for hh in range(RET_HEADS):
        cols = slice(hh * hd, (hh + 1) * hd)
        qh, kh, vh = q_all[:, cols], k_all[:, cols], v_all[:, cols].astype(BF16)
        qh = (qh * cos_t + pltpu.roll(qh, hd // 2, 1) * sin_t) * (hd ** -0.5)
        kh = kh * cos_t + pltpu.roll(kh, hd // 2, 1) * sin_t
        scores = lax.dot_general(qh.astype(BF16), kh.astype(BF16), (((1,), (1,)), ((), ())),
                                 preferred_element_type=F32) * dmat_ref[hh]
        o = jnp.dot(scores.astype(BF16), vh, preferred_element_type=F32)
        o = o + jnp.dot((qh * qdec_ref[hh]).astype(BF16), state_ref[hh].astype(BF16),
                        preferred_element_type=F32)
        kv = lax.dot_general((kh * kdec_ref[hh]).astype(BF16), vh, (((0,), (0,)), ((), ())),
                             preferred_element_type=F32)
        state_ref[hh] = state_ref[hh] * tile_decay[hh] + kv
        mu = jnp.mean(o, axis=-1, keepdims=True)
        oc = o - mu
        var = jnp.mean(oc * oc, axis=-1, keepdims=True)
        gh = g_all[:, cols]
        gated.append((gh * jax.nn.sigmoid(gh)) * (oc * lax.rsqrt(var + EPS)))
    y_ret = jnp.dot(jnp.concatenate(gated, axis=-1).astype(BF16), w_br_ref[...],
                    preferred_element_type=F32)

    gate_lo = pool_width + 4 * ret_width
    gate_pool = jax.nn.sigmoid(proj(gate_lo, d_model) + b_gate_ref[:, 0:d_model])
    gate_ret = jax.nn.sigmoid(proj(gate_lo + d_model, d_model) + b_gate_ref[:, d_model:2 * d_model])
    merged = (gate_pool * y_pool + gate_ret * y_ret).astype(BF16)
    y = jnp.dot(merged, w_out_ref[...], preferred_element_type=F32)
    o_ref[...] = x + gt_m * (_rms(y) * g_post_ref[...])


def _decay_tables(tile, hd):
    gamma = 1.0 - 2.0 ** (-5.0 - np.arange(RET_HEADS, dtype=np.float64))
    idx = np.arange(tile)
    dist = np.abs(idx[:, None] - idx[None, :])
    visible = (idx[None, :] // CHUNK) <= (idx[:, None] // CHUNK)
    dmat = np.where(visible[None], gamma[:, None, None] ** dist[None], 0.0)
    qdec = np.broadcast_to((gamma[:, None] ** (idx[None, :] + 1))[:, :, None], (RET_HEADS, tile, hd))
    kdec = np.broadcast_to((gamma[:, None] ** (tile - 1 - idx[None, :]))[:, :, None], (RET_HEADS, tile, hd))
    tile_decay = tuple(float(g) ** tile for g in gamma)
    as_f32 = lambda a: jnp.asarray(np.ascontiguousarray(a), dtype=F32)
    return as_f32(dmat), as_f32(qdec), as_f32(kdec), tile_decay


def _const_spec(shape):
    nd = len(shape)
    return pl.BlockSpec(shape, lambda b, t: (0,) * nd)


def _mixer(x, mod, positions, g_pre, g_post, w_in, b_gate, pool_w, pool_scale, w_bp, w_br, w_out):
    bsz, seq, d = x.shape
    tile = MIXER_TILE
    pool_width = pool_scale.shape[0]
    ret_width = w_br.shape[0]
    hd = ret_width // RET_HEADS
    assert seq % tile == 0 and tile % CHUNK == 0 and hd == LANES
    dmat, qdec, kdec, tile_decay = _decay_tables(tile, hd)
    half = hd // 2
    inv_freq = ROPE_BASE ** (-jnp.arange(half, dtype=F32) / half)
    invf = jnp.concatenate([inv_freq, inv_freq]).reshape(hd, 1)
    sgn = jnp.concatenate([-jnp.ones((half,), F32), jnp.ones((half,), F32)]).reshape(hd, 1)

    operands = [
        (x, pl.BlockSpec((None, tile, d), lambda b, t: (b, t, 0))),
        (mod.reshape(bsz, N_MOD, d), pl.BlockSpec((None, N_MOD, d), lambda b, t: (b, 0, 0))),
        (positions.reshape(bsz, 1, seq), pl.BlockSpec((None, 1, tile), lambda b, t: (b, 0, t))),
        (invf, _const_spec((hd, 1))),
        (sgn, _const_spec((hd, 1))),
        (g_pre.reshape(1, d), _const_spec((1, d))),
        (g_post.reshape(1, d), _const_spec((1, d))),
        (w_in.astype(BF16), _const_spec(w_in.shape)),
        (b_gate.reshape(1, N_BRANCHES * d), _const_spec((1, N_BRANCHES * d))),
        (pool_w.astype(BF16), _const_spec(pool_w.shape)),
        (pool_scale.reshape(1, pool_width), _const_spec((1, pool_width))),
        (w_bp.astype(BF16), _const_spec(w_bp.shape)),
        (w_br.astype(BF16), _const_spec(w_br.shape)),
        (w_out.astype(BF16), _const_spec(w_out.shape)),
        (dmat, _const_spec(dmat.shape)),
        (qdec, _const_spec(qdec.shape)),
        (kdec, _const_spec(kdec.shape)),
    ]
    return pl.pallas_call(
        functools.partial(_mixer_kernel, tile_decay=tile_decay),
        out_shape=jax.ShapeDtypeStruct(x.shape, F32),
        grid=(bsz, seq // tile),
        in_specs=[spec for _, spec in operands],
        out_specs=pl.BlockSpec((None, tile, d), lambda b, t: (b, t, 0)),
        scratch_shapes=[
            pltpu.VMEM((RET_HEADS, hd, hd), F32),
            pltpu.VMEM((POOL_HALO + tile, pool_width), F32),
        ],
        compiler_params=pltpu.CompilerParams(
            dimension_semantics=("arbitrary", "arbitrary"),
            vmem_limit_bytes=VMEM_LIMIT_BYTES),
        name="mixer",
    )(*[a for a, _ in operands])


def _ffn_kernel(x_ref, mod_ref, g_pre_ref, g_post_ref, w1_ref, w2_ref, o_ref):
    x = x_ref[...]
    sh_f, sc_f, gt_f = mod_ref[3:4, :], mod_ref[4:5, :], mod_ref[5:6, :]
    h = ((_rms(x) * g_pre_ref[...]) * (1.0 + sc_f) + sh_f).astype(BF16)
    a = jnp.maximum(jnp.dot(h, w1_ref[...], preferred_element_type=F32), 0.0)
    y = jnp.dot((a * a).astype(BF16), w2_ref[...], preferred_element_type=F32)
    o_ref[...] = x + gt_f * (_rms(y) * g_post_ref[...])


def _ffn(x, mod, g_pre, g_post, w1, w2):
    bsz, seq, d = x.shape
    tile = FFN_TILE
    assert seq % tile == 0
    return pl.pallas_call(
        _ffn_kernel,
        out_shape=jax.ShapeDtypeStruct(x.shape, F32),
        grid=(bsz, seq // tile),
        in_specs=[
            pl.BlockSpec((None, tile, d), lambda b, t: (b, t, 0)),
            pl.BlockSpec((None, N_MOD, d), lambda b, t: (b, 0, 0)),
            _const_spec((1, d)),
            _const_spec((1, d)),
            _const_spec(w1.shape),
            _const_spec(w2.shape),
        ],
        out_specs=pl.BlockSpec((None, tile, d), lambda b, t: (b, t, 0)),
        compiler_params=pltpu.CompilerParams(
            dimension_semantics=("arbitrary", "arbitrary"),
            vmem_limit_bytes=VMEM_LIMIT_BYTES),
        name="ffn",
    )(x, mod.reshape(bsz, N_MOD, d), g_pre.reshape(1, d), g_post.reshape(1, d),
      w1.astype(BF16), w2.astype(BF16))


def kernel(x, c, positions, ada_w, ada_b, mix_pre_g, mix_post_g, ffn_pre_g, ffn_post_g, w_in,
           b_branch_gate, pool_w, pool_scale, w_branch_pool, w_branch_ret, w_out, w_ff1, w_ff2):
    depth = ada_w.shape[0]
    for l in range(depth):
        mod = _adaln(c, ada_w[l], ada_b[l])
        x = _mixer(x, mod, positions, mix_pre_g[l], mix_post_g[l], w_in[l], b_branch_gate[l],
                   pool_w[l], pool_scale[l], w_branch_pool[l], w_branch_ret[l], w_out[l])
        x = _ffn(x, mod, ffn_pre_g[l], ffn_post_g[l], w_ff1[l], w_ff2[l])
    return x
```

```python
import functools
from typing import Any, NamedTuple

import jax
import jax.numpy as jnp
import numpy as np
from jax import lax
from jax.experimental import pallas as pl
from jax.experimental.pallas import tpu as pltpu

F32 = jnp.float32
BF16 = jnp.bfloat16

CHUNK = 64
POOL_WINDOWS = (2, 4, 8, 16)
POOL_HALO = 16
RET_HEADS = 4
N_BRANCHES = 2
N_MOD = 6
ROPE_BASE = 10000.0
EPS = 1e-6

LANES = 128
MIXER_TILE = 256
FFN_TILE = 512
ADALN_BLOCK_N = 1536
VMEM_LIMIT_BYTES = 56 * 1024 * 1024


def _rms(xf):
    return xf * lax.rsqrt(jnp.mean(xf * xf, axis=-1, keepdims=True) + EPS)


def _adaln_kernel(c_ref, w_ref, b_ref, o_ref):
    c = c_ref[...]
    a = (c * jax.nn.sigmoid(c)).astype(BF16)
    o_ref[...] = jnp.dot(a, w_ref[...].astype(BF16), preferred_element_type=F32) + b_ref[...]


def _adaln(c, w, b):
    bsz, d = c.shape
    n = w.shape[1]
    return pl.pallas_call(
        _adaln_kernel,
        out_shape=jax.ShapeDtypeStruct((bsz, n), F32),
        grid=(n // ADALN_BLOCK_N,),
        in_specs=[
            pl.BlockSpec((bsz, d), lambda j: (0, 0)),
            pl.BlockSpec((d, ADALN_BLOCK_N), lambda j: (0, j)),
            pl.BlockSpec((1, ADALN_BLOCK_N), lambda j: (0, j)),
        ],
        out_specs=pl.BlockSpec((bsz, ADALN_BLOCK_N), lambda j: (0, j)),
        compiler_params=pltpu.CompilerParams(dimension_semantics=("arbitrary",)),
        name="adaln",
    )(c, w, b.reshape(1, n))


def _pool_fold_kernel(pw_ref, scale_ref, wbp_ref, o_ref):
    o_ref[...] = jnp.dot(pw_ref[...] * scale_ref[...], wbp_ref[...], preferred_element_type=F32,
                         precision=lax.Precision.HIGHEST).astype(o_ref.dtype)


def _pool_fold(pool_w, pool_scale, w_bp):
    groups, gd, _ = pool_w.shape
    d = w_bp.shape[1]
    return pl.pallas_call(
        _pool_fold_kernel,
        out_shape=jax.ShapeDtypeStruct(w_bp.shape, BF16),
        grid=(groups,),
        in_specs=[
            pl.BlockSpec((None, gd, gd), lambda g: (g, 0, 0)),
            pl.BlockSpec((1, gd), lambda g: (0, g)),
            pl.BlockSpec((gd, d), lambda g: (g, 0)),
        ],
        out_specs=pl.BlockSpec((gd, d), lambda g: (g, 0)),
        compiler_params=pltpu.CompilerParams(dimension_semantics=("arbitrary",)),
        name="pool_fold",
    )(pool_w, pool_scale.reshape(1, groups * gd), w_bp)


class Staged(NamedTuple):
    rot: Any
    pooled: Any
    sg: Any
    gates: Any


class StageParams(NamedTuple):
    g_pre: Any
    w_in: Any
    b_gate: Any
    invf: Any
    qdec: Any
    kdec: Any
    halo: Any


class MixParams(NamedTuple):
    g_post: Any
    w_pool: Any
    w_br: Any
    w_out: Any
    dmat: Any
    state: Any


def _stage_phases(x_ref, rows, mod_ref, pos, seq_tile, first, sp: StageParams, out: Staged):
    tile = rows.stop - rows.start
    pool_width = out.pooled.shape[1]
    ret_width = out.sg.shape[1]
    d_model = out.gates.shape[1] // N_BRANCHES
    hd = ret_width // RET_HEADS
    sh_m, sc_m = mod_ref[0:1, :], mod_ref[1:2, :]
    h = ((_rms(x_ref[rows, :]) * sp.g_pre[...]) * (1.0 + sc_m) + sh_m).astype(BF16)
    yield

    def proj(lo, width):
        return jnp.dot(h, sp.w_in[:, lo:lo + width], preferred_element_type=F32)

    u = proj(0, pool_width)
    yield

    if first is True:
        halo = jnp.zeros(sp.halo.shape, F32)
    elif first is False:
        halo = sp.halo[...]
    else:
        halo = jnp.where(first, 0.0, sp.halo[...])
    sp.halo[...] = u[tile - POOL_HALO:tile, :]
    t_pos = (lax.broadcasted_iota(jnp.int32, (tile, LANES), 0) + (seq_tile * tile + 1)).astype(F32)
    inv_t = 1.0 / t_pos
    gd = pool_width // len(POOL_WINDOWS)
    for gi, w in enumerate(POOL_WINDOWS):
        cols = slice(gi * gd, (gi + 1) * gd)
        ug = u[:, cols]
        s = jnp.concatenate([halo[:, cols], ug], axis=0)
        k = 1
        while k < w:
            s = s + pltpu.roll(s, k, 0)
            k *= 2
        inv_cnt = jnp.where(t_pos < float(w), inv_t, 1.0 / w)
        out.pooled[:, cols] = (s[POOL_HALO:] * inv_cnt - ug).astype(BF16)
    q = proj(pool_width, ret_width)
    yield
    k_all = proj(pool_width + ret_width, ret_width)
    yield

    ang = sp.invf[...] * pos.astype(F32)
    cos_h, sin_h = jnp.cos(ang), jnp.sin(ang)
    cos_t = jnp.transpose(jnp.concatenate([cos_h, cos_h], axis=0))
    sin_t = jnp.transpose(jnp.concatenate([-sin_h, sin_h], axis=0))
    for hh in range(RET_HEADS):
        cols = slice(hh * hd, (hh + 1) * hd)
        qh, kh = q[:, cols], k_all[:, cols]
        qh = (qh * cos_t + pltpu.roll(qh, hd // 2, 1) * sin_t) * (hd ** -0.5)
        kh = kh * cos_t + pltpu.roll(kh, hd // 2, 1) * sin_t
        out.rot[0, :, cols] = qh.astype(BF16)
        out.rot[1, :, cols] = kh.astype(BF16)
        out.rot[3, :, cols] = (qh * sp.qdec[hh]).astype(BF16)
        out.rot[4, :, cols] = (kh * sp.kdec[hh]).astype(BF16)
    out.rot[2] = proj(pool_width + 2 * ret_width, ret_width).astype(BF16)
    yield
    g = proj(pool_width + 3 * ret_width, ret_width)
    out.sg[...] = g * jax.nn.sigmoid(g)
    yield
    gate_lo = pool_width + 4 * ret_width
    for br in range(N_BRANCHES):
        cols = slice(br * d_model, (br + 1) * d_model)
        out.gates[:, cols] = jax.nn.sigmoid(proj(gate_lo + br * d_model, d_model) + sp.b_gate[:, cols])
        yield


def _mix_phases(st: Staged, x_ref, rows, o_ref, mod_ref, mp: MixParams, tile_decay):
    d_model = x_ref.shape[1]
    ret_width = st.sg.shape[1]
    hd = ret_width // RET_HEADS
    gt_m = mod_ref[2:3, :]
    head_cols = [slice(hh * hd, (hh + 1) * hd) for hh in range(RET_HEADS)]

    scores = [lax.dot_general(st.rot[0, :, cols], st.rot[1, :, cols], (((1,), (1,)), ((), ())),
                              preferred_element_type=F32) for cols in head_cols]
    yield
    y_pool = jnp.dot(st.pooled[...], mp.w_pool[...], preferred_element_type=F32)
    yield
    gated = []
    for hh, cols in enumerate(head_cols):
        vh = st.rot[2, :, cols]
        o = jnp.dot((scores[hh] * mp.dmat[hh]).astype(BF16), vh, preferred_element_type=F32)
        o = o + jnp.dot(st.rot[3, :, cols], mp.state[hh].astype(BF16), preferred_element_type=F32)
        kv = lax.dot_general(st.rot[4, :, cols], vh, (((0,), (0,)), ((), ())),
                             preferred_element_type=F32)
        mp.state[hh] = mp.state[hh] * tile_decay[hh] + kv
        mu = jnp.mean(o, axis=-1, keepdims=True)
        oc = o - mu
        var = jnp.mean(oc * oc, axis=-1, keepdims=True)
        gated.append((st.sg[:, cols] * (oc * lax.rsqrt(var + EPS))).astype(BF16))
        yield
    y_ret = jnp.dot(jnp.concatenate(gated, axis=-1), mp.w_br[...], preferred_element_type=F32)
    yield
    merged = (st.gates[:, 0:d_model] * y_pool + st.gates[:, d_model:2 * d_model] * y_ret).astype(BF16)
    y = jnp.dot(merged, mp.w_out[...], preferred_element_type=F32)
    yield
    o_ref[rows, :] = x_ref[rows, :] + gt_m * (_rms(y) * mp.g_post[...])


WEAVE = "msmmmsmmssmssmssm"


def _weave(order, mix, stage):
    gens = {"m": mix, "s": stage}
    for c in order:
        next(gens[c], None)
    for g in (mix, stage):
        for _ in g:
            pass


def _mixer_kernel(x_ref, xn_ref, mod_ref, modn_ref, pos_ref, posn_ref, invf_ref, g_pre_ref, g_post_ref,
                  w_in_ref, b_gate_ref, w_pool_ref, w_br_ref, w_out_ref, dmat_ref, qdec_ref, kdec_ref,
                  o_ref, state_ref, halo_ref, rot0, pooled0, sg0, gates0, rot1, pooled1, sg1, gates1,
                  *, tile, steps_per_seq, n_steps, tile_decay):
    j = pl.program_id(0)
    s = j % steps_per_seq
    s_next = jnp.minimum(j + 1, n_steps - 1) % steps_per_seq
    sp = StageParams(g_pre_ref, w_in_ref, b_gate_ref, invf_ref, qdec_ref, kdec_ref, halo_ref)
    mp = MixParams(g_post_ref, w_pool_ref, w_br_ref, w_out_ref, dmat_ref, state_ref)
    st0 = Staged(rot0, pooled0, sg0, gates0)
    st1 = Staged(rot1, pooled1, sg1, gates1)
    lo, hi, whole = slice(0, tile), slice(tile, 2 * tile), slice(0, tile)

    @pl.when(j == 0)
    def _():
        for _ in _stage_phases(x_ref, lo, mod_ref, pos_ref[:, lo], 0, True, sp, st0):
            pass

    @pl.when(s == 0)
    def _():
        state_ref[...] = jnp.zeros_like(state_ref)

    _weave(WEAVE,
           _mix_phases(st0, x_ref, lo, o_ref, mod_ref, mp, tile_decay),
           _stage_phases(x_ref, hi, mod_ref, pos_ref[:, hi], 2 * s + 1, False, sp, st1))
    _weave(WEAVE,
           _mix_phases(st1, x_ref, hi, o_ref, mod_ref, mp, tile_decay),
           _stage_phases(xn_ref, whole, modn_ref, posn_ref[...], 2 * s_next, s_next == 0, sp, st0))


def _decay_tables(tile, hd):
    gamma = 1.0 - 2.0 ** (-5.0 - np.arange(RET_HEADS, dtype=np.float64))
    idx = np.arange(tile)
    dist = np.abs(idx[:, None] - idx[None, :])
    visible = (idx[None, :] // CHUNK) <= (idx[:, None] // CHUNK)
    dmat = np.where(visible[None], gamma[:, None, None] ** dist[None], 0.0)
    qdec = np.broadcast_to((gamma[:, None] ** (idx[None, :] + 1))[:, :, None], (RET_HEADS, tile, hd))
    kdec = np.broadcast_to((gamma[:, None] ** (tile - 1 - idx[None, :]))[:, :, None], (RET_HEADS, tile, hd))
    tile_decay = tuple(float(g) ** tile for g in gamma)
    as_f32 = lambda a: jnp.asarray(np.ascontiguousarray(a), dtype=F32)
    return as_f32(dmat), as_f32(qdec), as_f32(kdec), tile_decay


def _const_spec(shape, grid_rank):
    nd = len(shape)
    if grid_rank == 1:
        return pl.BlockSpec(shape, lambda j: (0,) * nd)
    return pl.BlockSpec(shape, lambda b, t: (0,) * nd)


def _mixer(x, mod, positions, g_pre, g_post, w_in, b_gate, w_pool, w_br, w_out):
    bsz, seq, d = x.shape
    tile = MIXER_TILE
    pool_width = w_pool.shape[0]
    ret_width = w_br.shape[0]
    hd = ret_width // RET_HEADS
    assert seq % (2 * tile) == 0 and tile % CHUNK == 0 and hd == LANES
    steps_per_seq = seq // (2 * tile)
    n_steps = bsz * steps_per_seq
    dmat, qdec, kdec, tile_decay = _decay_tables(tile, hd)
    half = hd // 2
    invf = (ROPE_BASE ** (-jnp.arange(half, dtype=F32) / half)).reshape(half, 1)
    mod3 = mod.reshape(bsz, N_MOD, d)
    pos3 = positions.reshape(bsz, 1, seq)

    def cur(j):
        return j // steps_per_seq, j % steps_per_seq

    def nxt(j):
        return cur(jnp.minimum(j + 1, n_steps - 1))

    const = functools.partial(_const_spec, grid_rank=1)
    operands = [
        (x, pl.BlockSpec((None, 2 * tile, d), lambda j: (*cur(j), 0))),
        (x, pl.BlockSpec((None, tile, d), lambda j: (nxt(j)[0], 2 * nxt(j)[1], 0))),
        (mod3, pl.BlockSpec((None, N_MOD, d), lambda j: (cur(j)[0], 0, 0))),
        (mod3, pl.BlockSpec((None, N_MOD, d), lambda j: (nxt(j)[0], 0, 0))),
        (pos3, pl.BlockSpec((None, 1, 2 * tile), lambda j: (cur(j)[0], 0, cur(j)[1]))),
        (pos3, pl.BlockSpec((None, 1, tile), lambda j: (nxt(j)[0], 0, 2 * nxt(j)[1]))),
        (invf, const((half, 1))),
        (g_pre.reshape(1, d), const((1, d))),
        (g_post.reshape(1, d), const((1, d))),
        (w_in.astype(BF16), const(w_in.shape)),
        (b_gate.reshape(1, N_BRANCHES * d), const((1, N_BRANCHES * d))),
        (w_pool, const(w_pool.shape)),
        (w_br.astype(BF16), const(w_br.shape)),
        (w_out.astype(BF16), const(w_out.shape)),
        (dmat, const(dmat.shape)),
        (qdec, const(qdec.shape)),
        (kdec, const(kdec.shape)),
    ]
    staged = [
        pltpu.VMEM((5, tile, ret_width), BF16),
        pltpu.VMEM((tile, pool_width), BF16),
        pltpu.VMEM((tile, ret_width), F32),
        pltpu.VMEM((tile, N_BRANCHES * d), F32),
    ]
    return pl.pallas_call(
        functools.partial(_mixer_kernel, tile=tile, steps_per_seq=steps_per_seq, n_steps=n_steps,
                          tile_decay=tile_decay),
        out_shape=jax.ShapeDtypeStruct(x.shape, F32),
        grid=(n_steps,),
        in_specs=[spec for _, spec in operands],
        out_specs=pl.BlockSpec((None, 2 * tile, d), lambda j: (*cur(j), 0)),
        scratch_shapes=[
            pltpu.VMEM((RET_HEADS, hd, hd), F32),
            pltpu.VMEM((POOL_HALO, pool_width), F32),
        ] + staged + staged,
        compiler_params=pltpu.CompilerParams(
            dimension_semantics=("arbitrary",),
            vmem_limit_bytes=VMEM_LIMIT_BYTES),
        name="mixer",
    )(*[a for a, _ in operands])


def _ffn_kernel(x_ref, mod_ref, g_pre_ref, g_post_ref, w1_ref, w2_ref, o_ref):
    x = x_ref[...]
    sh_f, sc_f, gt_f = mod_ref[3:4, :], mod_ref[4:5, :], mod_ref[5:6, :]
    h = ((_rms(x) * g_pre_ref[...]) * (1.0 + sc_f) + sh_f).astype(BF16)
    a = jnp.maximum(jnp.dot(h, w1_ref[...], preferred_element_type=F32), 0.0)
    y = jnp.dot((a * a).astype(BF16), w2_ref[...], preferred_element_type=F32)
    o_ref[...] = x + gt_f * (_rms(y) * g_post_ref[...])


def _ffn(x, mod, g_pre, g_post, w1, w2):
    bsz, seq, d = x.shape
    tile = FFN_TILE
    assert seq % tile == 0
    const = functools.partial(_const_spec, grid_rank=2)
    return pl.pallas_call(
        _ffn_kernel,
        out_shape=jax.ShapeDtypeStruct(x.shape, F32),
        grid=(bsz, seq // tile),
        in_specs=[
            pl.BlockSpec((None, tile, d), lambda b, t: (b, t, 0)),
            pl.BlockSpec((None, N_MOD, d), lambda b, t: (b, 0, 0)),
            const((1, d)),
            const((1, d)),
            const(w1.shape),
            const(w2.shape),
        ],
        out_specs=pl.BlockSpec((None, tile, d), lambda b, t: (b, t, 0)),
        compiler_params=pltpu.CompilerParams(
            dimension_semantics=("arbitrary", "arbitrary"),
            vmem_limit_bytes=VMEM_LIMIT_BYTES),
        name="ffn",
    )(x, mod.reshape(bsz, N_MOD, d), g_pre.reshape(1, d), g_post.reshape(1, d),
      w1.astype(BF16), w2.astype(BF16))


def kernel(x, c, positions, ada_w, ada_b, mix_pre_g, mix_post_g, ffn_pre_g, ffn_post_g, w_in,
           b_branch_gate, pool_w, pool_scale, w_branch_pool, w_branch_ret, w_out, w_ff1, w_ff2):
    depth = ada_w.shape[0]
    for l in range(depth):
        mod = _adaln(c, ada_w[l], ada_b[l])
        w_pool = _pool_fold(pool_w[l], pool_scale[l], w_branch_pool[l])
        x = _mixer(x, mod, positions, mix_pre_g[l], mix_post_g[l], w_in[l], b_branch_gate[l],
                   w_pool, w_branch_ret[l], w_out[l])
        x = _ffn(x, mod, ffn_pre_g[l], ffn_post_g[l], w_ff1[l], w_ff2[l])
    return x
```

```python
import functools
from typing import Any, NamedTuple

import jax
import jax.numpy as jnp
import numpy as np
from jax import lax
from jax.experimental import pallas as pl
from jax.experimental.pallas import tpu as pltpu

F32 = jnp.float32
BF16 = jnp.bfloat16

CHUNK = 64
POOL_WINDOWS = (2, 4, 8, 16)
POOL_HALO = 16
RET_HEADS = 4
N_BRANCHES = 2
N_MOD = 6
ROPE_BASE = 10000.0
EPS = 1e-6

LANES = 128
MIXER_TILE = 256
FFN_TILE = 256
ADALN_BLOCK_N = 1536
VMEM_LIMIT_BYTES = 56 * 1024 * 1024


def _rms(xf):
    return xf * lax.rsqrt(jnp.mean(xf * xf, axis=-1, keepdims=True) + EPS)


def _adaln_kernel(c_ref, w_ref, b_ref, o_ref):
    c = c_ref[...]
    a = (c * jax.nn.sigmoid(c)).astype(BF16)
    o_ref[...] = jnp.dot(a, w_ref[...].astype(BF16), preferred_element_type=F32) + b_ref[...]


def _adaln(c, w, b):
    bsz, d = c.shape
    n = w.shape[1]
    return pl.pallas_call(
        _adaln_kernel,
        out_shape=jax.ShapeDtypeStruct((bsz, n), F32),
        grid=(n // ADALN_BLOCK_N,),
        in_specs=[
            pl.BlockSpec((bsz, d), lambda j: (0, 0)),
            pl.BlockSpec((d, ADALN_BLOCK_N), lambda j: (0, j)),
            pl.BlockSpec((1, ADALN_BLOCK_N), lambda j: (0, j)),
        ],
        out_specs=pl.BlockSpec((bsz, ADALN_BLOCK_N), lambda j: (0, j)),
        compiler_params=pltpu.CompilerParams(dimension_semantics=("arbitrary",)),
        name="adaln",
    )(c, w, b.reshape(1, n))


def _pool_fold_kernel(pw_ref, scale_ref, wbp_ref, o_ref):
    o_ref[...] = jnp.dot(pw_ref[...] * scale_ref[...], wbp_ref[...], preferred_element_type=F32,
                         precision=lax.Precision.HIGHEST).astype(o_ref.dtype)


def _pool_fold(pool_w, pool_scale, w_bp):
    groups, gd, _ = pool_w.shape
    d = w_bp.shape[1]
    return pl.pallas_call(
        _pool_fold_kernel,
        out_shape=jax.ShapeDtypeStruct(w_bp.shape, BF16),
        grid=(groups,),
        in_specs=[
            pl.BlockSpec((None, gd, gd), lambda g: (g, 0, 0)),
            pl.BlockSpec((1, gd), lambda g: (0, g)),
            pl.BlockSpec((gd, d), lambda g: (g, 0)),
        ],
        out_specs=pl.BlockSpec((gd, d), lambda g: (g, 0)),
        compiler_params=pltpu.CompilerParams(dimension_semantics=("arbitrary",)),
        name="pool_fold",
    )(pool_w, pool_scale.reshape(1, groups * gd), w_bp)


class Staged(NamedTuple):
    rot: Any
    pooled: Any
    sg: Any
    gates: Any


class StageParams(NamedTuple):
    g_pre: Any
    w_in: Any
    b_gate: Any
    invf: Any
    qdec: Any
    kdec: Any
    halo: Any


class MixParams(NamedTuple):
    g_post: Any
    w_pool: Any
    w_br: Any
    w_out: Any
    dmat: Any
    state: Any


def _stage_phases(x_ref, rows, mod_ref, pos, seq_tile, first, sp: StageParams, out: Staged):
    tile = rows.stop - rows.start
    pool_width = out.pooled.shape[1]
    ret_width = out.sg.shape[1]
    d_model = out.gates.shape[1] // N_BRANCHES
    hd = ret_width // RET_HEADS
    sh_m, sc_m = mod_ref[0:1, :], mod_ref[1:2, :]
    h = ((_rms(x_ref[rows, :]) * sp.g_pre[...]) * (1.0 + sc_m) + sh_m).astype(BF16)
    yield

    def proj(lo, width):
        return jnp.dot(h, sp.w_in[:, lo:lo + width], preferred_element_type=F32)

    u = proj(0, pool_width)
    yield

    if first is True:
        halo = jnp.zeros(sp.halo.shape, F32)
    elif first is False:
        halo = sp.halo[...]
    else:
        halo = jnp.where(first, 0.0, sp.halo[...])
    sp.halo[...] = u[tile - POOL_HALO:tile, :]
    t_pos = (lax.broadcasted_iota(jnp.int32, (tile, LANES), 0) + (seq_tile * tile + 1)).astype(F32)
    inv_t = 1.0 / t_pos
    gd = pool_width // len(POOL_WINDOWS)
    for gi, w in enumerate(POOL_WINDOWS):
        cols = slice(gi * gd, (gi + 1) * gd)
        ug = u[:, cols]
        s = jnp.concatenate([halo[:, cols], ug], axis=0)
        k = 1
        while k < w:
            s = s + pltpu.roll(s, k, 0)
            k *= 2
        inv_cnt = jnp.where(t_pos < float(w), inv_t, 1.0 / w)
        out.pooled[:, cols] = (s[POOL_HALO:] * inv_cnt - ug).astype(BF16)
    q = proj(pool_width, ret_width)
    yield
    k_all = proj(pool_width + ret_width, ret_width)
    yield

    ang = sp.invf[...] * pos.astype(F32)
    cos_h, sin_h = jnp.cos(ang), jnp.sin(ang)
    cos_t = jnp.transpose(jnp.concatenate([cos_h, cos_h], axis=0))
    sin_t = jnp.transpose(jnp.concatenate([-sin_h, sin_h], axis=0))
    for hh in range(RET_HEADS):
        cols = slice(hh * hd, (hh + 1) * hd)
        qh, kh = q[:, cols], k_all[:, cols]
        qh = (qh * cos_t + pltpu.roll(qh, hd // 2, 1) * sin_t) * (hd ** -0.5)
        kh = kh * cos_t + pltpu.roll(kh, hd // 2, 1) * sin_t
        out.rot[0, :, cols] = qh.astype(BF16)
        out.rot[1, :, cols] = kh.astype(BF16)
        out.rot[3, :, cols] = (qh * sp.qdec[hh]).astype(BF16)
        out.rot[4, :, cols] = (kh * sp.kdec[hh]).astype(BF16)
    out.rot[2] = proj(pool_width + 2 * ret_width, ret_width).astype(BF16)
    yield
    g = proj(pool_width + 3 * ret_width, ret_width)
    out.sg[...] = g * jax.nn.sigmoid(g)
    yield
    gate_lo = pool_width + 4 * ret_width
    for br in range(N_BRANCHES):
        cols = slice(br * d_model, (br + 1) * d_model)
        out.gates[:, cols] = jax.nn.sigmoid(proj(gate_lo + br * d_model, d_model) + sp.b_gate[:, cols])
        yield


def _mix_phases(st: Staged, x_ref, rows, o_ref, mod_ref, mp: MixParams, tile_decay):
    d_model = x_ref.shape[1]
    ret_width = st.sg.shape[1]
    hd = ret_width // RET_HEADS
    gt_m = mod_ref[2:3, :]
    head_cols = [slice(hh * hd, (hh + 1) * hd) for hh in range(RET_HEADS)]

    scores = [lax.dot_general(st.rot[0, :, cols], st.rot[1, :, cols], (((1,), (1,)), ((), ())),
                              preferred_element_type=F32) for cols in head_cols]
    yield
    y_pool = jnp.dot(st.pooled[...], mp.w_pool[...], preferred_element_type=F32)
    yield
    gated = []
    for hh, cols in enumerate(head_cols):
        vh = st.rot[2, :, cols]
        o = jnp.dot((scores[hh] * mp.dmat[hh]).astype(BF16), vh, preferred_element_type=F32)
        o = o + jnp.dot(st.rot[3, :, cols], mp.state[hh].astype(BF16), preferred_element_type=F32)
        kv = lax.dot_general(st.rot[4, :, cols], vh, (((0,), (0,)), ((), ())),
                             preferred_element_type=F32)
        mp.state[hh] = mp.state[hh] * tile_decay[hh] + kv
        mu = jnp.mean(o, axis=-1, keepdims=True)
        oc = o - mu
        var = jnp.mean(oc * oc, axis=-1, keepdims=True)
        gated.append((st.sg[:, cols] * (oc * lax.rsqrt(var + EPS))).astype(BF16))
        yield
    y_ret = jnp.dot(jnp.concatenate(gated, axis=-1), mp.w_br[...], preferred_element_type=F32)
    yield
    merged = (st.gates[:, 0:d_model] * y_pool + st.gates[:, d_model:2 * d_model] * y_ret).astype(BF16)
    y = jnp.dot(merged, mp.w_out[...], preferred_element_type=F32)
    yield
    o_ref[rows, :] = x_ref[rows, :] + gt_m * (_rms(y) * mp.g_post[...])


WEAVE = "msmmmsmmssmssmssm"


def _weave(order, mix, stage):
    gens = {"m": mix, "s": stage}
    for c in order:
        next(gens[c], None)
    for g in (mix, stage):
        for _ in g:
            pass


def _mixer_kernel(x_ref, xn_ref, mod_ref, modn_ref, pos_ref, posn_ref, invf_ref, g_pre_ref, g_post_ref,
                  w_in_ref, b_gate_ref, w_pool_ref, w_br_ref, w_out_ref, dmat_ref, qdec_ref, kdec_ref,
                  o_ref, state_ref, halo_ref, rot0, pooled0, sg0, gates0, rot1, pooled1, sg1, gates1,
                  *, tile, steps_per_seq, n_steps, tile_decay):
    j = pl.program_id(0)
    s = j % steps_per_seq
    s_next = jnp.minimum(j + 1, n_steps - 1) % steps_per_seq
    sp = StageParams(g_pre_ref, w_in_ref, b_gate_ref, invf_ref, qdec_ref, kdec_ref, halo_ref)
    mp = MixParams(g_post_ref, w_pool_ref, w_br_ref, w_out_ref, dmat_ref, state_ref)
    st0 = Staged(rot0, pooled0, sg0, gates0)
    st1 = Staged(rot1, pooled1, sg1, gates1)
    lo, hi, whole = slice(0, tile), slice(tile, 2 * tile), slice(0, tile)

    @pl.when(j == 0)
    def _():
        for _ in _stage_phases(x_ref, lo, mod_ref, pos_ref[:, lo], 0, True, sp, st0):
            pass

    @pl.when(s == 0)
    def _():
        state_ref[...] = jnp.zeros_like(state_ref)

    _weave(WEAVE,
           _mix_phases(st0, x_ref, lo, o_ref, mod_ref, mp, tile_decay),
           _stage_phases(x_ref, hi, mod_ref, pos_ref[:, hi], 2 * s + 1, False, sp, st1))
    _weave(WEAVE,
           _mix_phases(st1, x_ref, hi, o_ref, mod_ref, mp, tile_decay),
           _stage_phases(xn_ref, whole, modn_ref, posn_ref[...], 2 * s_next, s_next == 0, sp, st0))


def _decay_tables(tile, hd):
    gamma = 1.0 - 2.0 ** (-5.0 - np.arange(RET_HEADS, dtype=np.float64))
    idx = np.arange(tile)
    dist = np.abs(idx[:, None] - idx[None, :])
    visible = (idx[None, :] // CHUNK) <= (idx[:, None] // CHUNK)
    dmat = np.where(visible[None], gamma[:, None, None] ** dist[None], 0.0)
    qdec = np.broadcast_to((gamma[:, None] ** (idx[None, :] + 1))[:, :, None], (RET_HEADS, tile, hd))
    kdec = np.broadcast_to((gamma[:, None] ** (tile - 1 - idx[None, :]))[:, :, None], (RET_HEADS, tile, hd))
    tile_decay = tuple(float(g) ** tile for g in gamma)
    as_f32 = lambda a: jnp.asarray(np.ascontiguousarray(a), dtype=F32)
    return as_f32(dmat), as_f32(qdec), as_f32(kdec), tile_decay


def _const_spec(shape, grid_rank):
    nd = len(shape)
    if grid_rank == 1:
        return pl.BlockSpec(shape, lambda j: (0,) * nd)
    return pl.BlockSpec(shape, lambda b, t: (0,) * nd)


def _mixer(x, mod, positions, g_pre, g_post, w_in, b_gate, w_pool, w_br, w_out):
    bsz, seq, d = x.shape
    tile = MIXER_TILE
    pool_width = w_pool.shape[0]
    ret_width = w_br.shape[0]
    hd = ret_width // RET_HEADS
    assert seq % (2 * tile) == 0 and tile % CHUNK == 0 and hd == LANES
    steps_per_seq = seq // (2 * tile)
    n_steps = bsz * steps_per_seq
    dmat, qdec, kdec, tile_decay = _decay_tables(tile, hd)
    half = hd // 2
    invf = (ROPE_BASE ** (-jnp.arange(half, dtype=F32) / half)).reshape(half, 1)
    mod3 = mod.reshape(bsz, N_MOD, d)
    pos3 = positions.reshape(bsz, 1, seq)

    def cur(j):
        return j // steps_per_seq, j % steps_per_seq

    def nxt(j):
        return cur(jnp.minimum(j + 1, n_steps - 1))

    const = functools.partial(_const_spec, grid_rank=1)
    operands = [
        (x, pl.BlockSpec((None, 2 * tile, d), lambda j: (*cur(j), 0))),
        (x, pl.BlockSpec((None, tile, d), lambda j: (nxt(j)[0], 2 * nxt(j)[1], 0))),
        (mod3, pl.BlockSpec((None, N_MOD, d), lambda j: (cur(j)[0], 0, 0))),
        (mod3, pl.BlockSpec((None, N_MOD, d), lambda j: (nxt(j)[0], 0, 0))),
        (pos3, pl.BlockSpec((None, 1, 2 * tile), lambda j: (cur(j)[0], 0, cur(j)[1]))),
        (pos3, pl.BlockSpec((None, 1, tile), lambda j: (nxt(j)[0], 0, 2 * nxt(j)[1]))),
        (invf, const((half, 1))),
        (g_pre.reshape(1, d), const((1, d))),
        (g_post.reshape(1, d), const((1, d))),
        (w_in.astype(BF16), const(w_in.shape)),
        (b_gate.reshape(1, N_BRANCHES * d), const((1, N_BRANCHES * d))),
        (w_pool, const(w_pool.shape)),
        (w_br.astype(BF16), const(w_br.shape)),
        (w_out.astype(BF16), const(w_out.shape)),
        (dmat, const(dmat.shape)),
        (qdec, const(qdec.shape)),
        (kdec, const(kdec.shape)),
    ]
    staged = [
        pltpu.VMEM((5, tile, ret_width), BF16),
        pltpu.VMEM((tile, pool_width), BF16),
        pltpu.VMEM((tile, ret_width), F32),
        pltpu.VMEM((tile, N_BRANCHES * d), F32),
    ]
    return pl.pallas_call(
        functools.partial(_mixer_kernel, tile=tile, steps_per_seq=steps_per_seq, n_steps=n_steps,
                          tile_decay=tile_decay),
        out_shape=jax.ShapeDtypeStruct(x.shape, F32),
        grid=(n_steps,),
        in_specs=[spec for _, spec in operands],
        out_specs=pl.BlockSpec((None, 2 * tile, d), lambda j: (*cur(j), 0)),
        scratch_shapes=[
            pltpu.VMEM((RET_HEADS, hd, hd), F32),
            pltpu.VMEM((POOL_HALO, pool_width), F32),
        ] + staged + staged,
        compiler_params=pltpu.CompilerParams(
            dimension_semantics=("arbitrary",),
            vmem_limit_bytes=VMEM_LIMIT_BYTES),
        name="mixer",
    )(*[a for a, _ in operands])


def _ffn_kernel(x_ref, mod_ref, g_pre_ref, g_post_ref, w1_ref, w2_ref, o_ref):
    x = x_ref[...]
    sh_f, sc_f, gt_f = mod_ref[3:4, :], mod_ref[4:5, :], mod_ref[5:6, :]
    h = ((_rms(x) * g_pre_ref[...]) * (1.0 + sc_f) + sh_f).astype(BF16)
    a = jnp.maximum(jnp.dot(h, w1_ref[...], preferred_element_type=F32), 0.0)
    y = jnp.dot((a * a).astype(BF16), w2_ref[...], preferred_element_type=F32)
    o_ref[...] = x + gt_f * (_rms(y) * g_post_ref[...])


def _ffn(x, mod, g_pre, g_post, w1, w2):
    bsz, seq, d = x.shape
    tile = FFN_TILE
    assert seq % tile == 0
    const = functools.partial(_const_spec, grid_rank=2)
    return pl.pallas_call(
        _ffn_kernel,
        out_shape=jax.ShapeDtypeStruct(x.shape, F32),
        grid=(bsz, seq // tile),
        in_specs=[
            pl.BlockSpec((None, tile, d), lambda b, t: (b, t, 0)),
            pl.BlockSpec((None, N_MOD, d), lambda b, t: (b, 0, 0)),
            const((1, d)),
            const((1, d)),
            const(w1.shape),
            const(w2.shape),
        ],
        out_specs=pl.BlockSpec((None, tile, d), lambda b, t: (b, t, 0)),
        compiler_params=pltpu.CompilerParams(
            dimension_semantics=("arbitrary", "arbitrary"),
            vmem_limit_bytes=VMEM_LIMIT_BYTES),
        name="ffn",
    )(x, mod.reshape(bsz, N_MOD, d), g_pre.reshape(1, d), g_post.reshape(1, d),
      w1.astype(BF16), w2.astype(BF16))


def kernel(x, c, positions, ada_w, ada_b, mix_pre_g, mix_post_g, ffn_pre_g, ffn_post_g, w_in,
           b_branch_gate, pool_w, pool_scale, w_branch_pool, w_branch_ret, w_out, w_ff1, w_ff2):
    depth = ada_w.shape[0]
    for l in range(depth):
        mod = _adaln(c, ada_w[l], ada_b[l])
        w_pool = _pool_fold(pool_w[l], pool_scale[l], w_branch_pool[l])
        x = _mixer(x, mod, positions, mix_pre_g[l], mix_post_g[l], w_in[l], b_branch_gate[l],
                   w_pool, w_branch_ret[l], w_out[l])
        x = _ffn(x, mod, ffn_pre_g[l], ffn_post_g[l], w_ff1[l], w_ff2[l])
    return x
```

```python
import functools
from typing import Any, NamedTuple

import jax
import jax.numpy as jnp
import numpy as np
from jax import lax
from jax.experimental import pallas as pl
from jax.experimental.pallas import tpu as pltpu

F32 = jnp.float32
BF16 = jnp.bfloat16

CHUNK = 64
POOL_WINDOWS = (2, 4, 8, 16)
POOL_HALO = 16
RET_HEADS = 4
N_BRANCHES = 2
N_MOD = 6
ROPE_BASE = 10000.0
EPS = 1e-6

LANES = 128
MXU_COLS = 256
MIXER_TILE = 256
FFN_TILE = 512
ADALN_BLOCK_N = 1536
VMEM_LIMIT_BYTES = 56 * 1024 * 1024


def _rms(xf):
    return xf * lax.rsqrt(jnp.mean(xf * xf, axis=-1, keepdims=True) + EPS)


def _adaln_kernel(c_ref, w_ref, b_ref, o_ref):
    c = c_ref[...]
    a = (c * jax.nn.sigmoid(c)).astype(BF16)
    o_ref[...] = jnp.dot(a, w_ref[...].astype(BF16), preferred_element_type=F32) + b_ref[...]


def _adaln(c, w, b):
    bsz, d = c.shape
    n = w.shape[1]
    return pl.pallas_call(
        _adaln_kernel,
        out_shape=jax.ShapeDtypeStruct((bsz, n), F32),
        grid=(n // ADALN_BLOCK_N,),
        in_specs=[
            pl.BlockSpec((bsz, d), lambda j: (0, 0)),
            pl.BlockSpec((d, ADALN_BLOCK_N), lambda j: (0, j)),
            pl.BlockSpec((1, ADALN_BLOCK_N), lambda j: (0, j)),
        ],
        out_specs=pl.BlockSpec((bsz, ADALN_BLOCK_N), lambda j: (0, j)),
        compiler_params=pltpu.CompilerParams(dimension_semantics=("arbitrary",)),
        name="adaln",
    )(c, w, b.reshape(1, n))


def _pool_fold_kernel(pw_ref, scale_ref, wbp_ref, o_ref):
    o_ref[...] = jnp.dot(pw_ref[...] * scale_ref[...], wbp_ref[...], preferred_element_type=F32,
                         precision=lax.Precision.HIGHEST).astype(o_ref.dtype)


def _pool_fold(pool_w, pool_scale, w_bp):
    groups, gd, _ = pool_w.shape
    d = w_bp.shape[1]
    return pl.pallas_call(
        _pool_fold_kernel,
        out_shape=jax.ShapeDtypeStruct(w_bp.shape, BF16),
        grid=(groups,),
        in_specs=[
            pl.BlockSpec((None, gd, gd), lambda g: (g, 0, 0)),
            pl.BlockSpec((1, gd), lambda g: (0, g)),
            pl.BlockSpec((gd, d), lambda g: (g, 0)),
        ],
        out_specs=pl.BlockSpec((gd, d), lambda g: (g, 0)),
        compiler_params=pltpu.CompilerParams(dimension_semantics=("arbitrary",)),
        name="pool_fold",
    )(pool_w, pool_scale.reshape(1, groups * gd), w_bp)


class Staged(NamedTuple):
    rot: Any
    pooled: Any
    sg: Any
    gates: Any


class StageParams(NamedTuple):
    w_in: Any
    b_gate: Any
    invf: Any
    qdec: Any
    kdec: Any
    halo: Any


class MixParams(NamedTuple):
    g_post: Any
    w_pool: Any
    w_br: Any
    w_out: Any
    dmat: Any
    state: Any


def _norm_tile(x, mod_ref, g_pre_ref, h_ref):
    sh_m, sc_m = mod_ref[0:1, :], mod_ref[1:2, :]
    h_ref[...] = ((_rms(x) * g_pre_ref[...]) * (1.0 + sc_m) + sh_m).astype(BF16)


def _stage_phases(h_ref, pos, seq_tile, first, sp: StageParams, out: Staged):
    tile = h_ref.shape[0]
    pool_width = out.pooled.shape[1]
    ret_width = out.sg.shape[1]
    hd = ret_width // RET_HEADS
    gd = pool_width // len(POOL_WINDOWS)
    wc = 2 * MXU_COLS

    def proj(lo):
        return jnp.dot(h_ref[...], sp.w_in[:, lo:lo + wc], preferred_element_type=F32)

    ang = sp.invf[...] * pos.astype(F32)
    cos_h, sin_h = jnp.cos(ang), jnp.sin(ang)
    cos_t = jnp.transpose(jnp.concatenate([cos_h, cos_h], axis=0))
    sin_t = jnp.transpose(jnp.concatenate([-sin_h, sin_h], axis=0))
    t_pos = (lax.broadcasted_iota(jnp.int32, (tile, LANES), 0) + (seq_tile * tile + 1)).astype(F32)
    inv_t = 1.0 / t_pos
    yield

    def pool_chunk(c):
        cols = slice(c * wc, (c + 1) * wc)
        u = proj(c * wc)
        if first is True:
            halo = jnp.zeros((POOL_HALO, wc), F32)
        elif first is False:
            halo = sp.halo[:, cols]
        else:
            halo = jnp.where(first, 0.0, sp.halo[:, cols])
        sp.halo[:, cols] = u[tile - POOL_HALO:tile, :]
        for gl in range(wc // gd):
            w = POOL_WINDOWS[c * (wc // gd) + gl]
            ug = u[:, gl * gd:(gl + 1) * gd]
            s = jnp.concatenate([halo[:, gl * gd:(gl + 1) * gd], ug], axis=0)
            k = 1
            while k < w:
                s = s + pltpu.roll(s, k, 0)
                k *= 2
            inv_cnt = jnp.where(t_pos < float(w), inv_t, 1.0 / w)
            out.pooled[:, c * wc + gl * gd:c * wc + (gl + 1) * gd] = (
                s[POOL_HALO:] * inv_cnt - ug).astype(BF16)

    def rope_chunk(c, lo, plain, decayed, dec_ref, scale):
        y = proj(lo + c * wc)
        for hl in range(wc // hd):
            hh = c * (wc // hd) + hl
            yh = y[:, hl * hd:(hl + 1) * hd]
            yh = yh * cos_t + pltpu.roll(yh, hd // 2, 1) * sin_t
            if scale != 1.0:
                yh = yh * scale
            out.rot[plain, :, hh * hd:(hh + 1) * hd] = yh.astype(BF16)
            out.rot[decayed, :, hh * hd:(hh + 1) * hd] = (yh * dec_ref[hh]).astype(BF16)

    def v_chunk(c):
        out.rot[2, :, c * wc:(c + 1) * wc] = proj(pool_width + 2 * ret_width + c * wc).astype(BF16)

    def g_chunk(c):
        g = proj(pool_width + 3 * ret_width + c * wc)
        out.sg[:, c * wc:(c + 1) * wc] = g * jax.nn.sigmoid(g)

    def gate_chunk(c):
        cols = slice(c * wc, (c + 1) * wc)
        out.gates[:, cols] = jax.nn.sigmoid(proj(pool_width + 4 * ret_width + c * wc) + sp.b_gate[:, cols])

    q_chunk = functools.partial(rope_chunk, lo=pool_width, plain=0, decayed=3, dec_ref=sp.qdec,
                                scale=hd ** -0.5)
    k_chunk = functools.partial(rope_chunk, lo=pool_width + ret_width, plain=1, decayed=4,
                                dec_ref=sp.kdec, scale=1.0)
    narrow = ([functools.partial(pool_chunk, c) for c in range(pool_width // wc)]
              + [functools.partial(f, c) for f in (q_chunk, k_chunk, v_chunk, g_chunk)
                 for c in range(ret_width // wc)])
    wide = [functools.partial(gate_chunk, c) for c in range(out.gates.shape[1] // wc)]
    order = []
    while narrow or wide:
        if wide:
            order.append(wide.pop(0))
        if narrow:
            order.append(narrow.pop(0))
    for chunk in order:
        chunk()
        yield


def _mix_phases(st: Staged, x_ref, rows, o_ref, mod_ref, mp: MixParams, tile_decay):
    d_model = x_ref.shape[1]
    ret_width = st.sg.shape[1]
    hd = ret_width // RET_HEADS
    gt_m = mod_ref[2:3, :]
    head_cols = [slice(hh * hd, (hh + 1) * hd) for hh in range(RET_HEADS)]

    scores = []
    for cols in head_cols:
        scores.append(lax.dot_general(st.rot[0, :, cols], st.rot[1, :, cols], (((1,), (1,)), ((), ())),
                                      preferred_element_type=F32))
        yield
    outs = []
    for hh, cols in enumerate(head_cols):
        vh = st.rot[2, :, cols]
        o = jnp.dot((scores[hh] * mp.dmat[hh]).astype(BF16), vh, preferred_element_type=F32)
        outs.append(o + jnp.dot(st.rot[3, :, cols], mp.state[hh].astype(BF16),
                                preferred_element_type=F32))
        yield
        kv = lax.dot_general(st.rot[4, :, cols], vh, (((0,), (0,)), ((), ())),
                             preferred_element_type=F32)
        mp.state[hh] = mp.state[hh] * tile_decay[hh] + kv
        yield
    gated = []
    for o, cols in zip(outs, head_cols):
        mu = jnp.mean(o, axis=-1, keepdims=True)
        oc = o - mu
        var = jnp.mean(oc * oc, axis=-1, keepdims=True)
        gated.append((st.sg[:, cols] * (oc * lax.rsqrt(var + EPS))).astype(BF16))
    y_ret = jnp.dot(jnp.concatenate(gated, axis=-1), mp.w_br[...], preferred_element_type=F32)
    yield
    y_pool = jnp.dot(st.pooled[...], mp.w_pool[...], preferred_element_type=F32)
    yield
    merged = (st.gates[:, 0:d_model] * y_pool + st.gates[:, d_model:2 * d_model] * y_ret).astype(BF16)
    y = jnp.dot(merged, mp.w_out[...], preferred_element_type=F32)
    yield
    o_ref[rows, :] = x_ref[rows, :] + gt_m * (_rms(y) * mp.g_post[...])


WEAVE = "s" + "mms" * 6 + "sm" * 3


def _weave(order, mix, stage):
    gens = {"m": mix, "s": stage}
    for c in order:
        next(gens[c], None)
    for g in (mix, stage):
        for _ in g:
            pass


def _mixer_kernel(x_ref, xn_ref, mod_ref, modn_ref, pos_ref, posn_ref, invf_ref, g_pre_ref, g_post_ref,
                  w_in_ref, b_gate_ref, w_pool_ref, w_br_ref, w_out_ref, dmat_ref, qdec_ref, kdec_ref,
                  o_ref, state_ref, halo_ref, h0, h1, rot0, pooled0, sg0, gates0, rot1, pooled1, sg1, gates1,
                  *, tile, steps_per_seq, n_steps, tile_decay):
    j = pl.program_id(0)
    s = j % steps_per_seq
    s_next = jnp.minimum(j + 1, n_steps - 1) % steps_per_seq
    sp = StageParams(w_in_ref, b_gate_ref, invf_ref, qdec_ref, kdec_ref, halo_ref)
    mp = MixParams(g_post_ref, w_pool_ref, w_br_ref, w_out_ref, dmat_ref, state_ref)
    st0 = Staged(rot0, pooled0, sg0, gates0)
    st1 = Staged(rot1, pooled1, sg1, gates1)
    lo, hi = slice(0, tile), slice(tile, 2 * tile)

    @pl.when(j == 0)
    def _():
        _norm_tile(x_ref[lo, :], mod_ref, g_pre_ref, h0)
        for _ in _stage_phases(h0, pos_ref[:, lo], 0, True, sp, st0):
            pass
        _norm_tile(x_ref[hi, :], mod_ref, g_pre_ref, h1)

    @pl.when(s == 0)
    def _():
        state_ref[...] = jnp.zeros_like(state_ref)

    _weave(WEAVE,
           _mix_phases(st0, x_ref, lo, o_ref, mod_ref, mp, tile_decay),
           _stage_phases(h1, pos_ref[:, hi], 2 * s + 1, False, sp, st1))
    _norm_tile(xn_ref[lo, :], modn_ref, g_pre_ref, h0)
    _weave(WEAVE,
           _mix_phases(st1, x_ref, hi, o_ref, mod_ref, mp, tile_decay),
           _stage_phases(h0, posn_ref[:, lo], 2 * s_next, s_next == 0, sp, st0))
    _norm_tile(xn_ref[hi, :], modn_ref, g_pre_ref, h1)


def _decay_tables(tile, hd):
    gamma = 1.0 - 2.0 ** (-5.0 - np.arange(RET_HEADS, dtype=np.float64))
    idx = np.arange(tile)
    dist = np.abs(idx[:, None] - idx[None, :])
    visible = (idx[None, :] // CHUNK) <= (idx[:, None] // CHUNK)
    dmat = np.where(visible[None], gamma[:, None, None] ** dist[None], 0.0)
    qdec = np.broadcast_to((gamma[:, None] ** (idx[None, :] + 1))[:, :, None], (RET_HEADS, tile, hd))
    kdec = np.broadcast_to((gamma[:, None] ** (tile - 1 - idx[None, :]))[:, :, None], (RET_HEADS, tile, hd))
    tile_decay = tuple(float(g) ** tile for g in gamma)
    as_f32 = lambda a: jnp.asarray(np.ascontiguousarray(a), dtype=F32)
    return as_f32(dmat), as_f32(qdec), as_f32(kdec), tile_decay


def _const_spec(shape, grid_rank):
    nd = len(shape)
    if grid_rank == 1:
        return pl.BlockSpec(shape, lambda j: (0,) * nd)
    return pl.BlockSpec(shape, lambda b, t: (0,) * nd)


def _mixer(x, mod, positions, g_pre, g_post, w_in, b_gate, w_pool, w_br, w_out):
    bsz, seq, d = x.shape
    tile = MIXER_TILE
    pool_width = w_pool.shape[0]
    ret_width = w_br.shape[0]
    hd = ret_width // RET_HEADS
    assert seq % (2 * tile) == 0 and tile % CHUNK == 0 and hd == LANES
    assert pool_width % MXU_COLS == 0 and ret_width % MXU_COLS == 0 and d % MXU_COLS == 0
    steps_per_seq = seq // (2 * tile)
    n_steps = bsz * steps_per_seq
    dmat, qdec, kdec, tile_decay = _decay_tables(tile, hd)
    half = hd // 2
    invf = (ROPE_BASE ** (-jnp.arange(half, dtype=F32) / half)).reshape(half, 1)
    mod3 = mod.reshape(bsz, N_MOD, d)
    pos3 = positions.reshape(bsz, 1, seq)

    def cur(j):
        return j // steps_per_seq, j % steps_per_seq

    def nxt(j):
        return cur(jnp.minimum(j + 1, n_steps - 1))

    const = functools.partial(_const_spec, grid_rank=1)
    operands = [
        (x, pl.BlockSpec((None, 2 * tile, d), lambda j: (*cur(j), 0))),
        (x, pl.BlockSpec((None, 2 * tile, d), lambda j: (*nxt(j), 0))),
        (mod3, pl.BlockSpec((None, N_MOD, d), lambda j: (cur(j)[0], 0, 0))),
        (mod3, pl.BlockSpec((None, N_MOD, d), lambda j: (nxt(j)[0], 0, 0))),
        (pos3, pl.BlockSpec((None, 1, 2 * tile), lambda j: (cur(j)[0], 0, cur(j)[1]))),
        (pos3, pl.BlockSpec((None, 1, 2 * tile), lambda j: (nxt(j)[0], 0, nxt(j)[1]))),
        (invf, const((half, 1))),
        (g_pre.reshape(1, d), const((1, d))),
        (g_post.reshape(1, d), const((1, d))),
        (w_in.astype(BF16), const(w_in.shape)),
        (b_gate.reshape(1, N_BRANCHES * d), const((1, N_BRANCHES * d))),
        (w_pool, const(w_pool.shape)),
        (w_br.astype(BF16), const(w_br.shape)),
        (w_out.astype(BF16), const(w_out.shape)),
        (dmat, const(dmat.shape)),
        (qdec, const(qdec.shape)),
        (kdec, const(kdec.shape)),
    ]
    normed = [pltpu.VMEM((tile, d), BF16)]
    staged = [
        pltpu.VMEM((5, tile, ret_width), BF16),
        pltpu.VMEM((tile, pool_width), BF16),
        pltpu.VMEM((tile, ret_width), F32),
        pltpu.VMEM((tile, N_BRANCHES * d), F32),
    ]
    return pl.pallas_call(
        functools.partial(_mixer_kernel, tile=tile, steps_per_seq=steps_per_seq, n_steps=n_steps,
                          tile_decay=tile_decay),
        out_shape=jax.ShapeDtypeStruct(x.shape, F32),
        grid=(n_steps,),
        in_specs=[spec for _, spec in operands],
        out_specs=pl.BlockSpec((None, 2 * tile, d), lambda j: (*cur(j), 0)),
        scratch_shapes=[
            pltpu.VMEM((RET_HEADS, hd, hd), F32),
            pltpu.VMEM((POOL_HALO, pool_width), F32),
        ] + normed + normed + staged + staged,
        compiler_params=pltpu.CompilerParams(
            dimension_semantics=("arbitrary",),
            vmem_limit_bytes=VMEM_LIMIT_BYTES),
        name="mixer",
    )(*[a for a, _ in operands])


def _ffn_kernel(x_ref, mod_ref, g_pre_ref, g_post_ref, w1_ref, w2_ref, o_ref):
    x = x_ref[...]
    sh_f, sc_f, gt_f = mod_ref[3:4, :], mod_ref[4:5, :], mod_ref[5:6, :]
    dims = (((1,), (0,)), ((), ()))
    h = ((_rms(x) * g_pre_ref[...]) * (1.0 + sc_f) + sh_f).astype(BF16)
    a = jnp.maximum(lax.dot_general(h, w1_ref[...], dims, preferred_element_type=F32), 0.0)
    y = lax.dot_general((a * a).astype(BF16), w2_ref[...], dims, preferred_element_type=F32)
    o_ref[...] = x + gt_f * (_rms(y) * g_post_ref[...])


def _ffn(x, mod, g_pre, g_post, w1, w2):
    bsz, seq, d = x.shape
    tile = FFN_TILE
    assert seq % tile == 0
    const = functools.partial(_const_spec, grid_rank=2)
    return pl.pallas_call(
        _ffn_kernel,
        out_shape=jax.ShapeDtypeStruct(x.shape, F32),
        grid=(bsz, seq // tile),
        in_specs=[
            pl.BlockSpec((None, tile, d), lambda b, t: (b, t, 0)),
            pl.BlockSpec((None, N_MOD, d), lambda b, t: (b, 0, 0)),
            const((1, d)),
            const((1, d)),
            const(w1.shape),
            const(w2.shape),
        ],
        out_specs=pl.BlockSpec((None, tile, d), lambda b, t: (b, t, 0)),
        compiler_params=pltpu.CompilerParams(
            dimension_semantics=("arbitrary", "arbitrary"),
            vmem_limit_bytes=VMEM_LIMIT_BYTES),
        name="ffn",
    )(x, mod.reshape(bsz, N_MOD, d), g_pre.reshape(1, d), g_post.reshape(1, d),
      w1, w2)


def kernel(x, c, positions, ada_w, ada_b, mix_pre_g, mix_post_g, ffn_pre_g, ffn_post_g, w_in,
           b_branch_gate, pool_w, pool_scale, w_branch_pool, w_branch_ret, w_out, w_ff1, w_ff2):
    depth = ada_w.shape[0]
    for l in range(depth):
        mod = _adaln(c, ada_w[l], ada_b[l])
        w_pool = _pool_fold(pool_w[l], pool_scale[l], w_branch_pool[l])
        x = _mixer(x, mod, positions, mix_pre_g[l], mix_post_g[l], w_in[l], b_branch_gate[l],
                   w_pool, w_branch_ret[l], w_out[l])
        x = _ffn(x, mod, ffn_pre_g[l], ffn_post_g[l], w_ff1[l], w_ff2[l])
    return x
```

```python
import functools
from typing import Any, NamedTuple

import jax
import jax.numpy as jnp
import numpy as np
from jax import lax
from jax.experimental import pallas as pl
from jax.experimental.pallas import tpu as pltpu

F32 = jnp.float32
BF16 = jnp.bfloat16

CHUNK = 64
POOL_WINDOWS = (2, 4, 8, 16)
POOL_HALO = 16
POOL_ROWS = 64
RET_HEADS = 4
N_BRANCHES = 2
N_MOD = 6
ROPE_BASE = 10000.0
EPS = 1e-6

LANES = 128
MIXER_TILE = 256
RET_BLOCK = 256
FFN_TILE = 512
ADALN_BLOCK_N = 1536
VMEM_LIMIT_BYTES = 56 * 1024 * 1024


def _rms(xf):
    return xf * lax.rsqrt(jnp.mean(xf * xf, axis=-1, keepdims=True) + EPS)


def _adaln_kernel(c_ref, w_ref, b_ref, o_ref):
    c = c_ref[...]
    a = (c * jax.nn.sigmoid(c)).astype(BF16)
    o_ref[...] = jnp.dot(a, w_ref[...].astype(BF16), preferred_element_type=F32) + b_ref[...]


def _adaln(c, w, b):
    bsz, d = c.shape
    n = w.shape[1]
    return pl.pallas_call(
        _adaln_kernel,
        out_shape=jax.ShapeDtypeStruct((bsz, n), F32),
        grid=(n // ADALN_BLOCK_N,),
        in_specs=[
            pl.BlockSpec((bsz, d), lambda j: (0, 0)),
            pl.BlockSpec((d, ADALN_BLOCK_N), lambda j: (0, j)),
            pl.BlockSpec((1, ADALN_BLOCK_N), lambda j: (0, j)),
        ],
        out_specs=pl.BlockSpec((bsz, ADALN_BLOCK_N), lambda j: (0, j)),
        compiler_params=pltpu.CompilerParams(dimension_semantics=("arbitrary",)),
        name="adaln",
    )(c, w, b.reshape(1, n))


def _pool_fold_kernel(pw_ref, scale_ref, wbp_ref, o_ref):
    o_ref[...] = jnp.dot(pw_ref[...] * scale_ref[...], wbp_ref[...], preferred_element_type=F32,
                         precision=lax.Precision.HIGHEST).astype(o_ref.dtype)


def _pool_fold(pool_w, pool_scale, w_bp):
    groups, gd, _ = pool_w.shape
    d = w_bp.shape[1]
    return pl.pallas_call(
        _pool_fold_kernel,
        out_shape=jax.ShapeDtypeStruct(w_bp.shape, BF16),
        grid=(groups,),
        in_specs=[
            pl.BlockSpec((None, gd, gd), lambda g: (g, 0, 0)),
            pl.BlockSpec((1, gd), lambda g: (0, g)),
            pl.BlockSpec((gd, d), lambda g: (g, 0)),
        ],
        out_specs=pl.BlockSpec((gd, d), lambda g: (g, 0)),
        compiler_params=pltpu.CompilerParams(dimension_semantics=("arbitrary",)),
        name="pool_fold",
    )(pool_w, pool_scale.reshape(1, groups * gd), w_bp)


class Staged(NamedTuple):
    probs: Any
    rot: Any
    pooled: Any
    sg: Any
    gates: Any


class StageParams(NamedTuple):
    g_pre: Any
    w_in: Any
    b_gate: Any
    invf: Any
    dmat: Any
    qdec: Any
    kdec: Any
    halo: Any


class MixParams(NamedTuple):
    g_post: Any
    w_pool: Any
    w_br: Any
    w_out: Any
    state: Any


def _stage_phases(x_ref, rows, mod_ref, pos, seq_tile, first, sp: StageParams, out: Staged):
    tile = rows.stop - rows.start
    pool_width = out.pooled.shape[1]
    ret_width = out.sg.shape[1]
    d_model = out.gates.shape[1] // N_BRANCHES
    hd = ret_width // RET_HEADS
    sh_m, sc_m = mod_ref[0:1, :], mod_ref[1:2, :]
    h = ((_rms(x_ref[rows, :]) * sp.g_pre[...]) * (1.0 + sc_m) + sh_m).astype(BF16)
    yield

    def proj(lo, width):
        return jnp.dot(h, sp.w_in[:, lo:lo + width], preferred_element_type=F32)

    u = proj(0, pool_width)
    yield

    if first is True:
        halo = jnp.zeros(sp.halo.shape, F32)
    elif first is False:
        halo = sp.halo[...]
    else:
        halo = jnp.where(first, 0.0, sp.halo[...])
    sp.halo[...] = u[tile - POOL_HALO:tile, :]
    t_pos = (lax.broadcasted_iota(jnp.int32, (tile, LANES), 0) + (seq_tile * tile + 1)).astype(F32)
    inv_t = 1.0 / t_pos
    gd = pool_width // len(POOL_WINDOWS)
    for gi, w in enumerate(POOL_WINDOWS):
        cols = slice(gi * gd, (gi + 1) * gd)
        inv_cnt = jnp.where(t_pos < float(w), inv_t, 1.0 / w)
        for r0 in range(0, tile, POOL_ROWS):
            blk = slice(r0, r0 + POOL_ROWS)
            before = halo[:, cols] if r0 == 0 else u[r0 - POOL_HALO:r0, cols]
            s = jnp.concatenate([before, u[blk, cols]], axis=0)
            k = 1
            while k < w:
                s = s + pltpu.roll(s, k, 0)
                k *= 2
            out.pooled[blk, cols] = (s[POOL_HALO:] * inv_cnt[blk] - u[blk, cols]).astype(BF16)
    q = proj(pool_width, ret_width)
    yield
    k_all = proj(pool_width + ret_width, ret_width)
    yield

    ang = sp.invf[...] * pos.astype(F32)
    cos_h, sin_h = jnp.cos(ang), jnp.sin(ang)
    cos_t = jnp.transpose(jnp.concatenate([cos_h, cos_h], axis=0))
    sin_t = jnp.transpose(jnp.concatenate([-sin_h, sin_h], axis=0))
    blk = sp.dmat.shape[1]
    roped = []
    for hh in range(RET_HEADS):
        cols = slice(hh * hd, (hh + 1) * hd)
        qh, kh = q[:, cols], k_all[:, cols]
        qh = (qh * cos_t + pltpu.roll(qh, hd // 2, 1) * sin_t) * (hd ** -0.5)
        kh = kh * cos_t + pltpu.roll(kh, hd // 2, 1) * sin_t
        roped.append((qh.astype(BF16), kh.astype(BF16)))
        for r0 in range(0, tile, blk):
            out.rot[1, r0:r0 + blk, cols] = (qh[r0:r0 + blk] * sp.qdec[hh]).astype(BF16)
            out.rot[2, r0:r0 + blk, cols] = (kh[r0:r0 + blk] * sp.kdec[hh]).astype(BF16)
    out.rot[0] = proj(pool_width + 2 * ret_width, ret_width).astype(BF16)
    yield
    g = proj(pool_width + 3 * ret_width, ret_width)
    out.sg[...] = g * jax.nn.sigmoid(g)
    yield
    for hh, (qh, kh) in enumerate(roped):
        for bi, r0 in enumerate(range(0, tile, blk)):
            scores = lax.dot_general(qh[r0:r0 + blk], kh[r0:r0 + blk], (((1,), (1,)), ((), ())),
                                     preferred_element_type=F32)
            out.probs[hh * (tile // blk) + bi] = (scores * sp.dmat[hh]).astype(BF16)
    yield
    gate_lo = pool_width + 4 * ret_width
    for br in range(N_BRANCHES):
        cols = slice(br * d_model, (br + 1) * d_model)
        out.gates[:, cols] = jax.nn.sigmoid(proj(gate_lo + br * d_model, d_model) + sp.b_gate[:, cols])
        yield


def _mix_phases(st: Staged, x_ref, rows, o_ref, mod_ref, mp: MixParams, tile_decay):
    d_model = x_ref.shape[1]
    ret_width = st.sg.shape[1]
    hd = ret_width // RET_HEADS
    gt_m = mod_ref[2:3, :]

    tile = st.sg.shape[0]
    blk = st.probs.shape[1]
    n_blk = tile // blk
    gated = [[None] * RET_HEADS for _ in range(n_blk)]
    for bi in range(n_blk):
        blk_rows = slice(bi * blk, (bi + 1) * blk)
        for hh in range(RET_HEADS):
            cols = slice(hh * hd, (hh + 1) * hd)
            vh = st.rot[0, blk_rows, cols]
            o = jnp.dot(st.probs[hh * n_blk + bi], vh, preferred_element_type=F32)
            o = o + jnp.dot(st.rot[1, blk_rows, cols], mp.state[hh].astype(BF16),
                            preferred_element_type=F32)
            kv = lax.dot_general(st.rot[2, blk_rows, cols], vh, (((0,), (0,)), ((), ())),
                                 preferred_element_type=F32)
            mp.state[hh] = mp.state[hh] * tile_decay[hh] + kv
            mu = jnp.mean(o, axis=-1, keepdims=True)
            oc = o - mu
            var = jnp.mean(oc * oc, axis=-1, keepdims=True)
            gated[bi][hh] = (st.sg[blk_rows, cols] * (oc * lax.rsqrt(var + EPS))).astype(BF16)
            yield
    y_pool = jnp.dot(st.pooled[...], mp.w_pool[...], preferred_element_type=F32)
    yield
    gated = jnp.concatenate([jnp.concatenate(row, axis=-1) for row in gated], axis=0)
    y_ret = jnp.dot(gated, mp.w_br[...], preferred_element_type=F32)
    yield
    merged = (st.gates[:, 0:d_model] * y_pool + st.gates[:, d_model:2 * d_model] * y_ret).astype(BF16)
    y = jnp.dot(merged, mp.w_out[...], preferred_element_type=F32)
    yield
    o_ref[rows, :] = x_ref[rows, :] + gt_m * (_rms(y) * mp.g_post[...])


WEAVE = "msmmmsmssmssmsssm"


def _weave(order, mix, stage):
    gens = {"m": mix, "s": stage}
    for c in order:
        next(gens[c], None)
    for g in (mix, stage):
        for _ in g:
            pass


def _mixer_kernel(x_ref, xn_ref, mod_ref, modn_ref, pos_ref, posn_ref, invf_ref, g_pre_ref, g_post_ref,
                  w_in_ref, b_gate_ref, w_pool_ref, w_br_ref, w_out_ref, dmat_ref, qdec_ref, kdec_ref,
                  o_ref, state_ref, halo_ref, *staged, tile, steps_per_seq, n_steps, tile_decay):
    j = pl.program_id(0)
    s = j % steps_per_seq
    s_next = jnp.minimum(j + 1, n_steps - 1) % steps_per_seq
    sp = StageParams(g_pre_ref, w_in_ref, b_gate_ref, invf_ref, dmat_ref, qdec_ref, kdec_ref, halo_ref)
    mp = MixParams(g_post_ref, w_pool_ref, w_br_ref, w_out_ref, state_ref)
    n_fields = len(Staged._fields)
    slots = (Staged(*staged[:n_fields]), Staged(*staged[n_fields:]))
    whole = slice(0, tile)

    @pl.when(j == 0)
    def _():
        for _ in _stage_phases(x_ref, whole, mod_ref, pos_ref[...], 0, True, sp, slots[0]):
            pass

    @pl.when(s == 0)
    def _():
        state_ref[...] = jnp.zeros_like(state_ref)

    for parity in range(2):
        @pl.when(j % 2 == parity)
        def _():
            _weave(WEAVE,
                   _mix_phases(slots[parity], x_ref, whole, o_ref, mod_ref, mp, tile_decay),
                   _stage_phases(xn_ref, whole, modn_ref, posn_ref[...], s_next, s_next == 0, sp,
                                 slots[1 - parity]))


def _decay_tables(tile, hd):
    gamma = 1.0 - 2.0 ** (-5.0 - np.arange(RET_HEADS, dtype=np.float64))
    idx = np.arange(tile)
    dist = np.abs(idx[:, None] - idx[None, :])
    visible = (idx[None, :] // CHUNK) <= (idx[:, None] // CHUNK)
    dmat = np.where(visible[None], gamma[:, None, None] ** dist[None], 0.0)
    qdec = np.broadcast_to((gamma[:, None] ** (idx[None, :] + 1))[:, :, None], (RET_HEADS, tile, hd))
    kdec = np.broadcast_to((gamma[:, None] ** (tile - 1 - idx[None, :]))[:, :, None], (RET_HEADS, tile, hd))
    tile_decay = tuple(float(g) ** tile for g in gamma)
    as_f32 = lambda a: jnp.asarray(np.ascontiguousarray(a), dtype=F32)
    return as_f32(dmat), as_f32(qdec), as_f32(kdec), tile_decay


def _const_spec(shape):
    nd = len(shape)
    return pl.BlockSpec(shape, lambda j: (0,) * nd)


def _mixer(x, mod, positions, g_pre, g_post, w_in, b_gate, w_pool, w_br, w_out):
    bsz, seq, d = x.shape
    tile = MIXER_TILE
    pool_width = w_pool.shape[0]
    ret_width = w_br.shape[0]
    hd = ret_width // RET_HEADS
    assert seq % tile == 0 and tile % RET_BLOCK == 0 and RET_BLOCK % CHUNK == 0 and hd == LANES
    steps_per_seq = seq // tile
    n_steps = bsz * steps_per_seq
    dmat, qdec, kdec, tile_decay = _decay_tables(RET_BLOCK, hd)
    half = hd // 2
    invf = (ROPE_BASE ** (-jnp.arange(half, dtype=F32) / half)).reshape(half, 1)
    mod3 = mod.reshape(bsz, N_MOD, d)
    pos3 = positions.reshape(bsz, 1, seq)

    def cur(j):
        return j // steps_per_seq, j % steps_per_seq

    def nxt(j):
        return cur(jnp.minimum(j + 1, n_steps - 1))

    const = _const_spec
    operands = [
        (x, pl.BlockSpec((None, tile, d), lambda j: (*cur(j), 0))),
        (x, pl.BlockSpec((None, tile, d), lambda j: (*nxt(j), 0))),
        (mod3, pl.BlockSpec((None, N_MOD, d), lambda j: (cur(j)[0], 0, 0))),
        (mod3, pl.BlockSpec((None, N_MOD, d), lambda j: (nxt(j)[0], 0, 0))),
        (pos3, pl.BlockSpec((None, 1, tile), lambda j: (cur(j)[0], 0, cur(j)[1]))),
        (pos3, pl.BlockSpec((None, 1, tile), lambda j: (nxt(j)[0], 0, nxt(j)[1]))),
        (invf, const((half, 1))),
        (g_pre.reshape(1, d), const((1, d))),
        (g_post.reshape(1, d), const((1, d))),
        (w_in.astype(BF16), const(w_in.shape)),
        (b_gate.reshape(1, N_BRANCHES * d), const((1, N_BRANCHES * d))),
        (w_pool, const(w_pool.shape)),
        (w_br.astype(BF16), const(w_br.shape)),
        (w_out.astype(BF16), const(w_out.shape)),
        (dmat, const(dmat.shape)),
        (qdec, const(qdec.shape)),
        (kdec, const(kdec.shape)),
    ]
    staged = [
        pltpu.VMEM((RET_HEADS * (tile // RET_BLOCK), RET_BLOCK, RET_BLOCK), BF16),
        pltpu.VMEM((3, tile, ret_width), BF16),
        pltpu.VMEM((tile, pool_width), BF16),
        pltpu.VMEM((tile, ret_width), F32),
        pltpu.VMEM((tile, N_BRANCHES * d), F32),
    ]
    return pl.pallas_call(
        functools.partial(_mixer_kernel, tile=tile, steps_per_seq=steps_per_seq, n_steps=n_steps,
                          tile_decay=tile_decay),
        out_shape=jax.ShapeDtypeStruct(x.shape, F32),
        grid=(n_steps,),
        in_specs=[spec for _, spec in operands],
        out_specs=pl.BlockSpec((None, tile, d), lambda j: (*cur(j), 0)),
        scratch_shapes=[
            pltpu.VMEM((RET_HEADS, hd, hd), F32),
            pltpu.VMEM((POOL_HALO, pool_width), F32),
        ] + staged + staged,
        compiler_params=pltpu.CompilerParams(
            dimension_semantics=("arbitrary",),
            vmem_limit_bytes=VMEM_LIMIT_BYTES),
        name="mixer",
    )(*[a for a, _ in operands])


def _ffn_kernel(x_ref, mod_ref, g_pre_ref, g_post_ref, w1_ref, w2_ref, o_ref):
    x = x_ref[...]
    sh_f, sc_f, gt_f = mod_ref[3:4, :], mod_ref[4:5, :], mod_ref[5:6, :]
    dims = (((1,), (0,)), ((), ()))
    h = ((_rms(x) * g_pre_ref[...]) * (1.0 + sc_f) + sh_f).astype(BF16)
    a = jnp.maximum(lax.dot_general(h, w1_ref[...], dims, preferred_element_type=F32), 0.0)
    y = lax.dot_general((a * a).astype(BF16), w2_ref[...], dims, preferred_element_type=F32)
    o_ref[...] = x + gt_f * (_rms(y) * g_post_ref[...])


def _ffn(x, mod, g_pre, g_post, w1, w2):
    bsz, seq, d = x.shape
    tile = FFN_TILE
    assert seq % tile == 0
    tiles_per_seq = seq // tile

    def cur(j):
        return j // tiles_per_seq, j % tiles_per_seq

    return pl.pallas_call(
        _ffn_kernel,
        out_shape=jax.ShapeDtypeStruct(x.shape, F32),
        grid=(bsz * tiles_per_seq,),
        in_specs=[
            pl.BlockSpec((None, tile, d), lambda j: (*cur(j), 0)),
            pl.BlockSpec((None, N_MOD, d), lambda j: (cur(j)[0], 0, 0)),
            _const_spec((1, d)),
            _const_spec((1, d)),
            _const_spec(w1.shape),
            _const_spec(w2.shape),
        ],
        out_specs=pl.BlockSpec((None, tile, d), lambda j: (*cur(j), 0)),
        compiler_params=pltpu.CompilerParams(
            dimension_semantics=("arbitrary",),
            vmem_limit_bytes=VMEM_LIMIT_BYTES),
        name="ffn",
    )(x, mod.reshape(bsz, N_MOD, d), g_pre.reshape(1, d), g_post.reshape(1, d), w1, w2)


def kernel(x, c, positions, ada_w, ada_b, mix_pre_g, mix_post_g, ffn_pre_g, ffn_post_g, w_in,
           b_branch_gate, pool_w, pool_scale, w_branch_pool, w_branch_ret, w_out, w_ff1, w_ff2):
    depth = ada_w.shape[0]
    for l in range(depth):
        mod = _adaln(c, ada_w[l], ada_b[l])
        w_pool = _pool_fold(pool_w[l], pool_scale[l], w_branch_pool[l])
        x = _mixer(x, mod, positions, mix_pre_g[l], mix_post_g[l], w_in[l], b_branch_gate[l],
                   w_pool, w_branch_ret[l], w_out[l])
        x = _ffn(x, mod, ffn_pre_g[l], ffn_post_g[l], w_ff1[l], w_ff2[l])
    return x
```

```python
import functools
from typing import Any, NamedTuple

import jax
import jax.numpy as jnp
import numpy as np
from jax import lax
from jax.experimental import pallas as pl
from jax.experimental.pallas import tpu as pltpu

F32 = jnp.float32
BF16 = jnp.bfloat16

CHUNK = 64
POOL_WINDOWS = (2, 4, 8, 16)
POOL_HALO = 16
POOL_ROWS = 64
RET_HEADS = 4
N_BRANCHES = 2
N_MOD = 6
ROPE_BASE = 10000.0
EPS = 1e-6

LANES = 128
MIXER_TILE = 256
RET_BLOCK = 256
FFN_TILE = 512
ADALN_BLOCK_N = 1536
VMEM_LIMIT_BYTES = 56 * 1024 * 1024


def _rms(xf):
    return xf * lax.rsqrt(jnp.mean(xf * xf, axis=-1, keepdims=True) + EPS)


def _adaln_kernel(c_ref, w_ref, b_ref, o_ref):
    c = c_ref[...]
    a = (c * jax.nn.sigmoid(c)).astype(BF16)
    o_ref[...] = jnp.dot(a, w_ref[...].astype(BF16), preferred_element_type=F32) + b_ref[...]


def _adaln(c, w, b):
    bsz, d = c.shape
    n = w.shape[1]
    return pl.pallas_call(
        _adaln_kernel,
        out_shape=jax.ShapeDtypeStruct((bsz, n), F32),
        grid=(n // ADALN_BLOCK_N,),
        in_specs=[
            pl.BlockSpec((bsz, d), lambda j: (0, 0)),
            pl.BlockSpec((d, ADALN_BLOCK_N), lambda j: (0, j)),
            pl.BlockSpec((1, ADALN_BLOCK_N), lambda j: (0, j)),
        ],
        out_specs=pl.BlockSpec((bsz, ADALN_BLOCK_N), lambda j: (0, j)),
        compiler_params=pltpu.CompilerParams(dimension_semantics=("arbitrary",)),
        name="adaln",
    )(c, w, b.reshape(1, n))


def _pool_fold_kernel(pw_ref, scale_ref, wbp_ref, o_ref):
    o_ref[...] = jnp.dot((pw_ref[...] * scale_ref[...]).astype(BF16), wbp_ref[...].astype(BF16),
                         preferred_element_type=F32).astype(o_ref.dtype)


def _pool_fold(pool_w, pool_scale, w_bp):
    groups, gd, _ = pool_w.shape
    d = w_bp.shape[1]
    return pl.pallas_call(
        _pool_fold_kernel,
        out_shape=jax.ShapeDtypeStruct(w_bp.shape, BF16),
        grid=(groups,),
        in_specs=[
            pl.BlockSpec((None, gd, gd), lambda g: (g, 0, 0)),
            pl.BlockSpec((1, gd), lambda g: (0, g)),
            pl.BlockSpec((gd, d), lambda g: (g, 0)),
        ],
        out_specs=pl.BlockSpec((gd, d), lambda g: (g, 0)),
        compiler_params=pltpu.CompilerParams(dimension_semantics=("arbitrary",)),
        name="pool_fold",
    )(pool_w, pool_scale.reshape(1, groups * gd), w_bp)


class Staged(NamedTuple):
    probs: Any
    rot: Any
    pooled: Any
    sg: Any
    gates: Any


class StageParams(NamedTuple):
    g_pre: Any
    w_in: Any
    b_gate: Any
    invf: Any
    dmat: Any
    qdec: Any
    kdec: Any
    halo: Any


class MixParams(NamedTuple):
    g_post: Any
    w_pool: Any
    w_br: Any
    w_out: Any
    state: Any


def _stage_phases(x_ref, rows, mod_ref, pos, seq_tile, first, sp: StageParams, out: Staged):
    tile = rows.stop - rows.start
    pool_width = out.pooled.shape[1]
    ret_width = out.sg.shape[1]
    d_model = out.gates.shape[1] // N_BRANCHES
    hd = ret_width // RET_HEADS
    sh_m, sc_m = mod_ref[0:1, :], mod_ref[1:2, :]
    h = ((_rms(x_ref[rows, :]) * sp.g_pre[...]) * (1.0 + sc_m) + sh_m).astype(BF16)
    yield

    def proj(lo, width):
        return jnp.dot(h, sp.w_in[:, lo:lo + width], preferred_element_type=F32)

    u = proj(0, pool_width)
    yield

    if first is True:
        halo = jnp.zeros(sp.halo.shape, F32)
    elif first is False:
        halo = sp.halo[...]
    else:
        halo = jnp.where(first, 0.0, sp.halo[...])
    sp.halo[...] = u[tile - POOL_HALO:tile, :]
    t_pos = (lax.broadcasted_iota(jnp.int32, (tile, LANES), 0) + (seq_tile * tile + 1)).astype(F32)
    inv_t = 1.0 / t_pos
    gd = pool_width // len(POOL_WINDOWS)
    for gi, w in enumerate(POOL_WINDOWS):
        cols = slice(gi * gd, (gi + 1) * gd)
        inv_cnt = jnp.where(t_pos < float(w), inv_t, 1.0 / w)
        for r0 in range(0, tile, POOL_ROWS):
            blk = slice(r0, r0 + POOL_ROWS)
            before = halo[:, cols] if r0 == 0 else u[r0 - POOL_HALO:r0, cols]
            s = jnp.concatenate([before, u[blk, cols]], axis=0)
            k = 1
            while k < w:
                s = s + pltpu.roll(s, k, 0)
                k *= 2
            out.pooled[blk, cols] = (s[POOL_HALO:] * inv_cnt[blk] - u[blk, cols]).astype(BF16)
    q = proj(pool_width, ret_width)
    yield
    k_all = proj(pool_width + ret_width, ret_width)
    yield

    ang = sp.invf[...] * pos.astype(F32)
    cos_h, sin_h = jnp.cos(ang), jnp.sin(ang)
    cos_t = jnp.transpose(jnp.concatenate([cos_h, cos_h], axis=0))
    sin_t = jnp.transpose(jnp.concatenate([-sin_h, sin_h], axis=0))
    blk = sp.dmat.shape[1]
    roped = []
    for hh in range(RET_HEADS):
        cols = slice(hh * hd, (hh + 1) * hd)
        qh, kh = q[:, cols], k_all[:, cols]
        qh = (qh * cos_t + pltpu.roll(qh, hd // 2, 1) * sin_t) * (hd ** -0.5)
        kh = kh * cos_t + pltpu.roll(kh, hd // 2, 1) * sin_t
        roped.append((qh.astype(BF16), kh.astype(BF16)))
        for r0 in range(0, tile, blk):
            out.rot[1, r0:r0 + blk, cols] = (qh[r0:r0 + blk] * sp.qdec[hh]).astype(BF16)
            out.rot[2, r0:r0 + blk, cols] = (kh[r0:r0 + blk] * sp.kdec[hh]).astype(BF16)
    out.rot[0] = proj(pool_width + 2 * ret_width, ret_width).astype(BF16)
    yield
    g = proj(pool_width + 3 * ret_width, ret_width)
    out.sg[...] = g * jax.nn.sigmoid(g)
    yield
    for hh, (qh, kh) in enumerate(roped):
        for bi, r0 in enumerate(range(0, tile, blk)):
            scores = lax.dot_general(qh[r0:r0 + blk], kh[r0:r0 + blk], (((1,), (1,)), ((), ())),
                                     preferred_element_type=F32)
            out.probs[hh * (tile // blk) + bi] = (scores * sp.dmat[hh]).astype(BF16)
    yield
    gate_lo = pool_width + 4 * ret_width
    for br in range(N_BRANCHES):
        cols = slice(br * d_model, (br + 1) * d_model)
        out.gates[:, cols] = jax.nn.sigmoid(proj(gate_lo + br * d_model, d_model) + sp.b_gate[:, cols])
        yield


def _mix_phases(st: Staged, x_ref, rows, o_ref, mod_ref, mp: MixParams, tile_decay):
    d_model = x_ref.shape[1]
    ret_width = st.sg.shape[1]
    hd = ret_width // RET_HEADS
    gt_m = mod_ref[2:3, :]

    tile = st.sg.shape[0]
    blk = st.probs.shape[1]
    n_blk = tile // blk
    gated = [[None] * RET_HEADS for _ in range(n_blk)]
    for bi in range(n_blk):
        blk_rows = slice(bi * blk, (bi + 1) * blk)
        for hh in range(RET_HEADS):
            cols = slice(hh * hd, (hh + 1) * hd)
            vh = st.rot[0, blk_rows, cols]
            o = jnp.dot(st.probs[hh * n_blk + bi], vh, preferred_element_type=F32)
            o = o + jnp.dot(st.rot[1, blk_rows, cols], mp.state[hh].astype(BF16),
                            preferred_element_type=F32)
            kv = lax.dot_general(st.rot[2, blk_rows, cols], vh, (((0,), (0,)), ((), ())),
                                 preferred_element_type=F32)
            mp.state[hh] = mp.state[hh] * tile_decay[hh] + kv
            mu = jnp.mean(o, axis=-1, keepdims=True)
            oc = o - mu
            var = jnp.mean(oc * oc, axis=-1, keepdims=True)
            gated[bi][hh] = (st.sg[blk_rows, cols] * (oc * lax.rsqrt(var + EPS))).astype(BF16)
            yield
    y_pool = jnp.dot(st.pooled[...], mp.w_pool[...], preferred_element_type=F32)
    yield
    gated = jnp.concatenate([jnp.concatenate(row, axis=-1) for row in gated], axis=0)
    y_ret = jnp.dot(gated, mp.w_br[...], preferred_element_type=F32)
    yield
    merged = (st.gates[:, 0:d_model] * y_pool + st.gates[:, d_model:2 * d_model] * y_ret).astype(BF16)
    y = jnp.dot(merged, mp.w_out[...], preferred_element_type=F32)
    yield
    o_ref[rows, :] = x_ref[rows, :] + gt_m * (_rms(y) * mp.g_post[...])


WEAVE = "msmmmsmssmssmsssm"


def _weave(order, mix, stage):
    gens = {"m": mix, "s": stage}
    for c in order:
        next(gens[c], None)
    for g in (mix, stage):
        for _ in g:
            pass


def _mixer_kernel(x_ref, xn_ref, mod_ref, modn_ref, pos_ref, posn_ref, invf_ref, g_pre_ref, g_post_ref,
                  w_in_ref, b_gate_ref, w_pool_ref, w_br_ref, w_out_ref, dmat_ref, qdec_ref, kdec_ref,
                  o_ref, state_ref, halo_ref, *staged, tile, steps_per_seq, n_steps, tile_decay):
    j = pl.program_id(0)
    s = j % steps_per_seq
    s_next = jnp.minimum(j + 1, n_steps - 1) % steps_per_seq
    sp = StageParams(g_pre_ref, w_in_ref, b_gate_ref, invf_ref, dmat_ref, qdec_ref, kdec_ref, halo_ref)
    mp = MixParams(g_post_ref, w_pool_ref, w_br_ref, w_out_ref, state_ref)
    n_fields = len(Staged._fields)
    slots = (Staged(*staged[:n_fields]), Staged(*staged[n_fields:]))
    whole = slice(0, tile)

    @pl.when(j == 0)
    def _():
        for _ in _stage_phases(x_ref, whole, mod_ref, pos_ref[...], 0, True, sp, slots[0]):
            pass

    @pl.when(s == 0)
    def _():
        state_ref[...] = jnp.zeros_like(state_ref)

    for parity in range(2):
        @pl.when(j % 2 == parity)
        def _():
            _weave(WEAVE,
                   _mix_phases(slots[parity], x_ref, whole, o_ref, mod_ref, mp, tile_decay),
                   _stage_phases(xn_ref, whole, modn_ref, posn_ref[...], s_next, s_next == 0, sp,
                                 slots[1 - parity]))


def _decay_tables(tile, hd):
    gamma = 1.0 - 2.0 ** (-5.0 - np.arange(RET_HEADS, dtype=np.float64))
    idx = np.arange(tile)
    dist = np.abs(idx[:, None] - idx[None, :])
    visible = (idx[None, :] // CHUNK) <= (idx[:, None] // CHUNK)
    dmat = np.where(visible[None], gamma[:, None, None] ** dist[None], 0.0)
    qdec = np.broadcast_to((gamma[:, None] ** (idx[None, :] + 1))[:, :, None], (RET_HEADS, tile, hd))
    kdec = np.broadcast_to((gamma[:, None] ** (tile - 1 - idx[None, :]))[:, :, None], (RET_HEADS, tile, hd))
    tile_decay = tuple(float(g) ** tile for g in gamma)
    as_f32 = lambda a: jnp.asarray(np.ascontiguousarray(a), dtype=F32)
    return as_f32(dmat), as_f32(qdec), as_f32(kdec), tile_decay


def _const_spec(shape):
    nd = len(shape)
    return pl.BlockSpec(shape, lambda j: (0,) * nd)


def _mixer(x, mod, positions, g_pre, g_post, w_in, b_gate, w_pool, w_br, w_out):
    bsz, seq, d = x.shape
    tile = MIXER_TILE
    pool_width = w_pool.shape[0]
    ret_width = w_br.shape[0]
    hd = ret_width // RET_HEADS
    assert seq % tile == 0 and tile % RET_BLOCK == 0 and RET_BLOCK % CHUNK == 0 and hd == LANES
    steps_per_seq = seq // tile
    n_steps = bsz * steps_per_seq
    dmat, qdec, kdec, tile_decay = _decay_tables(RET_BLOCK, hd)
    half = hd // 2
    invf = (ROPE_BASE ** (-jnp.arange(half, dtype=F32) / half)).reshape(half, 1)
    mod3 = mod.reshape(bsz, N_MOD, d)
    pos3 = positions.reshape(bsz, 1, seq)

    def cur(j):
        return j // steps_per_seq, j % steps_per_seq

    def nxt(j):
        return cur(jnp.minimum(j + 1, n_steps - 1))

    const = _const_spec
    operands = [
        (x, pl.BlockSpec((None, tile, d), lambda j: (*cur(j), 0))),
        (x, pl.BlockSpec((None, tile, d), lambda j: (*nxt(j), 0))),
        (mod3, pl.BlockSpec((None, N_MOD, d), lambda j: (cur(j)[0], 0, 0))),
        (mod3, pl.BlockSpec((None, N_MOD, d), lambda j: (nxt(j)[0], 0, 0))),
        (pos3, pl.BlockSpec((None, 1, tile), lambda j: (cur(j)[0], 0, cur(j)[1]))),
        (pos3, pl.BlockSpec((None, 1, tile), lambda j: (nxt(j)[0], 0, nxt(j)[1]))),
        (invf, const((half, 1))),
        (g_pre.reshape(1, d), const((1, d))),
        (g_post.reshape(1, d), const((1, d))),
        (w_in.astype(BF16), const(w_in.shape)),
        (b_gate.reshape(1, N_BRANCHES * d), const((1, N_BRANCHES * d))),
        (w_pool, const(w_pool.shape)),
        (w_br.astype(BF16), const(w_br.shape)),
        (w_out.astype(BF16), const(w_out.shape)),
        (dmat, const(dmat.shape)),
        (qdec, const(qdec.shape)),
        (kdec, const(kdec.shape)),
    ]
    staged = [
        pltpu.VMEM((RET_HEADS * (tile // RET_BLOCK), RET_BLOCK, RET_BLOCK), BF16),
        pltpu.VMEM((3, tile, ret_width), BF16),
        pltpu.VMEM((tile, pool_width), BF16),
        pltpu.VMEM((tile, ret_width), F32),
        pltpu.VMEM((tile, N_BRANCHES * d), F32),
    ]
    return pl.pallas_call(
        functools.partial(_mixer_kernel, tile=tile, steps_per_seq=steps_per_seq, n_steps=n_steps,
                          tile_decay=tile_decay),
        out_shape=jax.ShapeDtypeStruct(x.shape, F32),
        grid=(n_steps,),
        in_specs=[spec for _, spec in operands],
        out_specs=pl.BlockSpec((None, tile, d), lambda j: (*cur(j), 0)),
        scratch_shapes=[
            pltpu.VMEM((RET_HEADS, hd, hd), F32),
            pltpu.VMEM((POOL_HALO, pool_width), F32),
        ] + staged + staged,
        compiler_params=pltpu.CompilerParams(
            dimension_semantics=("arbitrary",),
            vmem_limit_bytes=VMEM_LIMIT_BYTES),
        name="mixer",
    )(*[a for a, _ in operands])


def _ffn_kernel(x_ref, mod_ref, g_pre_ref, g_post_ref, w1_hbm, w2_hbm, o_ref, w1_ref, w2_ref, sem):
    j = pl.program_id(0)
    half = w1_ref.shape[1] // 2
    lo, hi = slice(0, half), slice(half, 2 * half)
    dims = (((1,), (0,)), ((), ()))

    def weight_copies():
        return (pltpu.make_async_copy(w1_hbm.at[:, lo], w1_ref.at[:, lo], sem.at[0]),
                pltpu.make_async_copy(w1_hbm.at[:, hi], w1_ref.at[:, hi], sem.at[1]),
                pltpu.make_async_copy(w2_hbm, w2_ref, sem.at[2]))

    def up(h, cols):
        a = jnp.maximum(lax.dot_general(h, w1_ref[:, cols], dims, preferred_element_type=F32), 0.0)
        return (a * a).astype(BF16)

    def step(first):
        copies = weight_copies()
        if first:
            for cp in copies:
                cp.start()
        x = x_ref[...]
        sh_f, sc_f, gt_f = mod_ref[3:4, :], mod_ref[4:5, :], mod_ref[5:6, :]
        h = ((_rms(x) * g_pre_ref[...]) * (1.0 + sc_f) + sh_f).astype(BF16)
        if first:
            copies[0].wait()
            a_lo = up(h, lo)
            copies[1].wait()
            a = jnp.concatenate([a_lo, up(h, hi)], axis=1)
            copies[2].wait()
        else:
            a = up(h, slice(0, 2 * half))
        y = lax.dot_general(a, w2_ref[...], dims, preferred_element_type=F32)
        o_ref[...] = x + gt_f * (_rms(y) * g_post_ref[...])

    pl.when(j == 0)(functools.partial(step, True))
    pl.when(j > 0)(functools.partial(step, False))


def _ffn(x, mod, g_pre, g_post, w1, w2):
    bsz, seq, d = x.shape
    tile = FFN_TILE
    assert seq % tile == 0
    tiles_per_seq = seq // tile

    def cur(j):
        return j // tiles_per_seq, j % tiles_per_seq

    return pl.pallas_call(
        _ffn_kernel,
        out_shape=jax.ShapeDtypeStruct(x.shape, F32),
        grid=(bsz * tiles_per_seq,),
        in_specs=[
            pl.BlockSpec((None, tile, d), lambda j: (*cur(j), 0)),
            pl.BlockSpec((None, N_MOD, d), lambda j: (cur(j)[0], 0, 0)),
            _const_spec((1, d)),
            _const_spec((1, d)),
            pl.BlockSpec(memory_space=pl.ANY),
            pl.BlockSpec(memory_space=pl.ANY),
        ],
        out_specs=pl.BlockSpec((None, tile, d), lambda j: (*cur(j), 0)),
        scratch_shapes=[pltpu.VMEM(w1.shape, F32), pltpu.VMEM(w2.shape, F32),
                        pltpu.SemaphoreType.DMA((3,))],
        compiler_params=pltpu.CompilerParams(
            dimension_semantics=("arbitrary",),
            vmem_limit_bytes=VMEM_LIMIT_BYTES),
        name="ffn",
    )(x, mod.reshape(bsz, N_MOD, d), g_pre.reshape(1, d), g_post.reshape(1, d), w1, w2)


def kernel(x, c, positions, ada_w, ada_b, mix_pre_g, mix_post_g, ffn_pre_g, ffn_post_g, w_in,
           b_branch_gate, pool_w, pool_scale, w_branch_pool, w_branch_ret, w_out, w_ff1, w_ff2):
    depth = ada_w.shape[0]
    for l in range(depth):
        mod = _adaln(c, ada_w[l], ada_b[l])
        w_pool = _pool_fold(pool_w[l], pool_scale[l], w_branch_pool[l])
        x = _mixer(x, mod, positions, mix_pre_g[l], mix_post_g[l], w_in[l], b_branch_gate[l],
                   w_pool, w_branch_ret[l], w_out[l])
        x = _ffn(x, mod, ffn_pre_g[l], ffn_post_g[l], w_ff1[l], w_ff2[l])
    return x
```

```python
import functools
from typing import Any, NamedTuple

import jax
import jax.numpy as jnp
import numpy as np
from jax import lax
from jax.experimental import pallas as pl
from jax.experimental.pallas import tpu as pltpu

F32 = jnp.float32
BF16 = jnp.bfloat16

CHUNK = 64
POOL_WINDOWS = (2, 4, 8, 16)
POOL_HALO = 16
POOL_ROWS = 64
RET_HEADS = 4
N_BRANCHES = 2
N_MOD = 6
ROPE_BASE = 10000.0
EPS = 1e-6

LANES = 128
MIXER_TILE = 256
RET_BLOCK = 256
FFN_TILE = 512
FFN_WEIGHT_CHUNKS = 4
ADALN_BLOCK_N = 1536
VMEM_LIMIT_BYTES = 56 * 1024 * 1024


def _rms(xf):
    return xf * lax.rsqrt(jnp.mean(xf * xf, axis=-1, keepdims=True) + EPS)


def _adaln_kernel(c_ref, w_ref, b_ref, o_ref):
    c = c_ref[...]
    a = (c * jax.nn.sigmoid(c)).astype(BF16)
    o_ref[...] = jnp.dot(a, w_ref[...].astype(BF16), preferred_element_type=F32) + b_ref[...]


def _adaln(c, w, b):
    bsz, d = c.shape
    n = w.shape[1]
    return pl.pallas_call(
        _adaln_kernel,
        out_shape=jax.ShapeDtypeStruct((bsz, n), F32),
        grid=(n // ADALN_BLOCK_N,),
        in_specs=[
            pl.BlockSpec((bsz, d), lambda j: (0, 0)),
            pl.BlockSpec((d, ADALN_BLOCK_N), lambda j: (0, j)),
            pl.BlockSpec((1, ADALN_BLOCK_N), lambda j: (0, j)),
        ],
        out_specs=pl.BlockSpec((bsz, ADALN_BLOCK_N), lambda j: (0, j)),
        compiler_params=pltpu.CompilerParams(dimension_semantics=("arbitrary",)),
        name="adaln",
    )(c, w, b.reshape(1, n))


def _pool_fold_kernel(pw_ref, scale_ref, wbp_ref, o_ref):
    o_ref[...] = jnp.dot((pw_ref[...] * scale_ref[...]).astype(BF16), wbp_ref[...].astype(BF16),
                         preferred_element_type=F32).astype(o_ref.dtype)


def _pool_fold(pool_w, pool_scale, w_bp):
    groups, gd, _ = pool_w.shape
    d = w_bp.shape[1]
    return pl.pallas_call(
        _pool_fold_kernel,
        out_shape=jax.ShapeDtypeStruct(w_bp.shape, BF16),
        grid=(groups,),
        in_specs=[
            pl.BlockSpec((None, gd, gd), lambda g: (g, 0, 0)),
            pl.BlockSpec((1, gd), lambda g: (0, g)),
            pl.BlockSpec((gd, d), lambda g: (g, 0)),
        ],
        out_specs=pl.BlockSpec((gd, d), lambda g: (g, 0)),
        compiler_params=pltpu.CompilerParams(dimension_semantics=("arbitrary",)),
        name="pool_fold",
    )(pool_w, pool_scale.reshape(1, groups * gd), w_bp)


class Staged(NamedTuple):
    probs: Any
    rot: Any
    pooled: Any
    sg: Any
    gates: Any


class StageParams(NamedTuple):
    g_pre: Any
    w_in: Any
    b_gate: Any
    invf: Any
    dmat: Any
    qdec: Any
    kdec: Any
    halo: Any


class MixParams(NamedTuple):
    g_post: Any
    w_pool: Any
    w_br: Any
    w_out: Any
    state: Any


def _stage_phases(x_ref, rows, mod_ref, pos, seq_tile, first, sp: StageParams, out: Staged):
    tile = rows.stop - rows.start
    pool_width = out.pooled.shape[1]
    ret_width = out.sg.shape[1]
    d_model = out.gates.shape[1] // N_BRANCHES
    hd = ret_width // RET_HEADS
    sh_m, sc_m = mod_ref[0:1, :], mod_ref[1:2, :]
    h = ((_rms(x_ref[rows, :]) * sp.g_pre[...]) * (1.0 + sc_m) + sh_m).astype(BF16)
    yield

    def proj(lo, width):
        return jnp.dot(h, sp.w_in[:, lo:lo + width], preferred_element_type=F32)

    u = proj(0, pool_width)
    yield

    if first is True:
        halo = jnp.zeros(sp.halo.shape, F32)
    elif first is False:
        halo = sp.halo[...]
    else:
        halo = jnp.where(first, 0.0, sp.halo[...])
    sp.halo[...] = u[tile - POOL_HALO:tile, :]
    t_pos = (lax.broadcasted_iota(jnp.int32, (tile, LANES), 0) + (seq_tile * tile + 1)).astype(F32)
    inv_t = 1.0 / t_pos
    gd = pool_width // len(POOL_WINDOWS)
    for gi, w in enumerate(POOL_WINDOWS):
        cols = slice(gi * gd, (gi + 1) * gd)
        inv_cnt = jnp.where(t_pos < float(w), inv_t, 1.0 / w)
        for r0 in range(0, tile, POOL_ROWS):
            blk = slice(r0, r0 + POOL_ROWS)
            before = halo[:, cols] if r0 == 0 else u[r0 - POOL_HALO:r0, cols]
            s = jnp.concatenate([before, u[blk, cols]], axis=0)
            k = 1
            while k < w:
                s = s + pltpu.roll(s, k, 0)
                k *= 2
            out.pooled[blk, cols] = (s[POOL_HALO:] * inv_cnt[blk] - u[blk, cols]).astype(BF16)
    q = proj(pool_width, ret_width)
    yield
    k_all = proj(pool_width + ret_width, ret_width)
    yield

    ang = sp.invf[...] * pos.astype(F32)
    cos_h, sin_h = jnp.cos(ang), jnp.sin(ang)
    cos_t = jnp.transpose(jnp.concatenate([cos_h, cos_h], axis=0))
    sin_t = jnp.transpose(jnp.concatenate([-sin_h, sin_h], axis=0))
    blk = sp.dmat.shape[1]
    roped = []
    for hh in range(RET_HEADS):
        cols = slice(hh * hd, (hh + 1) * hd)
        qh, kh = q[:, cols], k_all[:, cols]
        qh = (qh * cos_t + pltpu.roll(qh, hd // 2, 1) * sin_t) * (hd ** -0.5)
        kh = kh * cos_t + pltpu.roll(kh, hd // 2, 1) * sin_t
        roped.append((qh.astype(BF16), kh.astype(BF16)))
        for r0 in range(0, tile, blk):
            out.rot[1, r0:r0 + blk, cols] = (qh[r0:r0 + blk] * sp.qdec[hh]).astype(BF16)
            out.rot[2, r0:r0 + blk, cols] = (kh[r0:r0 + blk] * sp.kdec[hh]).astype(BF16)
    out.rot[0] = proj(pool_width + 2 * ret_width, ret_width).astype(BF16)
    yield
    g = proj(pool_width + 3 * ret_width, ret_width)
    out.sg[...] = g * jax.nn.sigmoid(g)
    yield
    for hh, (qh, kh) in enumerate(roped):
        for bi, r0 in enumerate(range(0, tile, blk)):
            scores = lax.dot_general(qh[r0:r0 + blk], kh[r0:r0 + blk], (((1,), (1,)), ((), ())),
                                     preferred_element_type=F32)
            out.probs[hh * (tile // blk) + bi] = (scores * sp.dmat[hh]).astype(BF16)
    yield
    gate_lo = pool_width + 4 * ret_width
    for br in range(N_BRANCHES):
        cols = slice(br * d_model, (br + 1) * d_model)
        out.gates[:, cols] = jax.nn.sigmoid(proj(gate_lo + br * d_model, d_model) + sp.b_gate[:, cols])
        yield


def _mix_phases(st: Staged, x_ref, rows, o_ref, mod_ref, mp: MixParams, tile_decay):
    d_model = x_ref.shape[1]
    ret_width = st.sg.shape[1]
    hd = ret_width // RET_HEADS
    gt_m = mod_ref[2:3, :]

    tile = st.sg.shape[0]
    blk = st.probs.shape[1]
    n_blk = tile // blk
    gated = [[None] * RET_HEADS for _ in range(n_blk)]
    for bi in range(n_blk):
        blk_rows = slice(bi * blk, (bi + 1) * blk)
        for hh in range(RET_HEADS):
            cols = slice(hh * hd, (hh + 1) * hd)
            vh = st.rot[0, blk_rows, cols]
            o = jnp.dot(st.probs[hh * n_blk + bi], vh, preferred_element_type=F32)
            o = o + jnp.dot(st.rot[1, blk_rows, cols], mp.state[hh].astype(BF16),
                            preferred_element_type=F32)
            kv = lax.dot_general(st.rot[2, blk_rows, cols], vh, (((0,), (0,)), ((), ())),
                                 preferred_element_type=F32)
            mp.state[hh] = mp.state[hh] * tile_decay[hh] + kv
            mu = jnp.mean(o, axis=-1, keepdims=True)
            oc = o - mu
            var = jnp.mean(oc * oc, axis=-1, keepdims=True)
            gated[bi][hh] = (st.sg[blk_rows, cols] * (oc * lax.rsqrt(var + EPS))).astype(BF16)
            yield
    y_pool = jnp.dot(st.pooled[...], mp.w_pool[...], preferred_element_type=F32)
    yield
    gated = jnp.concatenate([jnp.concatenate(row, axis=-1) for row in gated], axis=0)
    y_ret = jnp.dot(gated, mp.w_br[...], preferred_element_type=F32)
    yield
    merged = (st.gates[:, 0:d_model] * y_pool + st.gates[:, d_model:2 * d_model] * y_ret).astype(BF16)
    y = jnp.dot(merged, mp.w_out[...], preferred_element_type=F32)
    yield
    o_ref[rows, :] = x_ref[rows, :] + gt_m * (_rms(y) * mp.g_post[...])


WEAVE = "msmmmsmssmssmsssm"


def _weave(order, mix, stage):
    gens = {"m": mix, "s": stage}
    for c in order:
        next(gens[c], None)
    for g in (mix, stage):
        for _ in g:
            pass


def _mixer_kernel(x_ref, xn_ref, mod_ref, modn_ref, pos_ref, posn_ref, invf_ref, g_pre_ref, g_post_ref,
                  w_in_ref, b_gate_ref, w_pool_ref, w_br_ref, w_out_ref, dmat_ref, qdec_ref, kdec_ref,
                  wf1_ref, wf2_ref, o_ref, wf1_o_ref, wf2_o_ref, state_ref, halo_ref, *staged,
                  tile, steps_per_seq, n_steps, tile_decay):
    j = pl.program_id(0)
    wf1_o_ref[...] = wf1_ref[...].astype(BF16)
    wf2_o_ref[...] = wf2_ref[...].astype(BF16)
    s = j % steps_per_seq
    s_next = jnp.minimum(j + 1, n_steps - 1) % steps_per_seq
    sp = StageParams(g_pre_ref, w_in_ref, b_gate_ref, invf_ref, dmat_ref, qdec_ref, kdec_ref, halo_ref)
    mp = MixParams(g_post_ref, w_pool_ref, w_br_ref, w_out_ref, state_ref)
    n_fields = len(Staged._fields)
    slots = (Staged(*staged[:n_fields]), Staged(*staged[n_fields:]))
    whole = slice(0, tile)

    @pl.when(j == 0)
    def _():
        for _ in _stage_phases(x_ref, whole, mod_ref, pos_ref[...], 0, True, sp, slots[0]):
            pass

    @pl.when(s == 0)
    def _():
        state_ref[...] = jnp.zeros_like(state_ref)

    for parity in range(2):
        @pl.when(j % 2 == parity)
        def _():
            _weave(WEAVE,
                   _mix_phases(slots[parity], x_ref, whole, o_ref, mod_ref, mp, tile_decay),
                   _stage_phases(xn_ref, whole, modn_ref, posn_ref[...], s_next, s_next == 0, sp,
                                 slots[1 - parity]))


def _decay_tables(tile, hd):
    gamma = 1.0 - 2.0 ** (-5.0 - np.arange(RET_HEADS, dtype=np.float64))
    idx = np.arange(tile)
    dist = np.abs(idx[:, None] - idx[None, :])
    visible = (idx[None, :] // CHUNK) <= (idx[:, None] // CHUNK)
    dmat = np.where(visible[None], gamma[:, None, None] ** dist[None], 0.0)
    qdec = np.broadcast_to((gamma[:, None] ** (idx[None, :] + 1))[:, :, None], (RET_HEADS, tile, hd))
    kdec = np.broadcast_to((gamma[:, None] ** (tile - 1 - idx[None, :]))[:, :, None], (RET_HEADS, tile, hd))
    tile_decay = tuple(float(g) ** tile for g in gamma)
    as_f32 = lambda a: jnp.asarray(np.ascontiguousarray(a), dtype=F32)
    return as_f32(dmat), as_f32(qdec), as_f32(kdec), tile_decay


def _const_spec(shape):
    nd = len(shape)
    return pl.BlockSpec(shape, lambda j: (0,) * nd)


def _mixer(x, mod, positions, g_pre, g_post, w_in, b_gate, w_pool, w_br, w_out, w_ffn):
    bsz, seq, d = x.shape
    tile = MIXER_TILE
    pool_width = w_pool.shape[0]
    ret_width = w_br.shape[0]
    hd = ret_width // RET_HEADS
    assert seq % tile == 0 and tile % RET_BLOCK == 0 and RET_BLOCK % CHUNK == 0 and hd == LANES
    steps_per_seq = seq // tile
    n_steps = bsz * steps_per_seq
    dmat, qdec, kdec, tile_decay = _decay_tables(RET_BLOCK, hd)
    half = hd // 2
    invf = (ROPE_BASE ** (-jnp.arange(half, dtype=F32) / half)).reshape(half, 1)
    mod3 = mod.reshape(bsz, N_MOD, d)
    pos3 = positions.reshape(bsz, 1, seq)

    def cur(j):
        return j // steps_per_seq, j % steps_per_seq

    def nxt(j):
        return cur(jnp.minimum(j + 1, n_steps - 1))

    const = _const_spec
    operands = [
        (x, pl.BlockSpec((None, tile, d), lambda j: (*cur(j), 0))),
        (x, pl.BlockSpec((None, tile, d), lambda j: (*nxt(j), 0))),
        (mod3, pl.BlockSpec((None, N_MOD, d), lambda j: (cur(j)[0], 0, 0))),
        (mod3, pl.BlockSpec((None, N_MOD, d), lambda j: (nxt(j)[0], 0, 0))),
        (pos3, pl.BlockSpec((None, 1, tile), lambda j: (cur(j)[0], 0, cur(j)[1]))),
        (pos3, pl.BlockSpec((None, 1, tile), lambda j: (nxt(j)[0], 0, nxt(j)[1]))),
        (invf, const((half, 1))),
        (g_pre.reshape(1, d), const((1, d))),
        (g_post.reshape(1, d), const((1, d))),
        (w_in.astype(BF16), const(w_in.shape)),
        (b_gate.reshape(1, N_BRANCHES * d), const((1, N_BRANCHES * d))),
        (w_pool, const(w_pool.shape)),
        (w_br.astype(BF16), const(w_br.shape)),
        (w_out.astype(BF16), const(w_out.shape)),
        (dmat, const(dmat.shape)),
        (qdec, const(qdec.shape)),
        (kdec, const(kdec.shape)),
    ] + [(w, pl.BlockSpec((w.shape[0] // n_steps, w.shape[1]), lambda j: (j, 0))) for w in w_ffn]
    assert all(w.shape[0] % (n_steps * 16) == 0 for w in w_ffn)
    staged = [
        pltpu.VMEM((RET_HEADS * (tile // RET_BLOCK), RET_BLOCK, RET_BLOCK), BF16),
        pltpu.VMEM((3, tile, ret_width), BF16),
        pltpu.VMEM((tile, pool_width), BF16),
        pltpu.VMEM((tile, ret_width), F32),
        pltpu.VMEM((tile, N_BRANCHES * d), F32),
    ]
    return pl.pallas_call(
        functools.partial(_mixer_kernel, tile=tile, steps_per_seq=steps_per_seq, n_steps=n_steps,
                          tile_decay=tile_decay),
        out_shape=[jax.ShapeDtypeStruct(x.shape, F32)]
                  + [jax.ShapeDtypeStruct(w.shape, BF16) for w in w_ffn],
        grid=(n_steps,),
        in_specs=[spec for _, spec in operands],
        out_specs=[pl.BlockSpec((None, tile, d), lambda j: (*cur(j), 0))]
                  + [pl.BlockSpec((w.shape[0] // n_steps, w.shape[1]), lambda j: (j, 0)) for w in w_ffn],
        scratch_shapes=[
            pltpu.VMEM((RET_HEADS, hd, hd), F32),
            pltpu.VMEM((POOL_HALO, pool_width), F32),
        ] + staged + staged,
        compiler_params=pltpu.CompilerParams(
            dimension_semantics=("arbitrary",),
            vmem_limit_bytes=VMEM_LIMIT_BYTES),
        name="mixer",
    )(*[a for a, _ in operands])


def _ffn_kernel(x_ref, mod_ref, g_pre_ref, g_post_ref, w1_hbm, w2_hbm, o_ref, w1_ref, w2_ref, sem):
    j = pl.program_id(0)
    d_ff = w1_ref.shape[1]
    width = d_ff // FFN_WEIGHT_CHUNKS
    chunks = [slice(c * width, (c + 1) * width) for c in range(FFN_WEIGHT_CHUNKS)]

    def weight_copies():
        return [(pltpu.make_async_copy(w1_hbm.at[:, ck], w1_ref.at[:, ck], sem.at[0, c]),
                 pltpu.make_async_copy(w2_hbm.at[ck, :], w2_ref.at[ck, :], sem.at[1, c]))
                for c, ck in enumerate(chunks)]

    def up(h, cols):
        a = jnp.maximum(jnp.dot(h, w1_ref[:, cols], preferred_element_type=F32), 0.0)
        return (a * a).astype(BF16)

    def step(first):
        copies = weight_copies()
        if first:
            for cp1, cp2 in copies:
                cp1.start()
                cp2.start()
        x = x_ref[...]
        sh_f, sc_f, gt_f = mod_ref[3:4, :], mod_ref[4:5, :], mod_ref[5:6, :]
        h = ((_rms(x) * g_pre_ref[...]) * (1.0 + sc_f) + sh_f).astype(BF16)
        if first:
            y = None
            for (cp1, cp2), ck in zip(copies, chunks):
                cp1.wait()
                a = up(h, ck)
                cp2.wait()
                part = jnp.dot(a, w2_ref[ck, :], preferred_element_type=F32)
                y = part if y is None else y + part
        else:
            y = jnp.dot(up(h, slice(0, d_ff)), w2_ref[...], preferred_element_type=F32)
        o_ref[...] = x + gt_f * (_rms(y) * g_post_ref[...])

    pl.when(j == 0)(functools.partial(step, True))
    pl.when(j > 0)(functools.partial(step, False))


def _ffn(x, mod, g_pre, g_post, w1, w2):
    bsz, seq, d = x.shape
    tile = FFN_TILE
    assert seq % tile == 0
    tiles_per_seq = seq // tile

    def cur(j):
        return j // tiles_per_seq, j % tiles_per_seq

    return pl.pallas_call(
        _ffn_kernel,
        out_shape=jax.ShapeDtypeStruct(x.shape, F32),
        grid=(bsz * tiles_per_seq,),
        in_specs=[
            pl.BlockSpec((None, tile, d), lambda j: (*cur(j), 0)),
            pl.BlockSpec((None, N_MOD, d), lambda j: (cur(j)[0], 0, 0)),
            _const_spec((1, d)),
            _const_spec((1, d)),
            pl.BlockSpec(memory_space=pl.ANY),
            pl.BlockSpec(memory_space=pl.ANY),
        ],
        out_specs=pl.BlockSpec((None, tile, d), lambda j: (*cur(j), 0)),
        scratch_shapes=[pltpu.VMEM(w1.shape, w1.dtype), pltpu.VMEM(w2.shape, w2.dtype),
                        pltpu.SemaphoreType.DMA((2, FFN_WEIGHT_CHUNKS))],
        compiler_params=pltpu.CompilerParams(
            dimension_semantics=("arbitrary",),
            vmem_limit_bytes=VMEM_LIMIT_BYTES),
        name="ffn",
    )(x, mod.reshape(bsz, N_MOD, d), g_pre.reshape(1, d), g_post.reshape(1, d), w1, w2)


def kernel(x, c, positions, ada_w, ada_b, mix_pre_g, mix_post_g, ffn_pre_g, ffn_post_g, w_in,
           b_branch_gate, pool_w, pool_scale, w_branch_pool, w_branch_ret, w_out, w_ff1, w_ff2):
    depth = ada_w.shape[0]
    for l in range(depth):
        mod = _adaln(c, ada_w[l], ada_b[l])
        w_pool = _pool_fold(pool_w[l], pool_scale[l], w_branch_pool[l])
        x, w1, w2 = _mixer(x, mod, positions, mix_pre_g[l], mix_post_g[l], w_in[l], b_branch_gate[l],
                           w_pool, w_branch_ret[l], w_out[l], (w_ff1[l], w_ff2[l]))
        x = _ffn(x, mod, ffn_pre_g[l], ffn_post_g[l], w1, w2)
    return x
```

```python
import functools
from typing import Any, NamedTuple

import jax
import jax.numpy as jnp
import numpy as np
from jax import lax
from jax.experimental import pallas as pl
from jax.experimental.pallas import tpu as pltpu

F32 = jnp.float32
BF16 = jnp.bfloat16

CHUNK = 64
POOL_WINDOWS = (2, 4, 8, 16)
POOL_HALO = 16
POOL_ROWS = 64
RET_HEADS = 4
N_BRANCHES = 2
N_MOD = 6
ROPE_BASE = 10000.0
EPS = 1e-6

LANES = 128
MIXER_TILE = 256
RET_BLOCK = 256
FFN_TILE = 1024
FFN_WEIGHT_CHUNKS = 4
ADALN_BLOCK_N = 1536
VMEM_LIMIT_BYTES = 56 * 1024 * 1024


def _rms(xf):
    return xf * lax.rsqrt(jnp.mean(xf * xf, axis=-1, keepdims=True) + EPS)


def _adaln_kernel(c_ref, w_ref, b_ref, o_ref):
    c = c_ref[...]
    a = (c * jax.nn.sigmoid(c)).astype(BF16)
    o_ref[...] = jnp.dot(a, w_ref[...].astype(BF16), preferred_element_type=F32) + b_ref[...]


def _adaln(c, w, b):
    bsz, d = c.shape
    n = w.shape[1]
    return pl.pallas_call(
        _adaln_kernel,
        out_shape=jax.ShapeDtypeStruct((bsz, n), F32),
        grid=(n // ADALN_BLOCK_N,),
        in_specs=[
            pl.BlockSpec((bsz, d), lambda j: (0, 0)),
            pl.BlockSpec((d, ADALN_BLOCK_N), lambda j: (0, j)),
            pl.BlockSpec((1, ADALN_BLOCK_N), lambda j: (0, j)),
        ],
        out_specs=pl.BlockSpec((bsz, ADALN_BLOCK_N), lambda j: (0, j)),
        compiler_params=pltpu.CompilerParams(dimension_semantics=("arbitrary",)),
        name="adaln",
    )(c, w, b.reshape(1, n))


def _pool_fold_kernel(pw_ref, scale_ref, wbp_ref, o_ref):
    o_ref[...] = jnp.dot((pw_ref[...] * scale_ref[...]).astype(BF16), wbp_ref[...].astype(BF16),
                         preferred_element_type=F32).astype(o_ref.dtype)


def _pool_fold(pool_w, pool_scale, w_bp):
    groups, gd, _ = pool_w.shape
    d = w_bp.shape[1]
    return pl.pallas_call(
        _pool_fold_kernel,
        out_shape=jax.ShapeDtypeStruct(w_bp.shape, BF16),
        grid=(groups,),
        in_specs=[
            pl.BlockSpec((None, gd, gd), lambda g: (g, 0, 0)),
            pl.BlockSpec((1, gd), lambda g: (0, g)),
            pl.BlockSpec((gd, d), lambda g: (g, 0)),
        ],
        out_specs=pl.BlockSpec((gd, d), lambda g: (g, 0)),
        compiler_params=pltpu.CompilerParams(dimension_semantics=("arbitrary",)),
        name="pool_fold",
    )(pool_w, pool_scale.reshape(1, groups * gd), w_bp)


class Staged(NamedTuple):
    probs: Any
    rot: Any
    pooled: Any
    sg: Any
    gates: Any


class StageParams(NamedTuple):
    g_pre: Any
    w_in: Any
    b_gate: Any
    invf: Any
    dmat: Any
    qdec: Any
    kdec: Any
    halo: Any


class MixParams(NamedTuple):
    g_post: Any
    w_pool: Any
    w_br: Any
    w_out: Any
    state: Any


def _stage_phases(x_ref, rows, mod_ref, pos, seq_tile, first, sp: StageParams, out: Staged):
    tile = rows.stop - rows.start
    pool_width = out.pooled.shape[1]
    ret_width = out.sg.shape[1]
    d_model = out.gates.shape[1] // N_BRANCHES
    hd = ret_width // RET_HEADS
    sh_m, sc_m = mod_ref[0:1, :], mod_ref[1:2, :]
    h = ((_rms(x_ref[rows, :]) * sp.g_pre[...]) * (1.0 + sc_m) + sh_m).astype(BF16)
    yield

    def proj(lo, width):
        return jnp.dot(h, sp.w_in[:, lo:lo + width], preferred_element_type=F32)

    u = proj(0, pool_width)
    yield

    if first is True:
        halo = jnp.zeros(sp.halo.shape, F32)
    elif first is False:
        halo = sp.halo[...]
    else:
        halo = jnp.where(first, 0.0, sp.halo[...])
    sp.halo[...] = u[tile - POOL_HALO:tile, :]
    t_pos = (lax.broadcasted_iota(jnp.int32, (tile, LANES), 0) + (seq_tile * tile + 1)).astype(F32)
    inv_t = 1.0 / t_pos
    gd = pool_width // len(POOL_WINDOWS)
    for gi, w in enumerate(POOL_WINDOWS):
        cols = slice(gi * gd, (gi + 1) * gd)
        inv_cnt = jnp.where(t_pos < float(w), inv_t, 1.0 / w)
        for r0 in range(0, tile, POOL_ROWS):
            blk = slice(r0, r0 + POOL_ROWS)
            before = halo[:, cols] if r0 == 0 else u[r0 - POOL_HALO:r0, cols]
            s = jnp.concatenate([before, u[blk, cols]], axis=0)
            k = 1
            while k < w:
                s = s + pltpu.roll(s, k, 0)
                k *= 2
            out.pooled[blk, cols] = (s[POOL_HALO:] * inv_cnt[blk] - u[blk, cols]).astype(BF16)
    q = proj(pool_width, ret_width)
    yield
    k_all = proj(pool_width + ret_width, ret_width)
    yield

    ang = sp.invf[...] * pos.astype(F32)
    cos_h, sin_h = jnp.cos(ang), jnp.sin(ang)
    cos_t = jnp.transpose(jnp.concatenate([cos_h, cos_h], axis=0))
    sin_t = jnp.transpose(jnp.concatenate([-sin_h, sin_h], axis=0))
    blk = sp.dmat.shape[1]
    roped = []
    for hh in range(RET_HEADS):
        cols = slice(hh * hd, (hh + 1) * hd)
        qh, kh = q[:, cols], k_all[:, cols]
        qh = (qh * cos_t + pltpu.roll(qh, hd // 2, 1) * sin_t) * (hd ** -0.5)
        kh = kh * cos_t + pltpu.roll(kh, hd // 2, 1) * sin_t
        roped.append((qh.astype(BF16), kh.astype(BF16)))
        for r0 in range(0, tile, blk):
            out.rot[1, r0:r0 + blk, cols] = (qh[r0:r0 + blk] * sp.qdec[hh]).astype(BF16)
            out.rot[2, r0:r0 + blk, cols] = (kh[r0:r0 + blk] * sp.kdec[hh]).astype(BF16)
    out.rot[0] = proj(pool_width + 2 * ret_width, ret_width).astype(BF16)
    yield
    g = proj(pool_width + 3 * ret_width, ret_width)
    out.sg[...] = g * jax.nn.sigmoid(g)
    yield
    for hh, (qh, kh) in enumerate(roped):
        for bi, r0 in enumerate(range(0, tile, blk)):
            scores = lax.dot_general(qh[r0:r0 + blk], kh[r0:r0 + blk], (((1,), (1,)), ((), ())),
                                     preferred_element_type=F32)
            out.probs[hh * (tile // blk) + bi] = (scores * sp.dmat[hh]).astype(BF16)
    yield
    gate_lo = pool_width + 4 * ret_width
    for br in range(N_BRANCHES):
        cols = slice(br * d_model, (br + 1) * d_model)
        out.gates[:, cols] = jax.nn.sigmoid(proj(gate_lo + br * d_model, d_model) + sp.b_gate[:, cols])
        yield


def _mix_phases(st: Staged, x_ref, rows, o_ref, mod_ref, mp: MixParams, tile_decay):
    d_model = x_ref.shape[1]
    ret_width = st.sg.shape[1]
    hd = ret_width // RET_HEADS
    gt_m = mod_ref[2:3, :]

    tile = st.sg.shape[0]
    blk = st.probs.shape[1]
    n_blk = tile // blk
    gated = [[None] * RET_HEADS for _ in range(n_blk)]
    for bi in range(n_blk):
        blk_rows = slice(bi * blk, (bi + 1) * blk)
        for hh in range(RET_HEADS):
            cols = slice(hh * hd, (hh + 1) * hd)
            vh = st.rot[0, blk_rows, cols]
            o = jnp.dot(st.probs[hh * n_blk + bi], vh, preferred_element_type=F32)
            o = o + jnp.dot(st.rot[1, blk_rows, cols], mp.state[hh].astype(BF16),
                            preferred_element_type=F32)
            kv = lax.dot_general(st.rot[2, blk_rows, cols], vh, (((0,), (0,)), ((), ())),
                                 preferred_element_type=F32)
            mp.state[hh] = mp.state[hh] * tile_decay[hh] + kv
            mu = jnp.mean(o, axis=-1, keepdims=True)
            oc = o - mu
            var = jnp.mean(oc * oc, axis=-1, keepdims=True)
            gated[bi][hh] = (st.sg[blk_rows, cols] * (oc * lax.rsqrt(var + EPS))).astype(BF16)
            yield
    y_pool = jnp.dot(st.pooled[...], mp.w_pool[...], preferred_element_type=F32)
    yield
    gated = jnp.concatenate([jnp.concatenate(row, axis=-1) for row in gated], axis=0)
    y_ret = jnp.dot(gated, mp.w_br[...], preferred_element_type=F32)
    yield
    merged = (st.gates[:, 0:d_model] * y_pool + st.gates[:, d_model:2 * d_model] * y_ret).astype(BF16)
    y = jnp.dot(merged, mp.w_out[...], preferred_element_type=F32)
    yield
    o_ref[rows, :] = x_ref[rows, :] + gt_m * (_rms(y) * mp.g_post[...])


WEAVE = "msmmmsmssmssmsssm"


def _weave(order, mix, stage):
    gens = {"m": mix, "s": stage}
    for c in order:
        next(gens[c], None)
    for g in (mix, stage):
        for _ in g:
            pass


def _mixer_kernel(x_ref, xn_ref, mod_ref, modn_ref, pos_ref, posn_ref, invf_ref, g_pre_ref, g_post_ref,
                  w_in_ref, b_gate_ref, w_pool_ref, w_br_ref, w_out_ref, dmat_ref, qdec_ref, kdec_ref,
                  wf1_ref, wf2_ref, o_ref, wf1_o_ref, wf2_o_ref, state_ref, halo_ref, *staged,
                  tile, steps_per_seq, n_steps, tile_decay):
    j = pl.program_id(0)
    wf1_o_ref[...] = wf1_ref[...].astype(BF16)
    wf2_o_ref[...] = wf2_ref[...].astype(BF16)
    s = j % steps_per_seq
    s_next = jnp.minimum(j + 1, n_steps - 1) % steps_per_seq
    sp = StageParams(g_pre_ref, w_in_ref, b_gate_ref, invf_ref, dmat_ref, qdec_ref, kdec_ref, halo_ref)
    mp = MixParams(g_post_ref, w_pool_ref, w_br_ref, w_out_ref, state_ref)
    n_fields = len(Staged._fields)
    slots = (Staged(*staged[:n_fields]), Staged(*staged[n_fields:]))
    whole = slice(0, tile)

    @pl.when(j == 0)
    def _():
        for _ in _stage_phases(x_ref, whole, mod_ref, pos_ref[...], 0, True, sp, slots[0]):
            pass

    @pl.when(s == 0)
    def _():
        state_ref[...] = jnp.zeros_like(state_ref)

    for parity in range(2):
        @pl.when(j % 2 == parity)
        def _():
            _weave(WEAVE,
                   _mix_phases(slots[parity], x_ref, whole, o_ref, mod_ref, mp, tile_decay),
                   _stage_phases(xn_ref, whole, modn_ref, posn_ref[...], s_next, s_next == 0, sp,
                                 slots[1 - parity]))


def _decay_tables(tile, hd):
    gamma = 1.0 - 2.0 ** (-5.0 - np.arange(RET_HEADS, dtype=np.float64))
    idx = np.arange(tile)
    dist = np.abs(idx[:, None] - idx[None, :])
    visible = (idx[None, :] // CHUNK) <= (idx[:, None] // CHUNK)
    dmat = np.where(visible[None], gamma[:, None, None] ** dist[None], 0.0)
    qdec = np.broadcast_to((gamma[:, None] ** (idx[None, :] + 1))[:, :, None], (RET_HEADS, tile, hd))
    kdec = np.broadcast_to((gamma[:, None] ** (tile - 1 - idx[None, :]))[:, :, None], (RET_HEADS, tile, hd))
    tile_decay = tuple(float(g) ** tile for g in gamma)
    as_f32 = lambda a: jnp.asarray(np.ascontiguousarray(a), dtype=F32)
    return as_f32(dmat), as_f32(qdec), as_f32(kdec), tile_decay


def _const_spec(shape):
    nd = len(shape)
    return pl.BlockSpec(shape, lambda j: (0,) * nd)


def _mixer(x, mod, positions, g_pre, g_post, w_in, b_gate, w_pool, w_br, w_out, w_ffn):
    bsz, seq, d = x.shape
    tile = MIXER_TILE
    pool_width = w_pool.shape[0]
    ret_width = w_br.shape[0]
    hd = ret_width // RET_HEADS
    assert seq % tile == 0 and tile % RET_BLOCK == 0 and RET_BLOCK % CHUNK == 0 and hd == LANES
    steps_per_seq = seq // tile
    n_steps = bsz * steps_per_seq
    dmat, qdec, kdec, tile_decay = _decay_tables(RET_BLOCK, hd)
    half = hd // 2
    invf = (ROPE_BASE ** (-jnp.arange(half, dtype=F32) / half)).reshape(half, 1)
    mod3 = mod.reshape(bsz, N_MOD, d)
    pos3 = positions.reshape(bsz, 1, seq)

    def cur(j):
        return j // steps_per_seq, j % steps_per_seq

    def nxt(j):
        return cur(jnp.minimum(j + 1, n_steps - 1))

    const = _const_spec
    operands = [
        (x, pl.BlockSpec((None, tile, d), lambda j: (*cur(j), 0))),
        (x, pl.BlockSpec((None, tile, d), lambda j: (*nxt(j), 0))),
        (mod3, pl.BlockSpec((None, N_MOD, d), lambda j: (cur(j)[0], 0, 0))),
        (mod3, pl.BlockSpec((None, N_MOD, d), lambda j: (nxt(j)[0], 0, 0))),
        (pos3, pl.BlockSpec((None, 1, tile), lambda j: (cur(j)[0], 0, cur(j)[1]))),
        (pos3, pl.BlockSpec((None, 1, tile), lambda j: (nxt(j)[0], 0, nxt(j)[1]))),
        (invf, const((half, 1))),
        (g_pre.reshape(1, d), const((1, d))),
        (g_post.reshape(1, d), const((1, d))),
        (w_in.astype(BF16), const(w_in.shape)),
        (b_gate.reshape(1, N_BRANCHES * d), const((1, N_BRANCHES * d))),
        (w_pool, const(w_pool.shape)),
        (w_br.astype(BF16), const(w_br.shape)),
        (w_out.astype(BF16), const(w_out.shape)),
        (dmat, const(dmat.shape)),
        (qdec, const(qdec.shape)),
        (kdec, const(kdec.shape)),
    ] + [(w, pl.BlockSpec((w.shape[0] // n_steps, w.shape[1]), lambda j: (j, 0))) for w in w_ffn]
    assert all(w.shape[0] % (n_steps * 16) == 0 for w in w_ffn)
    staged = [
        pltpu.VMEM((RET_HEADS * (tile // RET_BLOCK), RET_BLOCK, RET_BLOCK), BF16),
        pltpu.VMEM((3, tile, ret_width), BF16),
        pltpu.VMEM((tile, pool_width), BF16),
        pltpu.VMEM((tile, ret_width), F32),
        pltpu.VMEM((tile, N_BRANCHES * d), F32),
    ]
    return pl.pallas_call(
        functools.partial(_mixer_kernel, tile=tile, steps_per_seq=steps_per_seq, n_steps=n_steps,
                          tile_decay=tile_decay),
        out_shape=[jax.ShapeDtypeStruct(x.shape, F32)]
                  + [jax.ShapeDtypeStruct(w.shape, BF16) for w in w_ffn],
        grid=(n_steps,),
        in_specs=[spec for _, spec in operands],
        out_specs=[pl.BlockSpec((None, tile, d), lambda j: (*cur(j), 0))]
                  + [pl.BlockSpec((w.shape[0] // n_steps, w.shape[1]), lambda j: (j, 0)) for w in w_ffn],
        scratch_shapes=[
            pltpu.VMEM((RET_HEADS, hd, hd), F32),
            pltpu.VMEM((POOL_HALO, pool_width), F32),
        ] + staged + staged,
        compiler_params=pltpu.CompilerParams(
            dimension_semantics=("arbitrary",),
            vmem_limit_bytes=VMEM_LIMIT_BYTES),
        name="mixer",
    )(*[a for a, _ in operands])


def _ffn_kernel(x_ref, mod_ref, g_pre_ref, g_post_ref, w1_hbm, w2_hbm, o_ref, w1_ref, w2_ref, sem):
    j = pl.program_id(0)
    d_ff = w1_ref.shape[1]
    width = d_ff // FFN_WEIGHT_CHUNKS
    chunks = [slice(c * width, (c + 1) * width) for c in range(FFN_WEIGHT_CHUNKS)]

    def weight_copies():
        return [(pltpu.make_async_copy(w1_hbm.at[:, ck], w1_ref.at[:, ck], sem.at[0, c]),
                 pltpu.make_async_copy(w2_hbm.at[ck, :], w2_ref.at[ck, :], sem.at[1, c]))
                for c, ck in enumerate(chunks)]

    def up(h, cols):
        a = jnp.maximum(jnp.dot(h, w1_ref[:, cols], preferred_element_type=F32), 0.0)
        return (a * a).astype(BF16)

    def step(first):
        copies = weight_copies()
        if first:
            for cp1, cp2 in copies:
                cp1.start()
                cp2.start()
        x = x_ref[...]
        sh_f, sc_f, gt_f = mod_ref[3:4, :], mod_ref[4:5, :], mod_ref[5:6, :]
        h = ((_rms(x) * g_pre_ref[...]) * (1.0 + sc_f) + sh_f).astype(BF16)
        if first:
            y = None
            for (cp1, cp2), ck in zip(copies, chunks):
                cp1.wait()
                a = up(h, ck)
                cp2.wait()
                part = jnp.dot(a, w2_ref[ck, :], preferred_element_type=F32)
                y = part if y is None else y + part
        else:
            y = jnp.dot(up(h, slice(0, d_ff)), w2_ref[...], preferred_element_type=F32)
        o_ref[...] = x + gt_f * (_rms(y) * g_post_ref[...])

    pl.when(j == 0)(functools.partial(step, True))
    pl.when(j > 0)(functools.partial(step, False))


def _ffn(x, mod, g_pre, g_post, w1, w2):
    bsz, seq, d = x.shape
    tile = FFN_TILE
    assert seq % tile == 0
    tiles_per_seq = seq // tile

    def cur(j):
        return j // tiles_per_seq, j % tiles_per_seq

    return pl.pallas_call(
        _ffn_kernel,
        out_shape=jax.ShapeDtypeStruct(x.shape, F32),
        grid=(bsz * tiles_per_seq,),
        in_specs=[
            pl.BlockSpec((None, tile, d), lambda j: (*cur(j), 0)),
            pl.BlockSpec((None, N_MOD, d), lambda j: (cur(j)[0], 0, 0)),
            _const_spec((1, d)),
            _const_spec((1, d)),
            pl.BlockSpec(memory_space=pl.ANY),
            pl.BlockSpec(memory_space=pl.ANY),
        ],
        out_specs=pl.BlockSpec((None, tile, d), lambda j: (*cur(j), 0)),
        scratch_shapes=[pltpu.VMEM(w1.shape, w1.dtype), pltpu.VMEM(w2.shape, w2.dtype),
                        pltpu.SemaphoreType.DMA((2, FFN_WEIGHT_CHUNKS))],
        compiler_params=pltpu.CompilerParams(
            dimension_semantics=("arbitrary",),
            vmem_limit_bytes=VMEM_LIMIT_BYTES),
        name="ffn",
    )(x, mod.reshape(bsz, N_MOD, d), g_pre.reshape(1, d), g_post.reshape(1, d), w1, w2)


def kernel(x, c, positions, ada_w, ada_b, mix_pre_g, mix_post_g, ffn_pre_g, ffn_post_g, w_in,
           b_branch_gate, pool_w, pool_scale, w_branch_pool, w_branch_ret, w_out, w_ff1, w_ff2):
    depth = ada_w.shape[0]
    for l in range(depth):
        mod = _adaln(c, ada_w[l], ada_b[l])
        w_pool = _pool_fold(pool_w[l], pool_scale[l], w_branch_pool[l])
        x, w1, w2 = _mixer(x, mod, positions, mix_pre_g[l], mix_post_g[l], w_in[l], b_branch_gate[l],
                           w_pool, w_branch_ret[l], w_out[l], (w_ff1[l], w_ff2[l]))
        x = _ffn(x, mod, ffn_pre_g[l], ffn_post_g[l], w1, w2)
    return x
```

```python
import functools
from typing import Any, NamedTuple

import jax
import jax.numpy as jnp
import numpy as np
from jax import lax
from jax.experimental import pallas as pl
from jax.experimental.pallas import tpu as pltpu

F32 = jnp.float32
BF16 = jnp.bfloat16

CHUNK = 64
POOL_WINDOWS = (2, 4, 8, 16)
POOL_HALO = 16
POOL_ROWS = 64
RET_HEADS = 4
N_BRANCHES = 2
N_MOD = 6
ROPE_BASE = 10000.0
EPS = 1e-6

LANES = 128
MIXER_TILE = 256
RET_BLOCK = 256
FFN_TILE = 512
FFN_WEIGHT_CHUNKS = 4
ADALN_BLOCK_N = 768
VMEM_LIMIT_BYTES = 56 * 1024 * 1024


def _rms(xf):
    return xf * lax.rsqrt(jnp.mean(xf * xf, axis=-1, keepdims=True) + EPS)


def _adaln_kernel(c_ref, w_ref, b_ref, *refs):
    n_cast = len(refs) // 2
    cast_in, o_ref, cast_out = refs[:n_cast], refs[n_cast], refs[n_cast + 1:]
    c = c_ref[...]
    a = (c * jax.nn.sigmoid(c)).astype(BF16)
    o_ref[...] = jnp.dot(a, w_ref[...].astype(BF16), preferred_element_type=F32) + b_ref[...]
    for src, dst in zip(cast_in, cast_out):
        dst[...] = src[...].astype(BF16)


def _adaln(c, w, b, to_cast):
    bsz, d = c.shape
    n = w.shape[1]
    n_steps = n // ADALN_BLOCK_N
    assert all(m.shape[0] % (n_steps * 16) == 0 for m in to_cast)
    cast_specs = [pl.BlockSpec((m.shape[0] // n_steps, m.shape[1]), lambda j: (j, 0)) for m in to_cast]
    return pl.pallas_call(
        _adaln_kernel,
        out_shape=[jax.ShapeDtypeStruct((bsz, n), F32)]
                  + [jax.ShapeDtypeStruct(m.shape, BF16) for m in to_cast],
        grid=(n_steps,),
        in_specs=[
            pl.BlockSpec((bsz, d), lambda j: (0, 0)),
            pl.BlockSpec((d, ADALN_BLOCK_N), lambda j: (0, j)),
            pl.BlockSpec((1, ADALN_BLOCK_N), lambda j: (0, j)),
        ] + cast_specs,
        out_specs=[pl.BlockSpec((bsz, ADALN_BLOCK_N), lambda j: (0, j))] + cast_specs,
        compiler_params=pltpu.CompilerParams(dimension_semantics=("arbitrary",)),
        name="adaln",
    )(c, w, b.reshape(1, n), *to_cast)


def _pool_fold_kernel(pw_ref, scale_ref, wbp_ref, o_ref):
    groups, gd, _ = pw_ref.shape
    for g in range(groups):
        rows = slice(g * gd, (g + 1) * gd)
        o_ref[rows, :] = jnp.dot((pw_ref[g] * scale_ref[:, rows]).astype(BF16), wbp_ref[rows, :].astype(BF16),
                                 preferred_element_type=F32).astype(o_ref.dtype)


def _pool_fold(pool_w, pool_scale, w_bp):
    groups, gd, _ = pool_w.shape
    whole = lambda shape: pl.BlockSpec(shape, lambda j: (0,) * len(shape))
    return pl.pallas_call(
        _pool_fold_kernel,
        out_shape=jax.ShapeDtypeStruct(w_bp.shape, BF16),
        grid=(1,),
        in_specs=[whole(pool_w.shape), whole((1, groups * gd)), whole(w_bp.shape)],
        out_specs=whole(w_bp.shape),
        compiler_params=pltpu.CompilerParams(dimension_semantics=("arbitrary",)),
        name="pool_fold",
    )(pool_w, pool_scale.reshape(1, groups * gd), w_bp)


class Staged(NamedTuple):
    probs: Any
    rot: Any
    pooled: Any
    sg: Any
    gates: Any


class StageParams(NamedTuple):
    g_pre: Any
    w_in: Any
    b_gate: Any
    invf: Any
    dmat: Any
    qdec: Any
    kdec: Any
    halo: Any


class MixParams(NamedTuple):
    g_post: Any
    w_pool: Any
    w_br: Any
    w_out: Any
    state: Any


def _stage_phases(x_ref, rows, mod_ref, pos, seq_tile, first, sp: StageParams, out: Staged):
    tile = rows.stop - rows.start
    pool_width = out.pooled.shape[1]
    ret_width = out.sg.shape[1]
    d_model = out.gates.shape[1] // N_BRANCHES
    hd = ret_width // RET_HEADS
    sh_m, sc_m = mod_ref[0:1, :], mod_ref[1:2, :]
    h = ((_rms(x_ref[rows, :]) * sp.g_pre[...]) * (1.0 + sc_m) + sh_m).astype(BF16)
    yield

    def proj(lo, width):
        return jnp.dot(h, sp.w_in[:, lo:lo + width], preferred_element_type=F32)

    u = proj(0, pool_width)
    yield

    if first is True:
        halo = jnp.zeros(sp.halo.shape, F32)
    elif first is False:
        halo = sp.halo[...]
    else:
        halo = jnp.where(first, 0.0, sp.halo[...])
    sp.halo[...] = u[tile - POOL_HALO:tile, :]
    t_pos = (lax.broadcasted_iota(jnp.int32, (tile, LANES), 0) + (seq_tile * tile + 1)).astype(F32)
    inv_t = 1.0 / t_pos
    gd = pool_width // len(POOL_WINDOWS)
    for gi, w in enumerate(POOL_WINDOWS):
        cols = slice(gi * gd, (gi + 1) * gd)
        inv_cnt = jnp.where(t_pos < float(w), inv_t, 1.0 / w)
        for r0 in range(0, tile, POOL_ROWS):
            blk = slice(r0, r0 + POOL_ROWS)
            before = halo[:, cols] if r0 == 0 else u[r0 - POOL_HALO:r0, cols]
            s = jnp.concatenate([before, u[blk, cols]], axis=0)
            k = 1
            while k < w:
                s = s + pltpu.roll(s, k, 0)
                k *= 2
            out.pooled[blk, cols] = (s[POOL_HALO:] * inv_cnt[blk] - u[blk, cols]).astype(BF16)
    q = proj(pool_width, ret_width)
    yield
    k_all = proj(pool_width + ret_width, ret_width)
    yield

    ang = sp.invf[...] * pos.astype(F32)
    cos_h, sin_h = jnp.cos(ang), jnp.sin(ang)
    cos_t = jnp.transpose(jnp.concatenate([cos_h, cos_h], axis=0))
    sin_t = jnp.transpose(jnp.concatenate([-sin_h, sin_h], axis=0))
    blk = sp.dmat.shape[1]
    roped = []
    for hh in range(RET_HEADS):
        cols = slice(hh * hd, (hh + 1) * hd)
        qh, kh = q[:, cols], k_all[:, cols]
        qh = (qh * cos_t + pltpu.roll(qh, hd // 2, 1) * sin_t) * (hd ** -0.5)
        kh = kh * cos_t + pltpu.roll(kh, hd // 2, 1) * sin_t
        roped.append((qh.astype(BF16), kh.astype(BF16)))
        for r0 in range(0, tile, blk):
            out.rot[1, r0:r0 + blk, cols] = (qh[r0:r0 + blk] * sp.qdec[hh]).astype(BF16)
            out.rot[2, r0:r0 + blk, cols] = (kh[r0:r0 + blk] * sp.kdec[hh]).astype(BF16)
    out.rot[0] = proj(pool_width + 2 * ret_width, ret_width).astype(BF16)
    yield
    g = proj(pool_width + 3 * ret_width, ret_width)
    out.sg[...] = g * jax.nn.sigmoid(g)
    yield
    for hh, (qh, kh) in enumerate(roped):
        for bi, r0 in enumerate(range(0, tile, blk)):
            scores = lax.dot_general(qh[r0:r0 + blk], kh[r0:r0 + blk], (((1,), (1,)), ((), ())),
                                     preferred_element_type=F32)
            out.probs[hh * (tile // blk) + bi] = (scores * sp.dmat[hh]).astype(BF16)
    yield
    gate_lo = pool_width + 4 * ret_width
    for br in range(N_BRANCHES):
        cols = slice(br * d_model, (br + 1) * d_model)
        out.gates[:, cols] = jax.nn.sigmoid(proj(gate_lo + br * d_model, d_model) + sp.b_gate[:, cols])
        yield


def _mix_phases(st: Staged, x_ref, rows, o_ref, mod_ref, mp: MixParams, tile_decay):
    d_model = x_ref.shape[1]
    ret_width = st.sg.shape[1]
    hd = ret_width // RET_HEADS
    gt_m = mod_ref[2:3, :]

    tile = st.sg.shape[0]
    blk = st.probs.shape[1]
    n_blk = tile // blk
    gated = [[None] * RET_HEADS for _ in range(n_blk)]
    for bi in range(n_blk):
        blk_rows = slice(bi * blk, (bi + 1) * blk)
        for hh in range(RET_HEADS):
            cols = slice(hh * hd, (hh + 1) * hd)
            vh = st.rot[0, blk_rows, cols]
            o = jnp.dot(st.probs[hh * n_blk + bi], vh, preferred_element_type=F32)
            o = o + jnp.dot(st.rot[1, blk_rows, cols], mp.state[hh].astype(BF16),
                            preferred_element_type=F32)
            kv = lax.dot_general(st.rot[2, blk_rows, cols], vh, (((0,), (0,)), ((), ())),
                                 preferred_element_type=F32)
            mp.state[hh] = mp.state[hh] * tile_decay[hh] + kv
            mu = jnp.mean(o, axis=-1, keepdims=True)
            oc = o - mu
            var = jnp.mean(oc * oc, axis=-1, keepdims=True)
            gated[bi][hh] = (st.sg[blk_rows, cols] * (oc * lax.rsqrt(var + EPS))).astype(BF16)
            yield
    y_pool = jnp.dot(st.pooled[...], mp.w_pool[...], preferred_element_type=F32)
    yield
    gated = jnp.concatenate([jnp.concatenate(row, axis=-1) for row in gated], axis=0)
    y_ret = jnp.dot(gated, mp.w_br[...], preferred_element_type=F32)
    yield
    merged = (st.gates[:, 0:d_model] * y_pool + st.gates[:, d_model:2 * d_model] * y_ret).astype(BF16)
    y = jnp.dot(merged, mp.w_out[...], preferred_element_type=F32)
    yield
    o_ref[rows, :] = x_ref[rows, :] + gt_m * (_rms(y) * mp.g_post[...])


WEAVE = "msmmmsmssmssmsssm"


def _weave(order, mix, stage):
    gens = {"m": mix, "s": stage}
    for c in order:
        next(gens[c], None)
    for g in (mix, stage):
        for _ in g:
            pass


def _mixer_kernel(x_ref, xn_ref, mod_ref, modn_ref, pos_ref, posn_ref, invf_ref, g_pre_ref, g_post_ref,
                  w_in_ref, b_gate_ref, w_pool_ref, w_br_ref, w_out_ref, dmat_ref, qdec_ref, kdec_ref,
                  wf1_ref, wf2_ref, o_ref, wf1_o_ref, wf2_o_ref, state_ref, halo_ref, *staged,
                  tile, steps_per_seq, n_steps, tile_decay):
    j = pl.program_id(0)
    wf1_o_ref[...] = wf1_ref[...].astype(BF16)
    wf2_o_ref[...] = wf2_ref[...].astype(BF16)
    s = j % steps_per_seq
    s_next = jnp.minimum(j + 1, n_steps - 1) % steps_per_seq
    sp = StageParams(g_pre_ref, w_in_ref, b_gate_ref, invf_ref, dmat_ref, qdec_ref, kdec_ref, halo_ref)
    mp = MixParams(g_post_ref, w_pool_ref, w_br_ref, w_out_ref, state_ref)
    n_fields = len(Staged._fields)
    slots = (Staged(*staged[:n_fields]), Staged(*staged[n_fields:]))
    whole = slice(0, tile)

    @pl.when(j == 0)
    def _():
        for _ in _stage_phases(x_ref, whole, mod_ref, pos_ref[...], 0, True, sp, slots[0]):
            pass

    @pl.when(s == 0)
    def _():
        state_ref[...] = jnp.zeros_like(state_ref)

    for parity in range(2):
        @pl.when(j % 2 == parity)
        def _():
            _weave(WEAVE,
                   _mix_phases(slots[parity], x_ref, whole, o_ref, mod_ref, mp, tile_decay),
                   _stage_phases(xn_ref, whole, modn_ref, posn_ref[...], s_next, s_next == 0, sp,
                                 slots[1 - parity]))


def _decay_tables(tile, hd):
    gamma = 1.0 - 2.0 ** (-5.0 - np.arange(RET_HEADS, dtype=np.float64))
    idx = np.arange(tile)
    dist = np.abs(idx[:, None] - idx[None, :])
    visible = (idx[None, :] // CHUNK) <= (idx[:, None] // CHUNK)
    dmat = np.where(visible[None], gamma[:, None, None] ** dist[None], 0.0)
    qdec = np.broadcast_to((gamma[:, None] ** (idx[None, :] + 1))[:, :, None], (RET_HEADS, tile, hd))
    kdec = np.broadcast_to((gamma[:, None] ** (tile - 1 - idx[None, :]))[:, :, None], (RET_HEADS, tile, hd))
    tile_decay = tuple(float(g) ** tile for g in gamma)
    as_f32 = lambda a: jnp.asarray(np.ascontiguousarray(a), dtype=F32)
    return as_f32(dmat), as_f32(qdec), as_f32(kdec), tile_decay


def _const_spec(shape):
    nd = len(shape)
    return pl.BlockSpec(shape, lambda j: (0,) * nd)


def _mixer(x, mod, positions, g_pre, g_post, w_in, b_gate, w_pool, w_br, w_out, w_ffn):
    bsz, seq, d = x.shape
    tile = MIXER_TILE
    pool_width = w_pool.shape[0]
    ret_width = w_br.shape[0]
    hd = ret_width // RET_HEADS
    assert seq % tile == 0 and tile % RET_BLOCK == 0 and RET_BLOCK % CHUNK == 0 and hd == LANES
    steps_per_seq = seq // tile
    n_steps = bsz * steps_per_seq
    dmat, qdec, kdec, tile_decay = _decay_tables(RET_BLOCK, hd)
    half = hd // 2
    invf = (ROPE_BASE ** (-jnp.arange(half, dtype=F32) / half)).reshape(half, 1)
    mod3 = mod.reshape(bsz, N_MOD, d)
    pos3 = positions.reshape(bsz, 1, seq)

    def cur(j):
        return j // steps_per_seq, j % steps_per_seq

    def nxt(j):
        return cur(jnp.minimum(j + 1, n_steps - 1))

    const = _const_spec
    operands = [
        (x, pl.BlockSpec((None, tile, d), lambda j: (*cur(j), 0))),
        (x, pl.BlockSpec((None, tile, d), lambda j: (*nxt(j), 0))),
        (mod3, pl.BlockSpec((None, N_MOD, d), lambda j: (cur(j)[0], 0, 0))),
        (mod3, pl.BlockSpec((None, N_MOD, d), lambda j: (nxt(j)[0], 0, 0))),
        (pos3, pl.BlockSpec((None, 1, tile), lambda j: (cur(j)[0], 0, cur(j)[1]))),
        (pos3, pl.BlockSpec((None, 1, tile), lambda j: (nxt(j)[0], 0, nxt(j)[1]))),
        (invf, const((half, 1))),
        (g_pre.reshape(1, d), const((1, d))),
        (g_post.reshape(1, d), const((1, d))),
        (w_in, const(w_in.shape)),
        (b_gate.reshape(1, N_BRANCHES * d), const((1, N_BRANCHES * d))),
        (w_pool, const(w_pool.shape)),
        (w_br, const(w_br.shape)),
        (w_out, const(w_out.shape)),
        (dmat, const(dmat.shape)),
        (qdec, const(qdec.shape)),
        (kdec, const(kdec.shape)),
    ] + [(w, pl.BlockSpec((w.shape[0] // n_steps, w.shape[1]), lambda j: (j, 0))) for w in w_ffn]
    assert all(w.shape[0] % (n_steps * 16) == 0 for w in w_ffn)
    staged = [
        pltpu.VMEM((RET_HEADS * (tile // RET_BLOCK), RET_BLOCK, RET_BLOCK), BF16),
        pltpu.VMEM((3, tile, ret_width), BF16),
        pltpu.VMEM((tile, pool_width), BF16),
        pltpu.VMEM((tile, ret_width), F32),
        pltpu.VMEM((tile, N_BRANCHES * d), F32),
    ]
    return pl.pallas_call(
        functools.partial(_mixer_kernel, tile=tile, steps_per_seq=steps_per_seq, n_steps=n_steps,
                          tile_decay=tile_decay),
        out_shape=[jax.ShapeDtypeStruct(x.shape, F32)]
                  + [jax.ShapeDtypeStruct(w.shape, BF16) for w in w_ffn],
        grid=(n_steps,),
        in_specs=[spec for _, spec in operands],
        out_specs=[pl.BlockSpec((None, tile, d), lambda j: (*cur(j), 0))]
                  + [pl.BlockSpec((w.shape[0] // n_steps, w.shape[1]), lambda j: (j, 0)) for w in w_ffn],
        scratch_shapes=[
            pltpu.VMEM((RET_HEADS, hd, hd), F32),
            pltpu.VMEM((POOL_HALO, pool_width), F32),
        ] + staged + staged,
        compiler_params=pltpu.CompilerParams(
            dimension_semantics=("arbitrary",),
            vmem_limit_bytes=VMEM_LIMIT_BYTES),
        name="mixer",
    )(*[a for a, _ in operands])


def _ffn_kernel(x_ref, mod_ref, g_pre_ref, g_post_ref, w1_hbm, w2_hbm, o_ref, w1_ref, w2_ref, sem):
    j = pl.program_id(0)
    d_ff = w1_ref.shape[1]
    width = d_ff // FFN_WEIGHT_CHUNKS
    chunks = [slice(c * width, (c + 1) * width) for c in range(FFN_WEIGHT_CHUNKS)]

    def weight_copies():
        return [(pltpu.make_async_copy(w1_hbm.at[:, ck], w1_ref.at[:, ck], sem.at[0, c]),
                 pltpu.make_async_copy(w2_hbm.at[ck, :], w2_ref.at[ck, :], sem.at[1, c]))
                for c, ck in enumerate(chunks)]

    def up(h, cols):
        a = jnp.maximum(jnp.dot(h, w1_ref[:, cols], preferred_element_type=F32), 0.0)
        return (a * a).astype(BF16)

    def step(first):
        copies = weight_copies()
        if first:
            for cp1, cp2 in copies:
                cp1.start()
                cp2.start()
        x = x_ref[...]
        sh_f, sc_f, gt_f = mod_ref[3:4, :], mod_ref[4:5, :], mod_ref[5:6, :]
        h = ((_rms(x) * g_pre_ref[...]) * (1.0 + sc_f) + sh_f).astype(BF16)
        if first:
            y = None
            for (cp1, cp2), ck in zip(copies, chunks):
                cp1.wait()
                a = up(h, ck)
                cp2.wait()
                part = jnp.dot(a, w2_ref[ck, :], preferred_element_type=F32)
                y = part if y is None else y + part
        else:
            y = jnp.dot(up(h, slice(0, d_ff)), w2_ref[...], preferred_element_type=F32)
        o_ref[...] = x + gt_f * (_rms(y) * g_post_ref[...])

    pl.when(j == 0)(functools.partial(step, True))
    pl.when(j > 0)(functools.partial(step, False))


def _ffn(x, mod, g_pre, g_post, w1, w2):
    bsz, seq, d = x.shape
    tile = FFN_TILE
    assert seq % tile == 0
    tiles_per_seq = seq // tile

    def cur(j):
        return j // tiles_per_seq, j % tiles_per_seq

    return pl.pallas_call(
        _ffn_kernel,
        out_shape=jax.ShapeDtypeStruct(x.shape, F32),
        grid=(bsz * tiles_per_seq,),
        in_specs=[
            pl.BlockSpec((None, tile, d), lambda j: (*cur(j), 0)),
            pl.BlockSpec((None, N_MOD, d), lambda j: (cur(j)[0], 0, 0)),
            _const_spec((1, d)),
            _const_spec((1, d)),
            pl.BlockSpec(memory_space=pl.ANY),
            pl.BlockSpec(memory_space=pl.ANY),
        ],
        out_specs=pl.BlockSpec((None, tile, d), lambda j: (*cur(j), 0)),
        scratch_shapes=[pltpu.VMEM(w1.shape, w1.dtype), pltpu.VMEM(w2.shape, w2.dtype),
                        pltpu.SemaphoreType.DMA((2, FFN_WEIGHT_CHUNKS))],
        compiler_params=pltpu.CompilerParams(
            dimension_semantics=("arbitrary",),
            vmem_limit_bytes=VMEM_LIMIT_BYTES),
        name="ffn",
    )(x, mod.reshape(bsz, N_MOD, d), g_pre.reshape(1, d), g_post.reshape(1, d), w1, w2)


def kernel(x, c, positions, ada_w, ada_b, mix_pre_g, mix_post_g, ffn_pre_g, ffn_post_g, w_in,
           b_branch_gate, pool_w, pool_scale, w_branch_pool, w_branch_ret, w_out, w_ff1, w_ff2):
    depth = ada_w.shape[0]
    for l in range(depth):
        mod, w_in_b, w_br_b, w_out_b = _adaln(c, ada_w[l], ada_b[l], (w_in[l], w_branch_ret[l], w_out[l]))
        w_pool = _pool_fold(pool_w[l], pool_scale[l], w_branch_pool[l])
        x, w1, w2 = _mixer(x, mod, positions, mix_pre_g[l], mix_post_g[l], w_in_b, b_branch_gate[l],
                           w_pool, w_br_b, w_out_b, (w_ff1[l], w_ff2[l]))
        x = _ffn(x, mod, ffn_pre_g[l], ffn_post_g[l], w1, w2)
    return x
```

```python
import functools
from typing import Any, NamedTuple

import jax
import jax.numpy as jnp
import numpy as np
from jax import lax
from jax.experimental import pallas as pl
from jax.experimental.pallas import tpu as pltpu

F32 = jnp.float32
BF16 = jnp.bfloat16

CHUNK = 64
POOL_WINDOWS = (2, 4, 8, 16)
POOL_HALO = 16
POOL_ROWS = 64
RET_HEADS = 4
N_BRANCHES = 2
N_MOD = 6
ROPE_BASE = 10000.0
EPS = 1e-6

LANES = 128
MIXER_TILE = 256
RET_BLOCK = 256
FFN_TILE = 512
FFN_WEIGHT_CHUNKS = 4
ADALN_BLOCK_N = 768
VMEM_LIMIT_BYTES = 56 * 1024 * 1024


def _rms(xf):
    return xf * lax.rsqrt(jnp.mean(xf * xf, axis=-1, keepdims=True) + EPS)


class ModRows:
    def __init__(self, ref, b):
        self.ref, self.b, self.d = ref, b, ref.shape[1] // N_MOD

    def __getitem__(self, idx):
        k = idx[0].start
        return self.ref[pl.ds(self.b, 1), k * self.d:(k + 1) * self.d]


def _adaln_kernel(c_ref, w_ref, b_ref, *refs):
    n_cast = len(refs) // 2
    cast_in, o_ref, cast_out = refs[:n_cast], refs[n_cast], refs[n_cast + 1:]
    c = c_ref[...]
    a = (c * jax.nn.sigmoid(c)).astype(BF16)
    o_ref[...] = jnp.dot(a, w_ref[...].astype(BF16), preferred_element_type=F32) + b_ref[...]
    for src, dst in zip(cast_in, cast_out):
        dst[...] = src[...].astype(BF16)


def _adaln(c, w, b, to_cast):
    bsz, d = c.shape
    n = w.shape[1]
    n_steps = n // ADALN_BLOCK_N
    assert all(m.shape[0] % (n_steps * 16) == 0 for m in to_cast)
    cast_specs = [pl.BlockSpec((m.shape[0] // n_steps, m.shape[1]), lambda j: (j, 0)) for m in to_cast]
    return pl.pallas_call(
        _adaln_kernel,
        out_shape=[jax.ShapeDtypeStruct((bsz, n), F32)]
                  + [jax.ShapeDtypeStruct(m.shape, BF16) for m in to_cast],
        grid=(n_steps,),
        in_specs=[
            pl.BlockSpec((bsz, d), lambda j: (0, 0)),
            pl.BlockSpec((d, ADALN_BLOCK_N), lambda j: (0, j)),
            pl.BlockSpec((1, ADALN_BLOCK_N), lambda j: (0, j)),
        ] + cast_specs,
        out_specs=[pl.BlockSpec((bsz, ADALN_BLOCK_N), lambda j: (0, j))] + cast_specs,
        compiler_params=pltpu.CompilerParams(dimension_semantics=("arbitrary",)),
        name="adaln",
    )(c, w, b.reshape(1, n), *to_cast)


def _pool_fold_kernel(pw_ref, scale_ref, wbp_ref, o_ref):
    groups, gd, _ = pw_ref.shape
    for g in range(groups):
        rows = slice(g * gd, (g + 1) * gd)
        o_ref[rows, :] = jnp.dot((pw_ref[g] * scale_ref[:, rows]).astype(BF16), wbp_ref[rows, :].astype(BF16),
                                 preferred_element_type=F32).astype(o_ref.dtype)


def _pool_fold(pool_w, pool_scale, w_bp):
    groups, gd, _ = pool_w.shape
    whole = lambda shape: pl.BlockSpec(shape, lambda j: (0,) * len(shape))
    return pl.pallas_call(
        _pool_fold_kernel,
        out_shape=jax.ShapeDtypeStruct(w_bp.shape, BF16),
        grid=(1,),
        in_specs=[whole(pool_w.shape), whole((1, groups * gd)), whole(w_bp.shape)],
        out_specs=whole(w_bp.shape),
        compiler_params=pltpu.CompilerParams(dimension_semantics=("arbitrary",)),
        name="pool_fold",
    )(pool_w, pool_scale.reshape(1, groups * gd), w_bp)


class Staged(NamedTuple):
    probs: Any
    rot: Any
    pooled: Any
    sg: Any
    gates: Any


class StageParams(NamedTuple):
    g_pre: Any
    w_in: Any
    b_gate: Any
    invf: Any
    dmat: Any
    qdec: Any
    kdec: Any
    halo: Any


class MixParams(NamedTuple):
    g_post: Any
    w_pool: Any
    w_br: Any
    w_out: Any
    state: Any


def _stage_phases(x_ref, rows, mod_ref, pos, seq_tile, first, sp: StageParams, out: Staged):
    tile = rows.stop - rows.start
    pool_width = out.pooled.shape[1]
    ret_width = out.sg.shape[1]
    d_model = out.gates.shape[1] // N_BRANCHES
    hd = ret_width // RET_HEADS
    sh_m, sc_m = mod_ref[0:1, :], mod_ref[1:2, :]
    h = ((_rms(x_ref[rows, :]) * sp.g_pre[...]) * (1.0 + sc_m) + sh_m).astype(BF16)
    yield

    def proj(lo, width):
        return jnp.dot(h, sp.w_in[:, lo:lo + width], preferred_element_type=F32)

    u = proj(0, pool_width)
    yield

    if first is True:
        halo = jnp.zeros(sp.halo.shape, F32)
    elif first is False:
        halo = sp.halo[...]
    else:
        halo = jnp.where(first, 0.0, sp.halo[...])
    sp.halo[...] = u[tile - POOL_HALO:tile, :]
    t_pos = (lax.broadcasted_iota(jnp.int32, (tile, LANES), 0) + (seq_tile * tile + 1)).astype(F32)
    inv_t = 1.0 / t_pos
    gd = pool_width // len(POOL_WINDOWS)
    for gi, w in enumerate(POOL_WINDOWS):
        cols = slice(gi * gd, (gi + 1) * gd)
        inv_cnt = jnp.where(t_pos < float(w), inv_t, 1.0 / w)
        for r0 in range(0, tile, POOL_ROWS):
            blk = slice(r0, r0 + POOL_ROWS)
            before = halo[:, cols] if r0 == 0 else u[r0 - POOL_HALO:r0, cols]
            s = jnp.concatenate([before, u[blk, cols]], axis=0)
            k = 1
            while k < w:
                s = s + pltpu.roll(s, k, 0)
                k *= 2
            out.pooled[blk, cols] = (s[POOL_HALO:] * inv_cnt[blk] - u[blk, cols]).astype(BF16)
    q = proj(pool_width, ret_width)
    yield
    k_all = proj(pool_width + ret_width, ret_width)
    yield

    ang = sp.invf[...] * pos.astype(F32)
    cos_h, sin_h = jnp.cos(ang), jnp.sin(ang)
    cos_t = jnp.transpose(jnp.concatenate([cos_h, cos_h], axis=0))
    sin_t = jnp.transpose(jnp.concatenate([-sin_h, sin_h], axis=0))
    blk = sp.dmat.shape[1]
    roped = []
    for hh in range(RET_HEADS):
        cols = slice(hh * hd, (hh + 1) * hd)
        qh, kh = q[:, cols], k_all[:, cols]
        qh = (qh * cos_t + pltpu.roll(qh, hd // 2, 1) * sin_t) * (hd ** -0.5)
        kh = kh * cos_t + pltpu.roll(kh, hd // 2, 1) * sin_t
        roped.append((qh.astype(BF16), kh.astype(BF16)))
        for r0 in range(0, tile, blk):
            out.rot[1, r0:r0 + blk, cols] = (qh[r0:r0 + blk] * sp.qdec[hh]).astype(BF16)
            out.rot[2, r0:r0 + blk, cols] = (kh[r0:r0 + blk] * sp.kdec[hh]).astype(BF16)
    out.rot[0] = proj(pool_width + 2 * ret_width, ret_width).astype(BF16)
    yield
    g = proj(pool_width + 3 * ret_width, ret_width)
    out.sg[...] = g * jax.nn.sigmoid(g)
    yield
    for hh, (qh, kh) in enumerate(roped):
        for bi, r0 in enumerate(range(0, tile, blk)):
            scores = lax.dot_general(qh[r0:r0 + blk], kh[r0:r0 + blk], (((1,), (1,)), ((), ())),
                                     preferred_element_type=F32)
            out.probs[hh * (tile // blk) + bi] = (scores * sp.dmat[hh]).astype(BF16)
    yield
    gate_lo = pool_width + 4 * ret_width
    for br in range(N_BRANCHES):
        cols = slice(br * d_model, (br + 1) * d_model)
        out.gates[:, cols] = jax.nn.sigmoid(proj(gate_lo + br * d_model, d_model) + sp.b_gate[:, cols])
        yield


def _mix_phases(st: Staged, x_ref, rows, o_ref, mod_ref, mp: MixParams, tile_decay):
    d_model = x_ref.shape[1]
    ret_width = st.sg.shape[1]
    hd = ret_width // RET_HEADS
    gt_m = mod_ref[2:3, :]

    tile = st.sg.shape[0]
    blk = st.probs.shape[1]
    n_blk = tile // blk
    gated = [[None] * RET_HEADS for _ in range(n_blk)]
    for bi in range(n_blk):
        blk_rows = slice(bi * blk, (bi + 1) * blk)
        for hh in range(RET_HEADS):
            cols = slice(hh * hd, (hh + 1) * hd)
            vh = st.rot[0, blk_rows, cols]
            o = jnp.dot(st.probs[hh * n_blk + bi], vh, preferred_element_type=F32)
            o = o + jnp.dot(st.rot[1, blk_rows, cols], mp.state[hh].astype(BF16),
                            preferred_element_type=F32)
            kv = lax.dot_general(st.rot[2, blk_rows, cols], vh, (((0,), (0,)), ((), ())),
                                 preferred_element_type=F32)
            mp.state[hh] = mp.state[hh] * tile_decay[hh] + kv
            mu = jnp.mean(o, axis=-1, keepdims=True)
            oc = o - mu
            var = jnp.mean(oc * oc, axis=-1, keepdims=True)
            gated[bi][hh] = (st.sg[blk_rows, cols] * (oc * lax.rsqrt(var + EPS))).astype(BF16)
            yield
    y_pool = jnp.dot(st.pooled[...], mp.w_pool[...], preferred_element_type=F32)
    yield
    gated = jnp.concatenate([jnp.concatenate(row, axis=-1) for row in gated], axis=0)
    y_ret = jnp.dot(gated, mp.w_br[...], preferred_element_type=F32)
    yield
    merged = (st.gates[:, 0:d_model] * y_pool + st.gates[:, d_model:2 * d_model] * y_ret).astype(BF16)
    y = jnp.dot(merged, mp.w_out[...], preferred_element_type=F32)
    yield
    o_ref[rows, :] = x_ref[rows, :] + gt_m * (_rms(y) * mp.g_post[...])


WEAVE = "msmmmsmssmssmsssm"


def _weave(order, mix, stage):
    gens = {"m": mix, "s": stage}
    for c in order:
        next(gens[c], None)
    for g in (mix, stage):
        for _ in g:
            pass


def _mixer_kernel(x_ref, xn_ref, mod_all_ref, pos_ref, posn_ref, invf_ref, g_pre_ref, g_post_ref,
                  w_in_ref, b_gate_ref, w_pool_ref, w_br_ref, w_out_ref, dmat_ref, qdec_ref, kdec_ref,
                  wf1_ref, wf2_ref, o_ref, wf1_o_ref, wf2_o_ref, state_ref, halo_ref, *staged,
                  tile, steps_per_seq, n_steps, tile_decay):
    j = pl.program_id(0)
    wf1_o_ref[...] = wf1_ref[...].astype(BF16)
    wf2_o_ref[...] = wf2_ref[...].astype(BF16)
    b, s = j // steps_per_seq, j % steps_per_seq
    j_next = jnp.minimum(j + 1, n_steps - 1)
    b_next, s_next = j_next // steps_per_seq, j_next % steps_per_seq
    mod_ref, modn_ref = ModRows(mod_all_ref, b), ModRows(mod_all_ref, b_next)
    sp = StageParams(g_pre_ref, w_in_ref, b_gate_ref, invf_ref, dmat_ref, qdec_ref, kdec_ref, halo_ref)
    mp = MixParams(g_post_ref, w_pool_ref, w_br_ref, w_out_ref, state_ref)
    n_fields = len(Staged._fields)
    slots = (Staged(*staged[:n_fields]), Staged(*staged[n_fields:]))
    whole = slice(0, tile)

    @pl.when(j == 0)
    def _():
        for _ in _stage_phases(x_ref, whole, mod_ref, pos_ref[pl.ds(b, 1), :], 0, True, sp, slots[0]):
            pass

    @pl.when(s == 0)
    def _():
        state_ref[...] = jnp.zeros_like(state_ref)

    for parity in range(2):
        @pl.when(j % 2 == parity)
        def _():
            _weave(WEAVE,
                   _mix_phases(slots[parity], x_ref, whole, o_ref, mod_ref, mp, tile_decay),
                   _stage_phases(xn_ref, whole, modn_ref, posn_ref[pl.ds(b_next, 1), :], s_next,
                                 s_next == 0, sp, slots[1 - parity]))


def _decay_tables(tile, hd):
    gamma = 1.0 - 2.0 ** (-5.0 - np.arange(RET_HEADS, dtype=np.float64))
    idx = np.arange(tile)
    dist = np.abs(idx[:, None] - idx[None, :])
    visible = (idx[None, :] // CHUNK) <= (idx[:, None] // CHUNK)
    dmat = np.where(visible[None], gamma[:, None, None] ** dist[None], 0.0)
    qdec = np.broadcast_to((gamma[:, None] ** (idx[None, :] + 1))[:, :, None], (RET_HEADS, tile, hd))
    kdec = np.broadcast_to((gamma[:, None] ** (tile - 1 - idx[None, :]))[:, :, None], (RET_HEADS, tile, hd))
    tile_decay = tuple(float(g) ** tile for g in gamma)
    as_f32 = lambda a: jnp.asarray(np.ascontiguousarray(a), dtype=F32)
    return as_f32(dmat), as_f32(qdec), as_f32(kdec), tile_decay


def _const_spec(shape):
    nd = len(shape)
    return pl.BlockSpec(shape, lambda j: (0,) * nd)


def _mixer(x, mod, positions, g_pre, g_post, w_in, b_gate, w_pool, w_br, w_out, w_ffn):
    bsz, seq, d = x.shape
    tile = MIXER_TILE
    pool_width = w_pool.shape[0]
    ret_width = w_br.shape[0]
    hd = ret_width // RET_HEADS
    assert seq % tile == 0 and tile % RET_BLOCK == 0 and RET_BLOCK % CHUNK == 0 and hd == LANES
    steps_per_seq = seq // tile
    n_steps = bsz * steps_per_seq
    dmat, qdec, kdec, tile_decay = _decay_tables(RET_BLOCK, hd)
    half = hd // 2
    invf = (ROPE_BASE ** (-jnp.arange(half, dtype=F32) / half)).reshape(half, 1)

    def cur(j):
        return j // steps_per_seq, j % steps_per_seq

    def nxt(j):
        return cur(jnp.minimum(j + 1, n_steps - 1))

    const = _const_spec
    operands = [
        (x, pl.BlockSpec((None, tile, d), lambda j: (*cur(j), 0))),
        (x, pl.BlockSpec((None, tile, d), lambda j: (*nxt(j), 0))),
        (mod, const(mod.shape)),
        (positions, pl.BlockSpec((bsz, tile), lambda j: (0, cur(j)[1]))),
        (positions, pl.BlockSpec((bsz, tile), lambda j: (0, nxt(j)[1]))),
        (invf, const((half, 1))),
        (g_pre.reshape(1, d), const((1, d))),
        (g_post.reshape(1, d), const((1, d))),
        (w_in, const(w_in.shape)),
        (b_gate.reshape(1, N_BRANCHES * d), const((1, N_BRANCHES * d))),
        (w_pool, const(w_pool.shape)),
        (w_br, const(w_br.shape)),
        (w_out, const(w_out.shape)),
        (dmat, const(dmat.shape)),
        (qdec, const(qdec.shape)),
        (kdec, const(kdec.shape)),
    ] + [(w, pl.BlockSpec((w.shape[0] // n_steps, w.shape[1]), lambda j: (j, 0))) for w in w_ffn]
    assert all(w.shape[0] % (n_steps * 16) == 0 for w in w_ffn)
    staged = [
        pltpu.VMEM((RET_HEADS * (tile // RET_BLOCK), RET_BLOCK, RET_BLOCK), BF16),
        pltpu.VMEM((3, tile, ret_width), BF16),
        pltpu.VMEM((tile, pool_width), BF16),
        pltpu.VMEM((tile, ret_width), F32),
        pltpu.VMEM((tile, N_BRANCHES * d), F32),
    ]
    return pl.pallas_call(
        functools.partial(_mixer_kernel, tile=tile, steps_per_seq=steps_per_seq, n_steps=n_steps,
                          tile_decay=tile_decay),
        out_shape=[jax.ShapeDtypeStruct(x.shape, F32)]
                  + [jax.ShapeDtypeStruct(w.shape, BF16) for w in w_ffn],
        grid=(n_steps,),
        in_specs=[spec for _, spec in operands],
        out_specs=[pl.BlockSpec((None, tile, d), lambda j: (*cur(j), 0))]
                  + [pl.BlockSpec((w.shape[0] // n_steps, w.shape[1]), lambda j: (j, 0)) for w in w_ffn],
        scratch_shapes=[
            pltpu.VMEM((RET_HEADS, hd, hd), F32),
            pltpu.VMEM((POOL_HALO, pool_width), F32),
        ] + staged + staged,
        compiler_params=pltpu.CompilerParams(
            dimension_semantics=("arbitrary",),
            vmem_limit_bytes=VMEM_LIMIT_BYTES),
        name="mixer",
    )(*[a for a, _ in operands])


def _ffn_kernel(x_ref, mod_ref, g_pre_ref, g_post_ref, w1_hbm, w2_hbm, o_ref, w1_ref, w2_ref, sem,
                *, tiles_per_seq):
    j = pl.program_id(0)
    d_ff = w1_ref.shape[1]
    width = d_ff // FFN_WEIGHT_CHUNKS
    chunks = [slice(c * width, (c + 1) * width) for c in range(FFN_WEIGHT_CHUNKS)]

    def weight_copies():
        return [(pltpu.make_async_copy(w1_hbm.at[:, ck], w1_ref.at[:, ck], sem.at[0, c]),
                 pltpu.make_async_copy(w2_hbm.at[ck, :], w2_ref.at[ck, :], sem.at[1, c]))
                for c, ck in enumerate(chunks)]

    def up(h, cols):
        a = jnp.maximum(jnp.dot(h, w1_ref[:, cols], preferred_element_type=F32), 0.0)
        return (a * a).astype(BF16)

    def step(first):
        copies = weight_copies()
        if first:
            for cp1, cp2 in copies:
                cp1.start()
                cp2.start()
        x = x_ref[...]
        mod = ModRows(mod_ref, j // tiles_per_seq)
        sh_f, sc_f, gt_f = mod[3:4, :], mod[4:5, :], mod[5:6, :]
        h = ((_rms(x) * g_pre_ref[...]) * (1.0 + sc_f) + sh_f).astype(BF16)
        if first:
            y = None
            for (cp1, cp2), ck in zip(copies, chunks):
                cp1.wait()
                a = up(h, ck)
                cp2.wait()
                part = jnp.dot(a, w2_ref[ck, :], preferred_element_type=F32)
                y = part if y is None else y + part
        else:
            y = jnp.dot(up(h, slice(0, d_ff)), w2_ref[...], preferred_element_type=F32)
        o_ref[...] = x + gt_f * (_rms(y) * g_post_ref[...])

    pl.when(j == 0)(functools.partial(step, True))
    pl.when(j > 0)(functools.partial(step, False))


def _ffn(x, mod, g_pre, g_post, w1, w2):
    bsz, seq, d = x.shape
    tile = FFN_TILE
    assert seq % tile == 0
    tiles_per_seq = seq // tile

    def cur(j):
        return j // tiles_per_seq, j % tiles_per_seq

    return pl.pallas_call(
        functools.partial(_ffn_kernel, tiles_per_seq=tiles_per_seq),
        out_shape=jax.ShapeDtypeStruct(x.shape, F32),
        grid=(bsz * tiles_per_seq,),
        in_specs=[
            pl.BlockSpec((None, tile, d), lambda j: (*cur(j), 0)),
            _const_spec(mod.shape),
            _const_spec((1, d)),
            _const_spec((1, d)),
            pl.BlockSpec(memory_space=pl.ANY),
            pl.BlockSpec(memory_space=pl.ANY),
        ],
        out_specs=pl.BlockSpec((None, tile, d), lambda j: (*cur(j), 0)),
        scratch_shapes=[pltpu.VMEM(w1.shape, w1.dtype), pltpu.VMEM(w2.shape, w2.dtype),
                        pltpu.SemaphoreType.DMA((2, FFN_WEIGHT_CHUNKS))],
        compiler_params=pltpu.CompilerParams(
            dimension_semantics=("arbitrary",),
            vmem_limit_bytes=VMEM_LIMIT_BYTES),
        name="ffn",
    )(x, mod, g_pre.reshape(1, d), g_post.reshape(1, d), w1, w2)


def kernel(x, c, positions, ada_w, ada_b, mix_pre_g, mix_post_g, ffn_pre_g, ffn_post_g, w_in,
           b_branch_gate, pool_w, pool_scale, w_branch_pool, w_branch_ret, w_out, w_ff1, w_ff2):
    depth = ada_w.shape[0]
    for l in range(depth):
        mod, w_in_b, w_br_b, w_out_b = _adaln(c, ada_w[l], ada_b[l], (w_in[l], w_branch_ret[l], w_out[l]))
        w_pool = _pool_fold(pool_w[l], pool_scale[l], w_branch_pool[l])
        x, w1, w2 = _mixer(x, mod, positions, mix_pre_g[l], mix_post_g[l], w_in_b, b_branch_gate[l],
                           w_pool, w_br_b, w_out_b, (w_ff1[l], w_ff2[l]))
        x = _ffn(x, mod, ffn_pre_g[l], ffn_post_g[l], w1, w2)
    return x
```

```python
import functools
from typing import Any, NamedTuple

import jax
import jax.numpy as jnp
import numpy as np
from jax import lax
from jax.experimental import pallas as pl
from jax.experimental.pallas import tpu as pltpu

F32 = jnp.float32
BF16 = jnp.bfloat16

CHUNK = 64
POOL_WINDOWS = (2, 4, 8, 16)
POOL_HALO = 16
POOL_ROWS = 64
RET_HEADS = 4
N_BRANCHES = 2
N_MOD = 6
ROPE_BASE = 10000.0
EPS = 1e-6

LANES = 128
MIXER_TILE = 256
RET_BLOCK = 256
FFN_TILE = 512
FFN_WEIGHT_CHUNKS = 4
ADALN_BLOCK_N = 768
VMEM_LIMIT_BYTES = 56 * 1024 * 1024


def _rms(xf):
    return xf * lax.rsqrt(jnp.mean(xf * xf, axis=-1, keepdims=True) + EPS)


class ModRows:
    def __init__(self, ref, b):
        self.ref, self.b, self.d = ref, b, ref.shape[1] // N_MOD

    def __getitem__(self, idx):
        k = idx[0].start
        return self.ref[pl.ds(self.b, 1), k * self.d:(k + 1) * self.d]


def _prep_kernel(c_ref, w_ref, b_ref, pw_ref, scale_ref, wbp_ref, *refs):
    n_cast = (len(refs) - 2) // 2
    cast_in, (mod_ref, w_pool_ref), cast_out = refs[:n_cast], refs[n_cast:n_cast + 2], refs[n_cast + 2:]
    c = c_ref[...]
    a = (c * jax.nn.sigmoid(c)).astype(BF16)
    mod_ref[...] = jnp.dot(a, w_ref[...].astype(BF16), preferred_element_type=F32) + b_ref[...]
    w_pool_ref[...] = jnp.dot((pw_ref[...] * scale_ref[...]).astype(BF16), wbp_ref[...].astype(BF16),
                              preferred_element_type=F32).astype(BF16)
    for src, dst in zip(cast_in, cast_out):
        dst[...] = src[...].astype(BF16)


def _prep(c, w, b, pool_w, pool_scale, w_bp, to_cast):
    bsz, d = c.shape
    n = w.shape[1]
    n_steps = n // ADALN_BLOCK_N
    groups, gd, _ = pool_w.shape
    spg = n_steps // groups
    rows = gd // spg
    assert n_steps % groups == 0 and gd % spg == 0 and rows % 16 == 0
    assert all(m.shape[0] % (n_steps * 16) == 0 for m in to_cast)
    cast_specs = [pl.BlockSpec((m.shape[0] // n_steps, m.shape[1]), lambda j: (j, 0)) for m in to_cast]
    return pl.pallas_call(
        _prep_kernel,
        out_shape=[jax.ShapeDtypeStruct((bsz, n), F32), jax.ShapeDtypeStruct(w_bp.shape, BF16)]
                  + [jax.ShapeDtypeStruct(m.shape, BF16) for m in to_cast],
        grid=(n_steps,),
        in_specs=[
            pl.BlockSpec((bsz, d), lambda j: (0, 0)),
            pl.BlockSpec((d, ADALN_BLOCK_N), lambda j: (0, j)),
            pl.BlockSpec((1, ADALN_BLOCK_N), lambda j: (0, j)),
            pl.BlockSpec((None, rows, gd), lambda j: (j // spg, j % spg, 0)),
            pl.BlockSpec((1, gd), lambda j: (0, j // spg)),
            pl.BlockSpec((gd, w_bp.shape[1]), lambda j: (j // spg, 0)),
        ] + cast_specs,
        out_specs=[pl.BlockSpec((bsz, ADALN_BLOCK_N), lambda j: (0, j)),
                   pl.BlockSpec((rows, w_bp.shape[1]), lambda j: (j, 0))] + cast_specs,
        compiler_params=pltpu.CompilerParams(dimension_semantics=("arbitrary",)),
        name="prep",
    )(c, w, b.reshape(1, n), pool_w, pool_scale.reshape(1, groups * gd), w_bp, *to_cast)


class Staged(NamedTuple):
    probs: Any
    rot: Any
    pooled: Any
    sg: Any
    gates: Any


class StageParams(NamedTuple):
    g_pre: Any
    w_in: Any
    b_gate: Any
    invf: Any
    dmat: Any
    qdec: Any
    kdec: Any
    halo: Any


class MixParams(NamedTuple):
    g_post: Any
    w_pool: Any
    w_br: Any
    w_out: Any
    state: Any


def _stage_phases(x_ref, rows, mod_ref, pos, seq_tile, first, sp: StageParams, out: Staged):
    tile = rows.stop - rows.start
    pool_width = out.pooled.shape[1]
    ret_width = out.sg.shape[1]
    d_model = out.gates.shape[1] // N_BRANCHES
    hd = ret_width // RET_HEADS
    sh_m, sc_m = mod_ref[0:1, :], mod_ref[1:2, :]
    h = ((_rms(x_ref[rows, :]) * sp.g_pre[...]) * (1.0 + sc_m) + sh_m).astype(BF16)
    yield

    def proj(lo, width):
        return jnp.dot(h, sp.w_in[:, lo:lo + width], preferred_element_type=F32)

    u = proj(0, pool_width)
    yield

    if first is True:
        halo = jnp.zeros(sp.halo.shape, F32)
    elif first is False:
        halo = sp.halo[...]
    else:
        halo = jnp.where(first, 0.0, sp.halo[...])
    sp.halo[...] = u[tile - POOL_HALO:tile, :]
    t_pos = (lax.broadcasted_iota(jnp.int32, (tile, LANES), 0) + (seq_tile * tile + 1)).astype(F32)
    inv_t = 1.0 / t_pos
    gd = pool_width // len(POOL_WINDOWS)
    for gi, w in enumerate(POOL_WINDOWS):
        cols = slice(gi * gd, (gi + 1) * gd)
        inv_cnt = jnp.where(t_pos < float(w), inv_t, 1.0 / w)
        for r0 in range(0, tile, POOL_ROWS):
            blk = slice(r0, r0 + POOL_ROWS)
            before = halo[:, cols] if r0 == 0 else u[r0 - POOL_HALO:r0, cols]
            s = jnp.concatenate([before, u[blk, cols]], axis=0)
            k = 1
            while k < w:
                s = s + pltpu.roll(s, k, 0)
                k *= 2
            out.pooled[blk, cols] = (s[POOL_HALO:] * inv_cnt[blk] - u[blk, cols]).astype(BF16)
    q = proj(pool_width, ret_width)
    yield
    k_all = proj(pool_width + ret_width, ret_width)
    yield

    ang = sp.invf[...] * pos.astype(F32)
    cos_h, sin_h = jnp.cos(ang), jnp.sin(ang)
    cos_t = jnp.transpose(jnp.concatenate([cos_h, cos_h], axis=0))
    sin_t = jnp.transpose(jnp.concatenate([-sin_h, sin_h], axis=0))
    blk = sp.dmat.shape[1]
    roped = []
    for hh in range(RET_HEADS):
        cols = slice(hh * hd, (hh + 1) * hd)
        qh, kh = q[:, cols], k_all[:, cols]
        qh = (qh * cos_t + pltpu.roll(qh, hd // 2, 1) * sin_t) * (hd ** -0.5)
        kh = kh * cos_t + pltpu.roll(kh, hd // 2, 1) * sin_t
        roped.append((qh.astype(BF16), kh.astype(BF16)))
        for r0 in range(0, tile, blk):
            out.rot[1, r0:r0 + blk, cols] = (qh[r0:r0 + blk] * sp.qdec[hh]).astype(BF16)
            out.rot[2, r0:r0 + blk, cols] = (kh[r0:r0 + blk] * sp.kdec[hh]).astype(BF16)
    out.rot[0] = proj(pool_width + 2 * ret_width, ret_width).astype(BF16)
    yield
    g = proj(pool_width + 3 * ret_width, ret_width)
    out.sg[...] = g * jax.nn.sigmoid(g)
    yield
    for hh, (qh, kh) in enumerate(roped):
        for bi, r0 in enumerate(range(0, tile, blk)):
            scores = lax.dot_general(qh[r0:r0 + blk], kh[r0:r0 + blk], (((1,), (1,)), ((), ())),
                                     preferred_element_type=F32)
            out.probs[hh * (tile // blk) + bi] = (scores * sp.dmat[hh]).astype(BF16)
    yield
    gate_lo = pool_width + 4 * ret_width
    for br in range(N_BRANCHES):
        cols = slice(br * d_model, (br + 1) * d_model)
        out.gates[:, cols] = jax.nn.sigmoid(proj(gate_lo + br * d_model, d_model) + sp.b_gate[:, cols])
        yield


def _mix_phases(st: Staged, x_ref, rows, o_ref, mod_ref, mp: MixParams, tile_decay):
    d_model = x_ref.shape[1]
    ret_width = st.sg.shape[1]
    hd = ret_width // RET_HEADS
    gt_m = mod_ref[2:3, :]

    tile = st.sg.shape[0]
    blk = st.probs.shape[1]
    n_blk = tile // blk
    gated = [[None] * RET_HEADS for _ in range(n_blk)]
    for bi in range(n_blk):
        blk_rows = slice(bi * blk, (bi + 1) * blk)
        for hh in range(RET_HEADS):
            cols = slice(hh * hd, (hh + 1) * hd)
            vh = st.rot[0, blk_rows, cols]
            o = jnp.dot(st.probs[hh * n_blk + bi], vh, preferred_element_type=F32)
            o = o + jnp.dot(st.rot[1, blk_rows, cols], mp.state[hh].astype(BF16),
                            preferred_element_type=F32)
            kv = lax.dot_general(st.rot[2, blk_rows, cols], vh, (((0,), (0,)), ((), ())),
                                 preferred_element_type=F32)
            mp.state[hh] = mp.state[hh] * tile_decay[hh] + kv
            mu = jnp.mean(o, axis=-1, keepdims=True)
            oc = o - mu
            var = jnp.mean(oc * oc, axis=-1, keepdims=True)
            gated[bi][hh] = (st.sg[blk_rows, cols] * (oc * lax.rsqrt(var + EPS))).astype(BF16)
            yield
    y_pool = jnp.dot(st.pooled[...], mp.w_pool[...], preferred_element_type=F32)
    yield
    gated = jnp.concatenate([jnp.concatenate(row, axis=-1) for row in gated], axis=0)
    y_ret = jnp.dot(gated, mp.w_br[...], preferred_element_type=F32)
    yield
    merged = (st.gates[:, 0:d_model] * y_pool + st.gates[:, d_model:2 * d_model] * y_ret).astype(BF16)
    y = jnp.dot(merged, mp.w_out[...], preferred_element_type=F32)
    yield
    o_ref[rows, :] = x_ref[rows, :] + gt_m * (_rms(y) * mp.g_post[...])


WEAVE = "msmmmsmssmssmsssm"


def _weave(order, mix, stage):
    gens = {"m": mix, "s": stage}
    for c in order:
        next(gens[c], None)
    for g in (mix, stage):
        for _ in g:
            pass


def _mixer_kernel(x_ref, xn_ref, mod_all_ref, pos_ref, posn_ref, invf_ref, g_pre_ref, g_post_ref,
                  w_in_ref, b_gate_ref, w_pool_ref, w_br_ref, w_out_ref, dmat_ref, qdec_ref, kdec_ref,
                  wf1_ref, wf2_ref, o_ref, wf1_o_ref, wf2_o_ref, state_ref, halo_ref, *staged,
                  tile, steps_per_seq, n_steps, tile_decay):
    j = pl.program_id(0)
    wf1_o_ref[...] = wf1_ref[...].astype(BF16)
    wf2_o_ref[...] = wf2_ref[...].astype(BF16)
    b, s = j // steps_per_seq, j % steps_per_seq
    j_next = jnp.minimum(j + 1, n_steps - 1)
    b_next, s_next = j_next // steps_per_seq, j_next % steps_per_seq
    mod_ref, modn_ref = ModRows(mod_all_ref, b), ModRows(mod_all_ref, b_next)
    sp = StageParams(g_pre_ref, w_in_ref, b_gate_ref, invf_ref, dmat_ref, qdec_ref, kdec_ref, halo_ref)
    mp = MixParams(g_post_ref, w_pool_ref, w_br_ref, w_out_ref, state_ref)
    n_fields = len(Staged._fields)
    slots = (Staged(*staged[:n_fields]), Staged(*staged[n_fields:]))
    whole = slice(0, tile)

    @pl.when(j == 0)
    def _():
        for _ in _stage_phases(x_ref, whole, mod_ref, pos_ref[pl.ds(b, 1), :], 0, True, sp, slots[0]):
            pass

    @pl.when(s == 0)
    def _():
        state_ref[...] = jnp.zeros_like(state_ref)

    last = n_steps - 1
    for parity in range(2):
        @pl.when((j % 2 == parity) & (j < last))
        def _():
            _weave(WEAVE,
                   _mix_phases(slots[parity], x_ref, whole, o_ref, mod_ref, mp, tile_decay),
                   _stage_phases(xn_ref, whole, modn_ref, posn_ref[pl.ds(b_next, 1), :], s_next,
                                 s_next == 0, sp, slots[1 - parity]))

    @pl.when(j == last)
    def _():
        for _ in _mix_phases(slots[last % 2], x_ref, whole, o_ref, mod_ref, mp, tile_decay):
            pass


def _decay_tables(tile, hd):
    gamma = 1.0 - 2.0 ** (-5.0 - np.arange(RET_HEADS, dtype=np.float64))
    idx = np.arange(tile)
    dist = np.abs(idx[:, None] - idx[None, :])
    visible = (idx[None, :] // CHUNK) <= (idx[:, None] // CHUNK)
    dmat = np.where(visible[None], gamma[:, None, None] ** dist[None], 0.0)
    qdec = np.broadcast_to((gamma[:, None] ** (idx[None, :] + 1))[:, :, None], (RET_HEADS, tile, hd))
    kdec = np.broadcast_to((gamma[:, None] ** (tile - 1 - idx[None, :]))[:, :, None], (RET_HEADS, tile, hd))
    tile_decay = tuple(float(g) ** tile for g in gamma)
    as_f32 = lambda a: jnp.asarray(np.ascontiguousarray(a), dtype=F32)
    return as_f32(dmat), as_f32(qdec), as_f32(kdec), tile_decay


def _const_spec(shape):
    nd = len(shape)
    return pl.BlockSpec(shape, lambda j: (0,) * nd)


def _mixer(x, mod, positions, g_pre, g_post, w_in, b_gate, w_pool, w_br, w_out, w_ffn):
    bsz, seq, d = x.shape
    tile = MIXER_TILE
    pool_width = w_pool.shape[0]
    ret_width = w_br.shape[0]
    hd = ret_width // RET_HEADS
    assert seq % tile == 0 and tile % RET_BLOCK == 0 and RET_BLOCK % CHUNK == 0 and hd == LANES
    steps_per_seq = seq // tile
    n_steps = bsz * steps_per_seq
    dmat, qdec, kdec, tile_decay = _decay_tables(RET_BLOCK, hd)
    half = hd // 2
    invf = (ROPE_BASE ** (-jnp.arange(half, dtype=F32) / half)).reshape(half, 1)

    def cur(j):
        return j // steps_per_seq, j % steps_per_seq

    def nxt(j):
        return cur(jnp.minimum(j + 1, n_steps - 1))

    const = _const_spec
    operands = [
        (x, pl.BlockSpec((None, tile, d), lambda j: (*cur(j), 0))),
        (x, pl.BlockSpec((None, tile, d), lambda j: (*nxt(j), 0))),
        (mod, const(mod.shape)),
        (positions, pl.BlockSpec((bsz, tile), lambda j: (0, cur(j)[1]))),
        (positions, pl.BlockSpec((bsz, tile), lambda j: (0, nxt(j)[1]))),
        (invf, const((half, 1))),
        (g_pre.reshape(1, d), const((1, d))),
        (g_post.reshape(1, d), const((1, d))),
        (w_in, const(w_in.shape)),
        (b_gate.reshape(1, N_BRANCHES * d), const((1, N_BRANCHES * d))),
        (w_pool, const(w_pool.shape)),
        (w_br, const(w_br.shape)),
        (w_out, const(w_out.shape)),
        (dmat, const(dmat.shape)),
        (qdec, const(qdec.shape)),
        (kdec, const(kdec.shape)),
    ] + [(w, pl.BlockSpec((w.shape[0] // n_steps, w.shape[1]), lambda j: (j, 0))) for w in w_ffn]
    assert all(w.shape[0] % (n_steps * 16) == 0 for w in w_ffn)
    staged = [
        pltpu.VMEM((RET_HEADS * (tile // RET_BLOCK), RET_BLOCK, RET_BLOCK), BF16),
        pltpu.VMEM((3, tile, ret_width), BF16),
        pltpu.VMEM((tile, pool_width), BF16),
        pltpu.VMEM((tile, ret_width), F32),
        pltpu.VMEM((tile, N_BRANCHES * d), F32),
    ]
    return pl.pallas_call(
        functools.partial(_mixer_kernel, tile=tile, steps_per_seq=steps_per_seq, n_steps=n_steps,
                          tile_decay=tile_decay),
        out_shape=[jax.ShapeDtypeStruct(x.shape, F32)]
                  + [jax.ShapeDtypeStruct(w.shape, BF16) for w in w_ffn],
        grid=(n_steps,),
        in_specs=[spec for _, spec in operands],
        out_specs=[pl.BlockSpec((None, tile, d), lambda j: (*cur(j), 0))]
                  + [pl.BlockSpec((w.shape[0] // n_steps, w.shape[1]), lambda j: (j, 0)) for w in w_ffn],
        scratch_shapes=[
            pltpu.VMEM((RET_HEADS, hd, hd), F32),
            pltpu.VMEM((POOL_HALO, pool_width), F32),
        ] + staged + staged,
        compiler_params=pltpu.CompilerParams(
            dimension_semantics=("arbitrary",),
            vmem_limit_bytes=VMEM_LIMIT_BYTES),
        name="mixer",
    )(*[a for a, _ in operands])


def _ffn_kernel(x_ref, mod_ref, g_pre_ref, g_post_ref, w1_hbm, w2_hbm, o_ref, w1_ref, w2_ref, sem,
                *, tiles_per_seq):
    j = pl.program_id(0)
    d_ff = w1_ref.shape[1]
    width = d_ff // FFN_WEIGHT_CHUNKS
    chunks = [slice(c * width, (c + 1) * width) for c in range(FFN_WEIGHT_CHUNKS)]

    def weight_copies():
        return [(pltpu.make_async_copy(w1_hbm.at[:, ck], w1_ref.at[:, ck], sem.at[0, c]),
                 pltpu.make_async_copy(w2_hbm.at[ck, :], w2_ref.at[ck, :], sem.at[1, c]))
                for c, ck in enumerate(chunks)]

    def up(h, cols):
        a = jnp.maximum(jnp.dot(h, w1_ref[:, cols], preferred_element_type=F32), 0.0)
        return (a * a).astype(BF16)

    def step(first):
        copies = weight_copies()
        if first:
            for cp1, cp2 in copies:
                cp1.start()
                cp2.start()
        x = x_ref[...]
        mod = ModRows(mod_ref, j // tiles_per_seq)
        sh_f, sc_f, gt_f = mod[3:4, :], mod[4:5, :], mod[5:6, :]
        h = ((_rms(x) * g_pre_ref[...]) * (1.0 + sc_f) + sh_f).astype(BF16)
        if first:
            y = None
            for (cp1, cp2), ck in zip(copies, chunks):
                cp1.wait()
                a = up(h, ck)
                cp2.wait()
                part = jnp.dot(a, w2_ref[ck, :], preferred_element_type=F32)
                y = part if y is None else y + part
        else:
            y = jnp.dot(up(h, slice(0, d_ff)), w2_ref[...], preferred_element_type=F32)
        o_ref[...] = x + gt_f * (_rms(y) * g_post_ref[...])

    pl.when(j == 0)(functools.partial(step, True))
    pl.when(j > 0)(functools.partial(step, False))


def _ffn(x, mod, g_pre, g_post, w1, w2):
    bsz, seq, d = x.shape
    tile = FFN_TILE
    assert seq % tile == 0
    tiles_per_seq = seq // tile

    def cur(j):
        return j // tiles_per_seq, j % tiles_per_seq

    return pl.pallas_call(
        functools.partial(_ffn_kernel, tiles_per_seq=tiles_per_seq),
        out_shape=jax.ShapeDtypeStruct(x.shape, F32),
        grid=(bsz * tiles_per_seq,),
        in_specs=[
            pl.BlockSpec((None, tile, d), lambda j: (*cur(j), 0)),
            _const_spec(mod.shape),
            _const_spec((1, d)),
            _const_spec((1, d)),
            pl.BlockSpec(memory_space=pl.ANY),
            pl.BlockSpec(memory_space=pl.ANY),
        ],
        out_specs=pl.BlockSpec((None, tile, d), lambda j: (*cur(j), 0)),
        scratch_shapes=[pltpu.VMEM(w1.shape, w1.dtype), pltpu.VMEM(w2.shape, w2.dtype),
                        pltpu.SemaphoreType.DMA((2, FFN_WEIGHT_CHUNKS))],
        compiler_params=pltpu.CompilerParams(
            dimension_semantics=("arbitrary",),
            vmem_limit_bytes=VMEM_LIMIT_BYTES),
        name="ffn",
    )(x, mod, g_pre.reshape(1, d), g_post.reshape(1, d), w1, w2)


def kernel(x, c, positions, ada_w, ada_b, mix_pre_g, mix_post_g, ffn_pre_g, ffn_post_g, w_in,
           b_branch_gate, pool_w, pool_scale, w_branch_pool, w_branch_ret, w_out, w_ff1, w_ff2):
    depth = ada_w.shape[0]
    for l in range(depth):
        mod, w_pool, w_in_b, w_br_b, w_out_b = _prep(c, ada_w[l], ada_b[l], pool_w[l], pool_scale[l],
                                                     w_branch_pool[l], (w_in[l], w_branch_ret[l], w_out[l]))
        x, w1, w2 = _mixer(x, mod, positions, mix_pre_g[l], mix_post_g[l], w_in_b, b_branch_gate[l],
                           w_pool, w_br_b, w_out_b, (w_ff1[l], w_ff2[l]))
        x = _ffn(x, mod, ffn_pre_g[l], ffn_post_g[l], w1, w2)
    return x
```

```python
import functools
from typing import Any, NamedTuple

import jax
import jax.numpy as jnp
import numpy as np
from jax import lax
from jax.experimental import pallas as pl
from jax.experimental.pallas import tpu as pltpu

F32 = jnp.float32
BF16 = jnp.bfloat16

CHUNK = 64
POOL_WINDOWS = (2, 4, 8, 16)
POOL_HALO = 16
POOL_ROWS = 64
RET_HEADS = 4
N_BRANCHES = 2
N_MOD = 6
ROPE_BASE = 10000.0
EPS = 1e-6

LANES = 128
MIXER_TILE = 256
RET_BLOCK = 256
FFN_TILE = 512
FFN_WEIGHT_CHUNKS = 4
ADALN_BLOCK_N = 768
VMEM_LIMIT_BYTES = 56 * 1024 * 1024


def _rms(xf):
    return xf * lax.rsqrt(jnp.mean(xf * xf, axis=-1, keepdims=True) + EPS)


class ModRows:
    def __init__(self, ref, b):
        self.ref, self.b, self.d = ref, b, ref.shape[1] // N_MOD

    def __getitem__(self, idx):
        k = idx[0].start
        return self.ref[pl.ds(self.b, 1), k * self.d:(k + 1) * self.d]


def _prep_kernel(c_ref, w_ref, b_ref, pw_ref, scale_ref, wbp_ref, *refs):
    n_cast = (len(refs) - 2) // 2
    cast_in, (mod_ref, w_pool_ref), cast_out = refs[:n_cast], refs[n_cast:n_cast + 2], refs[n_cast + 2:]
    c = c_ref[...]
    a = (c * jax.nn.sigmoid(c)).astype(BF16)
    mod_ref[...] = jnp.dot(a, w_ref[...].astype(BF16), preferred_element_type=F32) + b_ref[...]
    w_pool_ref[...] = jnp.dot((pw_ref[...] * scale_ref[...]).astype(BF16), wbp_ref[...].astype(BF16),
                              preferred_element_type=F32).astype(BF16)
    for src, dst in zip(cast_in, cast_out):
        dst[...] = src[...].astype(BF16)


def _prep(c, w, b, pool_w, pool_scale, w_bp, to_cast):
    bsz, d = c.shape
    n = w.shape[1]
    n_steps = n // ADALN_BLOCK_N
    groups, gd, _ = pool_w.shape
    spg = n_steps // groups
    rows = gd // spg
    assert n_steps % groups == 0 and gd % spg == 0 and rows % 16 == 0
    assert all(m.shape[0] % (n_steps * 16) == 0 for m in to_cast)
    cast_specs = [pl.BlockSpec((m.shape[0] // n_steps, m.shape[1]), lambda j: (j, 0)) for m in to_cast]
    return pl.pallas_call(
        _prep_kernel,
        out_shape=[jax.ShapeDtypeStruct((bsz, n), F32), jax.ShapeDtypeStruct(w_bp.shape, BF16)]
                  + [jax.ShapeDtypeStruct(m.shape, BF16) for m in to_cast],
        grid=(n_steps,),
        in_specs=[
            pl.BlockSpec((bsz, d), lambda j: (0, 0)),
            pl.BlockSpec((d, ADALN_BLOCK_N), lambda j: (0, j)),
            pl.BlockSpec((1, ADALN_BLOCK_N), lambda j: (0, j)),
            pl.BlockSpec((None, rows, gd), lambda j: (j // spg, j % spg, 0)),
            pl.BlockSpec((1, gd), lambda j: (0, j // spg)),
            pl.BlockSpec((gd, w_bp.shape[1]), lambda j: (j // spg, 0)),
        ] + cast_specs,
        out_specs=[pl.BlockSpec((bsz, ADALN_BLOCK_N), lambda j: (0, j)),
                   pl.BlockSpec((rows, w_bp.shape[1]), lambda j: (j, 0))] + cast_specs,
        compiler_params=pltpu.CompilerParams(dimension_semantics=("arbitrary",)),
        name="prep",
    )(c, w, b.reshape(1, n), pool_w, pool_scale.reshape(1, groups * gd), w_bp, *to_cast)


class Staged(NamedTuple):
    probs: Any
    rot: Any
    pooled: Any
    sg: Any
    gates: Any


class StageParams(NamedTuple):
    g_pre: Any
    w_in: Any
    b_gate: Any
    invf: Any
    dmat: Any
    qdec: Any
    kdec: Any
    halo: Any


class MixParams(NamedTuple):
    g_post: Any
    w_pool: Any
    w_br: Any
    w_out: Any
    state: Any


def _stage_phases(x_ref, rows, mod_ref, pos, seq_tile, first, sp: StageParams, out: Staged):
    tile = rows.stop - rows.start
    pool_width = out.pooled.shape[1]
    ret_width = out.sg.shape[1]
    d_model = out.gates.shape[1] // N_BRANCHES
    hd = ret_width // RET_HEADS
    sh_m, sc_m = mod_ref[0:1, :], mod_ref[1:2, :]
    h = ((_rms(x_ref[rows, :]) * sp.g_pre[...]) * (1.0 + sc_m) + sh_m).astype(BF16)
    yield

    def proj(lo, width):
        return jnp.dot(h, sp.w_in[:, lo:lo + width], preferred_element_type=F32)

    u = proj(0, pool_width)
    yield

    if first is True:
        halo = jnp.zeros(sp.halo.shape, F32)
    elif first is False:
        halo = sp.halo[...]
    else:
        halo = jnp.where(first, 0.0, sp.halo[...])
    sp.halo[...] = u[tile - POOL_HALO:tile, :]
    t_pos = (lax.broadcasted_iota(jnp.int32, (tile, LANES), 0) + (seq_tile * tile + 1)).astype(F32)
    inv_t = 1.0 / t_pos
    gd = pool_width // len(POOL_WINDOWS)
    for gi, w in enumerate(POOL_WINDOWS):
        cols = slice(gi * gd, (gi + 1) * gd)
        inv_cnt = jnp.where(t_pos < float(w), inv_t, 1.0 / w)
        for r0 in range(0, tile, POOL_ROWS):
            blk = slice(r0, r0 + POOL_ROWS)
            before = halo[:, cols] if r0 == 0 else u[r0 - POOL_HALO:r0, cols]
            s = jnp.concatenate([before, u[blk, cols]], axis=0)
            k = 1
            while k < w:
                s = s + pltpu.roll(s, k, 0)
                k *= 2
            out.pooled[blk, cols] = (s[POOL_HALO:] * inv_cnt[blk] - u[blk, cols]).astype(BF16)
    q = proj(pool_width, ret_width)
    yield
    k_all = proj(pool_width + ret_width, ret_width)
    yield

    ang = sp.invf[...] * pos.astype(F32)
    cos_h, sin_h = jnp.cos(ang), jnp.sin(ang)
    cos_t = jnp.transpose(jnp.concatenate([cos_h, cos_h], axis=0))
    sin_t = jnp.transpose(jnp.concatenate([-sin_h, sin_h], axis=0))
    blk = sp.dmat.shape[1]
    roped = []
    for hh in range(RET_HEADS):
        cols = slice(hh * hd, (hh + 1) * hd)
        qh, kh = q[:, cols], k_all[:, cols]
        qh = (qh * cos_t + pltpu.roll(qh, hd // 2, 1) * sin_t) * (hd ** -0.5)
        kh = kh * cos_t + pltpu.roll(kh, hd // 2, 1) * sin_t
        roped.append((qh.astype(BF16), kh.astype(BF16)))
        for r0 in range(0, tile, blk):
            out.rot[1, r0:r0 + blk, cols] = (qh[r0:r0 + blk] * sp.qdec[hh]).astype(BF16)
            out.rot[2, r0:r0 + blk, cols] = (kh[r0:r0 + blk] * sp.kdec[hh]).astype(BF16)
    out.rot[0] = proj(pool_width + 2 * ret_width, ret_width).astype(BF16)
    yield
    g = proj(pool_width + 3 * ret_width, ret_width)
    out.sg[...] = g * jax.nn.sigmoid(g)
    yield
    for hh, (qh, kh) in enumerate(roped):
        for bi, r0 in enumerate(range(0, tile, blk)):
            scores = lax.dot_general(qh[r0:r0 + blk], kh[r0:r0 + blk], (((1,), (1,)), ((), ())),
                                     preferred_element_type=F32)
            out.probs[hh * (tile // blk) + bi] = (scores * sp.dmat[hh]).astype(BF16)
    yield
    gate_lo = pool_width + 4 * ret_width
    for br in range(N_BRANCHES):
        cols = slice(br * d_model, (br + 1) * d_model)
        out.gates[:, cols] = jax.nn.sigmoid(proj(gate_lo + br * d_model, d_model) + sp.b_gate[:, cols])
        yield


def _mix_phases(st: Staged, x_ref, rows, o_ref, mod_ref, mp: MixParams, tile_decay):
    d_model = x_ref.shape[1]
    ret_width = st.sg.shape[1]
    hd = ret_width // RET_HEADS
    gt_m = mod_ref[2:3, :]

    tile = st.sg.shape[0]
    blk = st.probs.shape[1]
    n_blk = tile // blk
    gated = [[None] * RET_HEADS for _ in range(n_blk)]
    for bi in range(n_blk):
        blk_rows = slice(bi * blk, (bi + 1) * blk)
        for hh in range(RET_HEADS):
            cols = slice(hh * hd, (hh + 1) * hd)
            vh = st.rot[0, blk_rows, cols]
            o = jnp.dot(st.probs[hh * n_blk + bi], vh, preferred_element_type=F32)
            o = o + jnp.dot(st.rot[1, blk_rows, cols], mp.state[hh].astype(BF16),
                            preferred_element_type=F32)
            kv = lax.dot_general(st.rot[2, blk_rows, cols], vh, (((0,), (0,)), ((), ())),
                                 preferred_element_type=F32)
            mp.state[hh] = mp.state[hh] * tile_decay[hh] + kv
            mu = jnp.mean(o, axis=-1, keepdims=True)
            oc = o - mu
            var = jnp.mean(oc * oc, axis=-1, keepdims=True)
            gated[bi][hh] = (st.sg[blk_rows, cols] * (oc * lax.rsqrt(var + EPS))).astype(BF16)
            yield
    y_pool = jnp.dot(st.pooled[...], mp.w_pool[...], preferred_element_type=F32)
    yield
    gated = jnp.concatenate([jnp.concatenate(row, axis=-1) for row in gated], axis=0)
    y_ret = jnp.dot(gated, mp.w_br[...], preferred_element_type=F32)
    yield
    merged = (st.gates[:, 0:d_model] * y_pool + st.gates[:, d_model:2 * d_model] * y_ret).astype(BF16)
    y = jnp.dot(merged, mp.w_out[...], preferred_element_type=F32)
    yield
    o_ref[rows, :] = x_ref[rows, :] + gt_m * (_rms(y) * mp.g_post[...])


WEAVE = "msmmmsmssmssmsssm"


def _weave(order, mix, stage):
    gens = {"m": mix, "s": stage}
    for c in order:
        next(gens[c], None)
    for g in (mix, stage):
        for _ in g:
            pass


def _mixer_kernel(x_ref, xn_ref, mod_all_ref, pos_ref, posn_ref, invf_ref, g_pre_ref, g_post_ref,
                  w_in_ref, b_gate_ref, w_pool_ref, w_br_ref, w_out_ref, dmat_ref, qdec_ref, kdec_ref,
                  wf1_ref, wf2_ref, o_ref, wf1_o_ref, wf2_o_ref, state_ref, halo_ref, *staged,
                  tile, steps_per_seq, n_steps, tile_decay):
    j = pl.program_id(0)
    wf1_o_ref[...] = wf1_ref[...].astype(BF16)
    wf2_o_ref[...] = wf2_ref[...].astype(BF16)
    b, s = j // steps_per_seq, j % steps_per_seq
    j_next = jnp.minimum(j + 1, n_steps - 1)
    b_next, s_next = j_next // steps_per_seq, j_next % steps_per_seq
    mod_ref, modn_ref = ModRows(mod_all_ref, b), ModRows(mod_all_ref, b_next)
    sp = StageParams(g_pre_ref, w_in_ref, b_gate_ref, invf_ref, dmat_ref, qdec_ref, kdec_ref, halo_ref)
    mp = MixParams(g_post_ref, w_pool_ref, w_br_ref, w_out_ref, state_ref)
    n_fields = len(Staged._fields)
    slots = (Staged(*staged[:n_fields]), Staged(*staged[n_fields:]))
    whole = slice(0, tile)

    @pl.when(j == 0)
    def _():
        for _ in _stage_phases(x_ref, whole, mod_ref, pos_ref[pl.ds(b, 1), :], 0, True, sp, slots[0]):
            pass

    @pl.when(s == 0)
    def _():
        state_ref[...] = jnp.zeros_like(state_ref)

    last = n_steps - 1
    for parity in range(2):
        @pl.when((j % 2 == parity) & (j < last))
        def _():
            _weave(WEAVE,
                   _mix_phases(slots[parity], x_ref, whole, o_ref, mod_ref, mp, tile_decay),
                   _stage_phases(xn_ref, whole, modn_ref, posn_ref[pl.ds(b_next, 1), :], s_next,
                                 s_next == 0, sp, slots[1 - parity]))

    @pl.when(j == last)
    def _():
        for _ in _mix_phases(slots[last % 2], x_ref, whole, o_ref, mod_ref, mp, tile_decay):
            pass


def _decay_tables(tile, hd):
    gamma = 1.0 - 2.0 ** (-5.0 - np.arange(RET_HEADS, dtype=np.float64))
    idx = np.arange(tile)
    dist = np.abs(idx[:, None] - idx[None, :])
    visible = (idx[None, :] // CHUNK) <= (idx[:, None] // CHUNK)
    dmat = np.where(visible[None], gamma[:, None, None] ** dist[None], 0.0)
    qdec = np.broadcast_to((gamma[:, None] ** (idx[None, :] + 1))[:, :, None], (RET_HEADS, tile, hd))
    kdec = np.broadcast_to((gamma[:, None] ** (tile - 1 - idx[None, :]))[:, :, None], (RET_HEADS, tile, hd))
    tile_decay = tuple(float(g) ** tile for g in gamma)
    as_f32 = lambda a: jnp.asarray(np.ascontiguousarray(a), dtype=F32)
    return as_f32(dmat), as_f32(qdec), as_f32(kdec), tile_decay


def _const_spec(shape):
    nd = len(shape)
    return pl.BlockSpec(shape, lambda j: (0,) * nd)


def _mixer(x, mod, positions, g_pre, g_post, w_in, b_gate, w_pool, w_br, w_out, w_ffn):
    bsz, seq, d = x.shape
    tile = MIXER_TILE
    pool_width = w_pool.shape[0]
    ret_width = w_br.shape[0]
    hd = ret_width // RET_HEADS
    assert seq % tile == 0 and tile % RET_BLOCK == 0 and RET_BLOCK % CHUNK == 0 and hd == LANES
    steps_per_seq = seq // tile
    n_steps = bsz * steps_per_seq
    dmat, qdec, kdec, tile_decay = _decay_tables(RET_BLOCK, hd)
    half = hd // 2
    invf = (ROPE_BASE ** (-jnp.arange(half, dtype=F32) / half)).reshape(half, 1)

    def cur(j):
        return j // steps_per_seq, j % steps_per_seq

    def nxt(j):
        return cur(jnp.minimum(j + 1, n_steps - 1))

    const = _const_spec
    operands = [
        (x, pl.BlockSpec((None, tile, d), lambda j: (*cur(j), 0))),
        (x, pl.BlockSpec((None, tile, d), lambda j: (*nxt(j), 0))),
        (mod, const(mod.shape)),
        (positions, pl.BlockSpec((bsz, tile), lambda j: (0, cur(j)[1]))),
        (positions, pl.BlockSpec((bsz, tile), lambda j: (0, nxt(j)[1]))),
        (invf, const((half, 1))),
        (g_pre.reshape(1, d), const((1, d))),
        (g_post.reshape(1, d), const((1, d))),
        (w_in, const(w_in.shape)),
        (b_gate.reshape(1, N_BRANCHES * d), const((1, N_BRANCHES * d))),
        (w_pool, const(w_pool.shape)),
        (w_br, const(w_br.shape)),
        (w_out, const(w_out.shape)),
        (dmat, const(dmat.shape)),
        (qdec, const(qdec.shape)),
        (kdec, const(kdec.shape)),
    ] + [(w, pl.BlockSpec((w.shape[0] // n_steps, w.shape[1]), lambda j: (j, 0))) for w in w_ffn]
    assert all(w.shape[0] % (n_steps * 16) == 0 for w in w_ffn)
    staged = [
        pltpu.VMEM((RET_HEADS * (tile // RET_BLOCK), RET_BLOCK, RET_BLOCK), BF16),
        pltpu.VMEM((3, tile, ret_width), BF16),
        pltpu.VMEM((tile, pool_width), BF16),
        pltpu.VMEM((tile, ret_width), F32),
        pltpu.VMEM((tile, N_BRANCHES * d), F32),
    ]
    return pl.pallas_call(
        functools.partial(_mixer_kernel, tile=tile, steps_per_seq=steps_per_seq, n_steps=n_steps,
                          tile_decay=tile_decay),
        out_shape=[jax.ShapeDtypeStruct(x.shape, F32)]
                  + [jax.ShapeDtypeStruct(w.shape, BF16) for w in w_ffn],
        grid=(n_steps,),
        in_specs=[spec for _, spec in operands],
        out_specs=[pl.BlockSpec((None, tile, d), lambda j: (*cur(j), 0))]
                  + [pl.BlockSpec((w.shape[0] // n_steps, w.shape[1]), lambda j: (j, 0)) for w in w_ffn],
        scratch_shapes=[
            pltpu.VMEM((RET_HEADS, hd, hd), F32),
            pltpu.VMEM((POOL_HALO, pool_width), F32),
        ] + staged + staged,
        compiler_params=pltpu.CompilerParams(
            dimension_semantics=("arbitrary",),
            vmem_limit_bytes=VMEM_LIMIT_BYTES),
        name="mixer",
    )(*[a for a, _ in operands])


def _ffn_kernel(x_ref, xn_ref, mod_ref, g_pre_ref, g_post_ref, w1_hbm, w2_hbm, o_ref,
                w1_ref, w2_ref, a0_ref, a1_ref, sem, *, tiles_per_seq, n_tiles):
    j = pl.program_id(0)
    last = n_tiles - 1
    b = j // tiles_per_seq
    b_next = jnp.minimum(j + 1, last) // tiles_per_seq
    d_ff = w1_ref.shape[1]
    width = d_ff // FFN_WEIGHT_CHUNKS
    chunks = [slice(c * width, (c + 1) * width) for c in range(FFN_WEIGHT_CHUNKS)]
    slots = (a0_ref, a1_ref)

    def weight_copies():
        return [(pltpu.make_async_copy(w1_hbm.at[:, ck], w1_ref.at[:, ck], sem.at[0, c]),
                 pltpu.make_async_copy(w2_hbm.at[ck, :], w2_ref.at[ck, :], sem.at[1, c]))
                for c, ck in enumerate(chunks)]

    def up(tile_ref, batch, a_ref, pieces):
        mod = ModRows(mod_ref, batch)
        sh_f, sc_f = mod[3:4, :], mod[4:5, :]
        h = ((_rms(tile_ref[...]) * g_pre_ref[...]) * (1.0 + sc_f) + sh_f).astype(BF16)
        for ck in pieces:
            yield
            a = jnp.maximum(jnp.dot(h, w1_ref[:, ck], preferred_element_type=F32), 0.0)
            a_ref[:, ck] = (a * a).astype(BF16)

    def down(a_ref, cols):
        return jnp.dot(a_ref[...], w2_ref[:, cols], preferred_element_type=F32)

    def finish(y):
        gt_f = ModRows(mod_ref, b)[5:6, :]
        o_ref[...] = x_ref[...] + gt_f * (_rms(y) * g_post_ref[...])

    d_model = w2_ref.shape[1]
    halves = [slice(0, d_model // 2), slice(d_model // 2, d_model)]
    up_halves = [slice(0, d_ff // 2), slice(d_ff // 2, d_ff)]

    @pl.when(j == 0)
    def _():
        copies = weight_copies()
        for cp1, cp2 in copies:
            cp1.start()
            cp2.start()
        first_up = up(x_ref, b, slots[0], chunks)
        for cp1, _ in copies:
            next(first_up)
            cp1.wait()
        next(first_up, None)
        for _, cp2 in copies:
            cp2.wait()

    for parity in range(2):
        @pl.when((j % 2 == parity) & (j < last))
        def _():
            nxt_up = up(xn_ref, b_next, slots[1 - parity], up_halves)
            y_lo = down(slots[parity], halves[0])
            next(nxt_up)
            next(nxt_up)
            y_hi = down(slots[parity], halves[1])
            next(nxt_up, None)
            finish(jnp.concatenate([y_lo, y_hi], axis=1))

    @pl.when(j == last)
    def _():
        finish(down(slots[last % 2], slice(0, d_model)))


def _ffn(x, mod, g_pre, g_post, w1, w2):
    bsz, seq, d = x.shape
    tile = FFN_TILE
    assert seq % tile == 0
    tiles_per_seq = seq // tile

    n_tiles = bsz * tiles_per_seq
    hidden = pltpu.VMEM((tile, w1.shape[1]), BF16)

    def cur(j):
        return j // tiles_per_seq, j % tiles_per_seq

    def nxt(j):
        return cur(jnp.minimum(j + 1, n_tiles - 1))

    return pl.pallas_call(
        functools.partial(_ffn_kernel, tiles_per_seq=tiles_per_seq, n_tiles=n_tiles),
        out_shape=jax.ShapeDtypeStruct(x.shape, F32),
        grid=(n_tiles,),
        in_specs=[
            pl.BlockSpec((None, tile, d), lambda j: (*cur(j), 0)),
            pl.BlockSpec((None, tile, d), lambda j: (*nxt(j), 0)),
            _const_spec(mod.shape),
            _const_spec((1, d)),
            _const_spec((1, d)),
            pl.BlockSpec(memory_space=pl.ANY),
            pl.BlockSpec(memory_space=pl.ANY),
        ],
        out_specs=pl.BlockSpec((None, tile, d), lambda j: (*cur(j), 0)),
        scratch_shapes=[pltpu.VMEM(w1.shape, w1.dtype), pltpu.VMEM(w2.shape, w2.dtype), hidden, hidden,
                        pltpu.SemaphoreType.DMA((2, FFN_WEIGHT_CHUNKS))],
        compiler_params=pltpu.CompilerParams(
            dimension_semantics=("arbitrary",),
            vmem_limit_bytes=VMEM_LIMIT_BYTES),
        name="ffn",
    )(x, x, mod, g_pre.reshape(1, d), g_post.reshape(1, d), w1, w2)


def kernel(x, c, positions, ada_w, ada_b, mix_pre_g, mix_post_g, ffn_pre_g, ffn_post_g, w_in,
           b_branch_gate, pool_w, pool_scale, w_branch_pool, w_branch_ret, w_out, w_ff1, w_ff2):
    depth = ada_w.shape[0]
    for l in range(depth):
        mod, w_pool, w_in_b, w_br_b, w_out_b = _prep(c, ada_w[l], ada_b[l], pool_w[l], pool_scale[l],
                                                     w_branch_pool[l], (w_in[l], w_branch_ret[l], w_out[l]))
        x, w1, w2 = _mixer(x, mod, positions, mix_pre_g[l], mix_post_g[l], w_in_b, b_branch_gate[l],
                           w_pool, w_br_b, w_out_b, (w_ff1[l], w_ff2[l]))
        x = _ffn(x, mod, ffn_pre_g[l], ffn_post_g[l], w1, w2)
    return x
```

```python
import functools
from typing import Any, NamedTuple

import jax
import jax.numpy as jnp
import numpy as np
from jax import lax
from jax.experimental import pallas as pl
from jax.experimental.pallas import tpu as pltpu

F32 = jnp.float32
BF16 = jnp.bfloat16

CHUNK = 64
POOL_WINDOWS = (2, 4, 8, 16)
POOL_HALO = 16
POOL_ROWS = 64
RET_HEADS = 4
N_BRANCHES = 2
N_MOD = 6
ROPE_BASE = 10000.0
EPS = 1e-6

LANES = 128
MIXER_TILE = 512
RET_BLOCK = 256
FFN_TILE = 512
FFN_WEIGHT_CHUNKS = 4
ADALN_BLOCK_N = 768
VMEM_LIMIT_BYTES = 56 * 1024 * 1024


def _rms(xf):
    return xf * lax.rsqrt(jnp.mean(xf * xf, axis=-1, keepdims=True) + EPS)


class ModRows:
    def __init__(self, ref, b):
        self.ref, self.b, self.d = ref, b, ref.shape[1] // N_MOD

    def __getitem__(self, idx):
        k = idx[0].start
        return self.ref[pl.ds(self.b, 1), k * self.d:(k + 1) * self.d]


def _prep_kernel(c_ref, w_ref, b_ref, pw_ref, scale_ref, wbp_ref, *refs):
    n_cast = (len(refs) - 2) // 2
    cast_in, (mod_ref, w_pool_ref), cast_out = refs[:n_cast], refs[n_cast:n_cast + 2], refs[n_cast + 2:]
    c = c_ref[...]
    a = (c * jax.nn.sigmoid(c)).astype(BF16)
    mod_ref[...] = jnp.dot(a, w_ref[...].astype(BF16), preferred_element_type=F32) + b_ref[...]
    w_pool_ref[...] = jnp.dot((pw_ref[...] * scale_ref[...]).astype(BF16), wbp_ref[...].astype(BF16),
                              preferred_element_type=F32).astype(BF16)
    for src, dst in zip(cast_in, cast_out):
        dst[...] = src[...].astype(BF16)


def _prep(c, w, b, pool_w, pool_scale, w_bp, to_cast):
    bsz, d = c.shape
    n = w.shape[1]
    n_steps = n // ADALN_BLOCK_N
    groups, gd, _ = pool_w.shape
    spg = n_steps // groups
    rows = gd // spg
    assert n_steps % groups == 0 and gd % spg == 0 and rows % 16 == 0
    assert all(m.shape[0] % (n_steps * 16) == 0 for m in to_cast)
    cast_specs = [pl.BlockSpec((m.shape[0] // n_steps, m.shape[1]), lambda j: (j, 0)) for m in to_cast]
    return pl.pallas_call(
        _prep_kernel,
        out_shape=[jax.ShapeDtypeStruct((bsz, n), F32), jax.ShapeDtypeStruct(w_bp.shape, BF16)]
                  + [jax.ShapeDtypeStruct(m.shape, BF16) for m in to_cast],
        grid=(n_steps,),
        in_specs=[
            pl.BlockSpec((bsz, d), lambda j: (0, 0)),
            pl.BlockSpec((d, ADALN_BLOCK_N), lambda j: (0, j)),
            pl.BlockSpec((1, ADALN_BLOCK_N), lambda j: (0, j)),
            pl.BlockSpec((None, rows, gd), lambda j: (j // spg, j % spg, 0)),
            pl.BlockSpec((1, gd), lambda j: (0, j // spg)),
            pl.BlockSpec((gd, w_bp.shape[1]), lambda j: (j // spg, 0)),
        ] + cast_specs,
        out_specs=[pl.BlockSpec((bsz, ADALN_BLOCK_N), lambda j: (0, j)),
                   pl.BlockSpec((rows, w_bp.shape[1]), lambda j: (j, 0))] + cast_specs,
        compiler_params=pltpu.CompilerParams(dimension_semantics=("arbitrary",)),
        name="prep",
    )(c, w, b.reshape(1, n), pool_w, pool_scale.reshape(1, groups * gd), w_bp, *to_cast)


class Staged(NamedTuple):
    probs: Any
    rot: Any
    pooled: Any
    sg: Any
    gates: Any


class StageParams(NamedTuple):
    g_pre: Any
    w_in: Any
    b_gate: Any
    invf: Any
    dmat: Any
    qdec: Any
    kdec: Any
    halo: Any


class MixParams(NamedTuple):
    g_post: Any
    w_pool: Any
    w_br: Any
    w_out: Any
    state: Any


def _stage_phases(x_ref, rows, mod_ref, pos, seq_tile, first, sp: StageParams, out: Staged):
    tile = rows.stop - rows.start
    pool_width = out.pooled.shape[1]
    ret_width = out.sg.shape[1]
    d_model = out.gates.shape[1] // N_BRANCHES
    hd = ret_width // RET_HEADS
    sh_m, sc_m = mod_ref[0:1, :], mod_ref[1:2, :]
    h = ((_rms(x_ref[rows, :]) * sp.g_pre[...]) * (1.0 + sc_m) + sh_m).astype(BF16)
    yield

    def proj(lo, width):
        return jnp.dot(h, sp.w_in[:, lo:lo + width], preferred_element_type=F32)

    u = proj(0, pool_width)
    yield

    if first is True:
        halo = jnp.zeros(sp.halo.shape, F32)
    elif first is False:
        halo = sp.halo[...]
    else:
        halo = jnp.where(first, 0.0, sp.halo[...])
    sp.halo[...] = u[tile - POOL_HALO:tile, :]
    t_pos = (lax.broadcasted_iota(jnp.int32, (tile, LANES), 0) + (seq_tile * tile + 1)).astype(F32)
    inv_t = 1.0 / t_pos
    gd = pool_width // len(POOL_WINDOWS)
    for gi, w in enumerate(POOL_WINDOWS):
        cols = slice(gi * gd, (gi + 1) * gd)
        inv_cnt = jnp.where(t_pos < float(w), inv_t, 1.0 / w)
        for r0 in range(0, tile, POOL_ROWS):
            blk = slice(r0, r0 + POOL_ROWS)
            before = halo[:, cols] if r0 == 0 else u[r0 - POOL_HALO:r0, cols]
            s = jnp.concatenate([before, u[blk, cols]], axis=0)
            k = 1
            while k < w:
                s = s + pltpu.roll(s, k, 0)
                k *= 2
            out.pooled[blk, cols] = (s[POOL_HALO:] * inv_cnt[blk] - u[blk, cols]).astype(BF16)
    q = proj(pool_width, ret_width)
    yield
    k_all = proj(pool_width + ret_width, ret_width)
    yield

    ang = sp.invf[...] * pos.astype(F32)
    cos_h, sin_h = jnp.cos(ang), jnp.sin(ang)
    cos_t = jnp.transpose(jnp.concatenate([cos_h, cos_h], axis=0))
    sin_t = jnp.transpose(jnp.concatenate([-sin_h, sin_h], axis=0))
    blk = sp.dmat.shape[1]
    roped = []
    for hh in range(RET_HEADS):
        cols = slice(hh * hd, (hh + 1) * hd)
        qh, kh = q[:, cols], k_all[:, cols]
        qh = (qh * cos_t + pltpu.roll(qh, hd // 2, 1) * sin_t) * (hd ** -0.5)
        kh = kh * cos_t + pltpu.roll(kh, hd // 2, 1) * sin_t
        roped.append((qh.astype(BF16), kh.astype(BF16)))
        for r0 in range(0, tile, blk):
            out.rot[1, r0:r0 + blk, cols] = (qh[r0:r0 + blk] * sp.qdec[hh]).astype(BF16)
            out.rot[2, r0:r0 + blk, cols] = (kh[r0:r0 + blk] * sp.kdec[hh]).astype(BF16)
    out.rot[0] = proj(pool_width + 2 * ret_width, ret_width).astype(BF16)
    yield
    g = proj(pool_width + 3 * ret_width, ret_width)
    out.sg[...] = g * jax.nn.sigmoid(g)
    yield
    for hh, (qh, kh) in enumerate(roped):
        for bi, r0 in enumerate(range(0, tile, blk)):
            scores = lax.dot_general(qh[r0:r0 + blk], kh[r0:r0 + blk], (((1,), (1,)), ((), ())),
                                     preferred_element_type=F32)
            out.probs[hh * (tile // blk) + bi] = (scores * sp.dmat[hh]).astype(BF16)
    yield
    gate_lo = pool_width + 4 * ret_width
    for br in range(N_BRANCHES):
        cols = slice(br * d_model, (br + 1) * d_model)
        out.gates[:, cols] = jax.nn.sigmoid(proj(gate_lo + br * d_model, d_model) + sp.b_gate[:, cols])
        yield


def _mix_phases(st: Staged, x_ref, rows, o_ref, mod_ref, mp: MixParams, tile_decay):
    d_model = x_ref.shape[1]
    ret_width = st.sg.shape[1]
    hd = ret_width // RET_HEADS
    gt_m = mod_ref[2:3, :]

    tile = st.sg.shape[0]
    blk = st.probs.shape[1]
    n_blk = tile // blk
    gated = [[None] * RET_HEADS for _ in range(n_blk)]
    for bi in range(n_blk):
        blk_rows = slice(bi * blk, (bi + 1) * blk)
        for hh in range(RET_HEADS):
            cols = slice(hh * hd, (hh + 1) * hd)
            vh = st.rot[0, blk_rows, cols]
            o = jnp.dot(st.probs[hh * n_blk + bi], vh, preferred_element_type=F32)
            o = o + jnp.dot(st.rot[1, blk_rows, cols], mp.state[hh].astype(BF16),
                            preferred_element_type=F32)
            kv = lax.dot_general(st.rot[2, blk_rows, cols], vh, (((0,), (0,)), ((), ())),
                                 preferred_element_type=F32)
            mp.state[hh] = mp.state[hh] * tile_decay[hh] + kv
            mu = jnp.mean(o, axis=-1, keepdims=True)
            oc = o - mu
            var = jnp.mean(oc * oc, axis=-1, keepdims=True)
            gated[bi][hh] = (st.sg[blk_rows, cols] * (oc * lax.rsqrt(var + EPS))).astype(BF16)
            yield
    y_pool = jnp.dot(st.pooled[...], mp.w_pool[...], preferred_element_type=F32)
    yield
    gated = jnp.concatenate([jnp.concatenate(row, axis=-1) for row in gated], axis=0)
    y_ret = jnp.dot(gated, mp.w_br[...], preferred_element_type=F32)
    yield
    merged = (st.gates[:, 0:d_model] * y_pool + st.gates[:, d_model:2 * d_model] * y_ret).astype(BF16)
    y = jnp.dot(merged, mp.w_out[...], preferred_element_type=F32)
    yield
    o_ref[rows, :] = x_ref[rows, :] + gt_m * (_rms(y) * mp.g_post[...])


WEAVE = "mms" * 4 + "mssmsmssm"


def _weave(order, mix, stage):
    gens = {"m": mix, "s": stage}
    for c in order:
        next(gens[c], None)
    for g in (mix, stage):
        for _ in g:
            pass


def _mixer_kernel(x_ref, xn_ref, mod_all_ref, pos_ref, posn_ref, invf_ref, g_pre_ref, g_post_ref,
                  w_in_ref, b_gate_ref, w_pool_ref, w_br_ref, w_out_ref, dmat_ref, qdec_ref, kdec_ref,
                  wf1_ref, wf2_ref, o_ref, wf1_o_ref, wf2_o_ref, state_ref, halo_ref, *staged,
                  tile, steps_per_seq, n_steps, tile_decay):
    j = pl.program_id(0)
    wf1_o_ref[...] = wf1_ref[...].astype(BF16)
    wf2_o_ref[...] = wf2_ref[...].astype(BF16)
    b, s = j // steps_per_seq, j % steps_per_seq
    j_next = jnp.minimum(j + 1, n_steps - 1)
    b_next, s_next = j_next // steps_per_seq, j_next % steps_per_seq
    mod_ref, modn_ref = ModRows(mod_all_ref, b), ModRows(mod_all_ref, b_next)
    sp = StageParams(g_pre_ref, w_in_ref, b_gate_ref, invf_ref, dmat_ref, qdec_ref, kdec_ref, halo_ref)
    mp = MixParams(g_post_ref, w_pool_ref, w_br_ref, w_out_ref, state_ref)
    n_fields = len(Staged._fields)
    slots = (Staged(*staged[:n_fields]), Staged(*staged[n_fields:]))
    whole = slice(0, tile)

    @pl.when(j == 0)
    def _():
        for _ in _stage_phases(x_ref, whole, mod_ref, pos_ref[pl.ds(b, 1), :], 0, True, sp, slots[0]):
            pass

    @pl.when(s == 0)
    def _():
        state_ref[...] = jnp.zeros_like(state_ref)

    last = n_steps - 1
    for parity in range(2):
        @pl.when((j % 2 == parity) & (j < last))
        def _():
            _weave(WEAVE,
                   _mix_phases(slots[parity], x_ref, whole, o_ref, mod_ref, mp, tile_decay),
                   _stage_phases(xn_ref, whole, modn_ref, posn_ref[pl.ds(b_next, 1), :], s_next,
                                 s_next == 0, sp, slots[1 - parity]))

    @pl.when(j == last)
    def _():
        for _ in _mix_phases(slots[last % 2], x_ref, whole, o_ref, mod_ref, mp, tile_decay):
            pass


def _decay_tables(tile, hd):
    gamma = 1.0 - 2.0 ** (-5.0 - np.arange(RET_HEADS, dtype=np.float64))
    idx = np.arange(tile)
    dist = np.abs(idx[:, None] - idx[None, :])
    visible = (idx[None, :] // CHUNK) <= (idx[:, None] // CHUNK)
    dmat = np.where(visible[None], gamma[:, None, None] ** dist[None], 0.0)
    qdec = np.broadcast_to((gamma[:, None] ** (idx[None, :] + 1))[:, :, None], (RET_HEADS, tile, hd))
    kdec = np.broadcast_to((gamma[:, None] ** (tile - 1 - idx[None, :]))[:, :, None], (RET_HEADS, tile, hd))
    tile_decay = tuple(float(g) ** tile for g in gamma)
    as_f32 = lambda a: jnp.asarray(np.ascontiguousarray(a), dtype=F32)
    return as_f32(dmat), as_f32(qdec), as_f32(kdec), tile_decay


def _const_spec(shape):
    nd = len(shape)
    return pl.BlockSpec(shape, lambda j: (0,) * nd)


def _mixer(x, mod, positions, g_pre, g_post, w_in, b_gate, w_pool, w_br, w_out, w_ffn):
    bsz, seq, d = x.shape
    tile = MIXER_TILE
    pool_width = w_pool.shape[0]
    ret_width = w_br.shape[0]
    hd = ret_width // RET_HEADS
    assert seq % tile == 0 and tile % RET_BLOCK == 0 and RET_BLOCK % CHUNK == 0 and hd == LANES
    steps_per_seq = seq // tile
    n_steps = bsz * steps_per_seq
    dmat, qdec, kdec, tile_decay = _decay_tables(RET_BLOCK, hd)
    half = hd // 2
    invf = (ROPE_BASE ** (-jnp.arange(half, dtype=F32) / half)).reshape(half, 1)

    def cur(j):
        return j // steps_per_seq, j % steps_per_seq

    def nxt(j):
        return cur(jnp.minimum(j + 1, n_steps - 1))

    const = _const_spec
    operands = [
        (x, pl.BlockSpec((None, tile, d), lambda j: (*cur(j), 0))),
        (x, pl.BlockSpec((None, tile, d), lambda j: (*nxt(j), 0))),
        (mod, const(mod.shape)),
        (positions, pl.BlockSpec((bsz, tile), lambda j: (0, cur(j)[1]))),
        (positions, pl.BlockSpec((bsz, tile), lambda j: (0, nxt(j)[1]))),
        (invf, const((half, 1))),
        (g_pre.reshape(1, d), const((1, d))),
        (g_post.reshape(1, d), const((1, d))),
        (w_in, const(w_in.shape)),
        (b_gate.reshape(1, N_BRANCHES * d), const((1, N_BRANCHES * d))),
        (w_pool, const(w_pool.shape)),
        (w_br, const(w_br.shape)),
        (w_out, const(w_out.shape)),
        (dmat, const(dmat.shape)),
        (qdec, const(qdec.shape)),
        (kdec, const(kdec.shape)),
    ] + [(w, pl.BlockSpec((w.shape[0] // n_steps, w.shape[1]), lambda j: (j, 0))) for w in w_ffn]
    assert all(w.shape[0] % (n_steps * 16) == 0 for w in w_ffn)
    staged = [
        pltpu.VMEM((RET_HEADS * (tile // RET_BLOCK), RET_BLOCK, RET_BLOCK), BF16),
        pltpu.VMEM((3, tile, ret_width), BF16),
        pltpu.VMEM((tile, pool_width), BF16),
        pltpu.VMEM((tile, ret_width), F32),
        pltpu.VMEM((tile, N_BRANCHES * d), F32),
    ]
    return pl.pallas_call(
        functools.partial(_mixer_kernel, tile=tile, steps_per_seq=steps_per_seq, n_steps=n_steps,
                          tile_decay=tile_decay),
        out_shape=[jax.ShapeDtypeStruct(x.shape, F32)]
                  + [jax.ShapeDtypeStruct(w.shape, BF16) for w in w_ffn],
        grid=(n_steps,),
        in_specs=[spec for _, spec in operands],
        out_specs=[pl.BlockSpec((None, tile, d), lambda j: (*cur(j), 0))]
                  + [pl.BlockSpec((w.shape[0] // n_steps, w.shape[1]), lambda j: (j, 0)) for w in w_ffn],
        scratch_shapes=[
            pltpu.VMEM((RET_HEADS, hd, hd), F32),
            pltpu.VMEM((POOL_HALO, pool_width), F32),
        ] + staged + staged,
        compiler_params=pltpu.CompilerParams(
            dimension_semantics=("arbitrary",),
            vmem_limit_bytes=VMEM_LIMIT_BYTES),
        name="mixer",
    )(*[a for a, _ in operands])


def _ffn_kernel(x_ref, xn_ref, mod_ref, g_pre_ref, g_post_ref, w1_hbm, w2_hbm, o_ref,
                w1_ref, w2_ref, a0_ref, a1_ref, sem, *, tiles_per_seq, n_tiles):
    j = pl.program_id(0)
    last = n_tiles - 1
    b = j // tiles_per_seq
    b_next = jnp.minimum(j + 1, last) // tiles_per_seq
    d_ff = w1_ref.shape[1]
    width = d_ff // FFN_WEIGHT_CHUNKS
    chunks = [slice(c * width, (c + 1) * width) for c in range(FFN_WEIGHT_CHUNKS)]
    slots = (a0_ref, a1_ref)

    def weight_copies():
        return [(pltpu.make_async_copy(w1_hbm.at[:, ck], w1_ref.at[:, ck], sem.at[0, c]),
                 pltpu.make_async_copy(w2_hbm.at[ck, :], w2_ref.at[ck, :], sem.at[1, c]))
                for c, ck in enumerate(chunks)]

    def up(tile_ref, batch, a_ref, pieces):
        mod = ModRows(mod_ref, batch)
        sh_f, sc_f = mod[3:4, :], mod[4:5, :]
        h = ((_rms(tile_ref[...]) * g_pre_ref[...]) * (1.0 + sc_f) + sh_f).astype(BF16)
        for ck in pieces:
            yield
            a = jnp.maximum(jnp.dot(h, w1_ref[:, ck], preferred_element_type=F32), 0.0)
            a_ref[:, ck] = (a * a).astype(BF16)

    def down(a_ref, cols):
        return jnp.dot(a_ref[...], w2_ref[:, cols], preferred_element_type=F32)

    def finish(y):
        gt_f = ModRows(mod_ref, b)[5:6, :]
        o_ref[...] = x_ref[...] + gt_f * (_rms(y) * g_post_ref[...])

    d_model = w2_ref.shape[1]
    halves = [slice(0, d_model // 2), slice(d_model // 2, d_model)]
    up_halves = [slice(0, d_ff // 2), slice(d_ff // 2, d_ff)]

    @pl.when(j == 0)
    def _():
        copies = weight_copies()
        for cp1, cp2 in copies:
            cp1.start()
            cp2.start()
        first_up = up(x_ref, b, slots[0], chunks)
        for cp1, _ in copies:
            next(first_up)
            cp1.wait()
        next(first_up, None)
        for _, cp2 in copies:
            cp2.wait()

    for parity in range(2):
        @pl.when((j % 2 == parity) & (j < last))
        def _():
            nxt_up = up(xn_ref, b_next, slots[1 - parity], up_halves)
            y_lo = down(slots[parity], halves[0])
            next(nxt_up)
            next(nxt_up)
            y_hi = down(slots[parity], halves[1])
            next(nxt_up, None)
            finish(jnp.concatenate([y_lo, y_hi], axis=1))

    @pl.when(j == last)
    def _():
        finish(down(slots[last % 2], slice(0, d_model)))


def _ffn(x, mod, g_pre, g_post, w1, w2):
    bsz, seq, d = x.shape
    tile = FFN_TILE
    assert seq % tile == 0
    tiles_per_seq = seq // tile

    n_tiles = bsz * tiles_per_seq
    hidden = pltpu.VMEM((tile, w1.shape[1]), BF16)

    def cur(j):
        return j // tiles_per_seq, j % tiles_per_seq

    def nxt(j):
        return cur(jnp.minimum(j + 1, n_tiles - 1))

    return pl.pallas_call(
        functools.partial(_ffn_kernel, tiles_per_seq=tiles_per_seq, n_tiles=n_tiles),
        out_shape=jax.ShapeDtypeStruct(x.shape, F32),
        grid=(n_tiles,),
        in_specs=[
            pl.BlockSpec((None, tile, d), lambda j: (*cur(j), 0)),
            pl.BlockSpec((None, tile, d), lambda j: (*nxt(j), 0)),
            _const_spec(mod.shape),
            _const_spec((1, d)),
            _const_spec((1, d)),
            pl.BlockSpec(memory_space=pl.ANY),
            pl.BlockSpec(memory_space=pl.ANY),
        ],
        out_specs=pl.BlockSpec((None, tile, d), lambda j: (*cur(j), 0)),
        scratch_shapes=[pltpu.VMEM(w1.shape, w1.dtype), pltpu.VMEM(w2.shape, w2.dtype), hidden, hidden,
                        pltpu.SemaphoreType.DMA((2, FFN_WEIGHT_CHUNKS))],
        compiler_params=pltpu.CompilerParams(
            dimension_semantics=("arbitrary",),
            vmem_limit_bytes=VMEM_LIMIT_BYTES),
        name="ffn",
    )(x, x, mod, g_pre.reshape(1, d), g_post.reshape(1, d), w1, w2)


def kernel(x, c, positions, ada_w, ada_b, mix_pre_g, mix_post_g, ffn_pre_g, ffn_post_g, w_in,
           b_branch_gate, pool_w, pool_scale, w_branch_pool, w_branch_ret, w_out, w_ff1, w_ff2):
    depth = ada_w.shape[0]
    for l in range(depth):
        mod, w_pool, w_in_b, w_br_b, w_out_b = _prep(c, ada_w[l], ada_b[l], pool_w[l], pool_scale[l],
                                                     w_branch_pool[l], (w_in[l], w_branch_ret[l], w_out[l]))
        x, w1, w2 = _mixer(x, mod, positions, mix_pre_g[l], mix_post_g[l], w_in_b, b_branch_gate[l],
                           w_pool, w_br_b, w_out_b, (w_ff1[l], w_ff2[l]))
        x = _ffn(x, mod, ffn_pre_g[l], ffn_post_g[l], w1, w2)
    return x
```

```python
import functools
from typing import Any, NamedTuple

import jax
import jax.numpy as jnp
import numpy as np
from jax import lax
from jax.experimental import pallas as pl
from jax.experimental.pallas import tpu as pltpu

F32 = jnp.float32
BF16 = jnp.bfloat16

CHUNK = 64
POOL_WINDOWS = (2, 4, 8, 16)
POOL_HALO = 16
POOL_ROWS = 64
RET_HEADS = 4
N_BRANCHES = 2
N_MOD = 6
ROPE_BASE = 10000.0
EPS = 1e-6

LANES = 128
MIXER_TILE = 256
RET_BLOCK = 256
FFN_TILE = 512
UP_CHUNK = 512
ADALN_BLOCK_N = 768
VMEM_LIMIT_BYTES = 56 * 1024 * 1024


def _rms(xf):
    return xf * lax.rsqrt(jnp.mean(xf * xf, axis=-1, keepdims=True) + EPS)


class ModRows:
    def __init__(self, ref, b):
        self.ref, self.b, self.d = ref, b, ref.shape[1] // N_MOD

    def __getitem__(self, idx):
        k = idx[0].start
        return self.ref[pl.ds(self.b, 1), k * self.d:(k + 1) * self.d]


def _prep_kernel(c_ref, w_ref, b_ref, pw_ref, scale_ref, wbp_ref, *refs):
    n_cast = (len(refs) - 2) // 2
    cast_in, (mod_ref, w_pool_ref), cast_out = refs[:n_cast], refs[n_cast:n_cast + 2], refs[n_cast + 2:]
    c = c_ref[...]
    a = (c * jax.nn.sigmoid(c)).astype(BF16)
    mod_ref[...] = jnp.dot(a, w_ref[...].astype(BF16), preferred_element_type=F32) + b_ref[...]
    w_pool_ref[...] = jnp.dot((pw_ref[...] * scale_ref[...]).astype(BF16), wbp_ref[...].astype(BF16),
                              preferred_element_type=F32).astype(BF16)
    for src, dst in zip(cast_in, cast_out):
        dst[...] = src[...].astype(BF16)


def _prep(c, w, b, pool_w, pool_scale, w_bp, to_cast):
    bsz, d = c.shape
    n = w.shape[1]
    n_steps = n // ADALN_BLOCK_N
    groups, gd, _ = pool_w.shape
    spg = n_steps // groups
    rows = gd // spg
    assert n_steps % groups == 0 and gd % spg == 0 and rows % 16 == 0
    assert all(m.shape[0] % (n_steps * 16) == 0 for m in to_cast)
    cast_specs = [pl.BlockSpec((m.shape[0] // n_steps, m.shape[1]), lambda j: (j, 0)) for m in to_cast]
    return pl.pallas_call(
        _prep_kernel,
        out_shape=[jax.ShapeDtypeStruct((bsz, n), F32), jax.ShapeDtypeStruct(w_bp.shape, BF16)]
                  + [jax.ShapeDtypeStruct(m.shape, BF16) for m in to_cast],
        grid=(n_steps,),
        in_specs=[
            pl.BlockSpec((bsz, d), lambda j: (0, 0)),
            pl.BlockSpec((d, ADALN_BLOCK_N), lambda j: (0, j)),
            pl.BlockSpec((1, ADALN_BLOCK_N), lambda j: (0, j)),
            pl.BlockSpec((None, rows, gd), lambda j: (j // spg, j % spg, 0)),
            pl.BlockSpec((1, gd), lambda j: (0, j // spg)),
            pl.BlockSpec((gd, w_bp.shape[1]), lambda j: (j // spg, 0)),
        ] + cast_specs,
        out_specs=[pl.BlockSpec((bsz, ADALN_BLOCK_N), lambda j: (0, j)),
                   pl.BlockSpec((rows, w_bp.shape[1]), lambda j: (j, 0))] + cast_specs,
        compiler_params=pltpu.CompilerParams(dimension_semantics=("arbitrary",)),
        name="prep",
    )(c, w, b.reshape(1, n), pool_w, pool_scale.reshape(1, groups * gd), w_bp, *to_cast)


class Staged(NamedTuple):
    probs: Any
    rot: Any
    pooled: Any
    sg: Any
    gates: Any


class StageParams(NamedTuple):
    g_pre: Any
    w_in: Any
    b_gate: Any
    invf: Any
    dmat: Any
    qdec: Any
    kdec: Any
    halo: Any


class MixParams(NamedTuple):
    g_post: Any
    w_pool: Any
    w_br: Any
    w_out: Any
    state: Any
    x1: Any


def _stage_phases(x_ref, rows, mod_ref, pos, seq_tile, first, sp: StageParams, out: Staged):
    tile = rows.stop - rows.start
    pool_width = out.pooled.shape[1]
    ret_width = out.sg.shape[1]
    d_model = out.gates.shape[1] // N_BRANCHES
    hd = ret_width // RET_HEADS
    sh_m, sc_m = mod_ref[0:1, :], mod_ref[1:2, :]
    h = ((_rms(x_ref[rows, :]) * sp.g_pre[...]) * (1.0 + sc_m) + sh_m).astype(BF16)
    yield

    def proj(lo, width):
        return jnp.dot(h, sp.w_in[:, lo:lo + width], preferred_element_type=F32)

    u = proj(0, pool_width)
    yield

    if first is True:
        halo = jnp.zeros(sp.halo.shape, F32)
    elif first is False:
        halo = sp.halo[...]
    else:
        halo = jnp.where(first, 0.0, sp.halo[...])
    sp.halo[...] = u[tile - POOL_HALO:tile, :]
    t_pos = (lax.broadcasted_iota(jnp.int32, (tile, LANES), 0) + (seq_tile * tile + 1)).astype(F32)
    inv_t = 1.0 / t_pos
    gd = pool_width // len(POOL_WINDOWS)
    for gi, w in enumerate(POOL_WINDOWS):
        cols = slice(gi * gd, (gi + 1) * gd)
        inv_cnt = jnp.where(t_pos < float(w), inv_t, 1.0 / w)
        for r0 in range(0, tile, POOL_ROWS):
            blk = slice(r0, r0 + POOL_ROWS)
            before = halo[:, cols] if r0 == 0 else u[r0 - POOL_HALO:r0, cols]
            s = jnp.concatenate([before, u[blk, cols]], axis=0)
            k = 1
            while k < w:
                s = s + pltpu.roll(s, k, 0)
                k *= 2
            out.pooled[blk, cols] = (s[POOL_HALO:] * inv_cnt[blk] - u[blk, cols]).astype(BF16)
    q = proj(pool_width, ret_width)
    yield
    k_all = proj(pool_width + ret_width, ret_width)
    yield

    ang = sp.invf[...] * pos.astype(F32)
    cos_h, sin_h = jnp.cos(ang), jnp.sin(ang)
    cos_t = jnp.transpose(jnp.concatenate([cos_h, cos_h], axis=0))
    sin_t = jnp.transpose(jnp.concatenate([-sin_h, sin_h], axis=0))
    blk = sp.dmat.shape[1]
    roped = []
    for hh in range(RET_HEADS):
        cols = slice(hh * hd, (hh + 1) * hd)
        qh, kh = q[:, cols], k_all[:, cols]
        qh = (qh * cos_t + pltpu.roll(qh, hd // 2, 1) * sin_t) * (hd ** -0.5)
        kh = kh * cos_t + pltpu.roll(kh, hd // 2, 1) * sin_t
        roped.append((qh.astype(BF16), kh.astype(BF16)))
        for r0 in range(0, tile, blk):
            out.rot[1, r0:r0 + blk, cols] = (qh[r0:r0 + blk] * sp.qdec[hh]).astype(BF16)
            out.rot[2, r0:r0 + blk, cols] = (kh[r0:r0 + blk] * sp.kdec[hh]).astype(BF16)
    out.rot[0] = proj(pool_width + 2 * ret_width, ret_width).astype(BF16)
    yield
    g = proj(pool_width + 3 * ret_width, ret_width)
    out.sg[...] = g * jax.nn.sigmoid(g)
    yield
    for hh, (qh, kh) in enumerate(roped):
        for bi, r0 in enumerate(range(0, tile, blk)):
            scores = lax.dot_general(qh[r0:r0 + blk], kh[r0:r0 + blk], (((1,), (1,)), ((), ())),
                                     preferred_element_type=F32)
            out.probs[hh * (tile // blk) + bi] = (scores * sp.dmat[hh]).astype(BF16)
    yield
    gate_lo = pool_width + 4 * ret_width
    for br in range(N_BRANCHES):
        cols = slice(br * d_model, (br + 1) * d_model)
        out.gates[:, cols] = jax.nn.sigmoid(proj(gate_lo + br * d_model, d_model) + sp.b_gate[:, cols])
        yield


def _mix_phases(st: Staged, x_ref, rows, o_ref, mod_ref, mp: MixParams, tile_decay):
    d_model = x_ref.shape[1]
    ret_width = st.sg.shape[1]
    hd = ret_width // RET_HEADS
    gt_m = mod_ref[2:3, :]

    tile = st.sg.shape[0]
    blk = st.probs.shape[1]
    n_blk = tile // blk
    gated = [[None] * RET_HEADS for _ in range(n_blk)]
    for bi in range(n_blk):
        blk_rows = slice(bi * blk, (bi + 1) * blk)
        for hh in range(RET_HEADS):
            cols = slice(hh * hd, (hh + 1) * hd)
            vh = st.rot[0, blk_rows, cols]
            o = jnp.dot(st.probs[hh * n_blk + bi], vh, preferred_element_type=F32)
            o = o + jnp.dot(st.rot[1, blk_rows, cols], mp.state[hh].astype(BF16),
                            preferred_element_type=F32)
            kv = lax.dot_general(st.rot[2, blk_rows, cols], vh, (((0,), (0,)), ((), ())),
                                 preferred_element_type=F32)
            mp.state[hh] = mp.state[hh] * tile_decay[hh] + kv
            mu = jnp.mean(o, axis=-1, keepdims=True)
            oc = o - mu
            var = jnp.mean(oc * oc, axis=-1, keepdims=True)
            gated[bi][hh] = (st.sg[blk_rows, cols] * (oc * lax.rsqrt(var + EPS))).astype(BF16)
            yield
    y_pool = jnp.dot(st.pooled[...], mp.w_pool[...], preferred_element_type=F32)
    yield
    gated = jnp.concatenate([jnp.concatenate(row, axis=-1) for row in gated], axis=0)
    y_ret = jnp.dot(gated, mp.w_br[...], preferred_element_type=F32)
    yield
    merged = (st.gates[:, 0:d_model] * y_pool + st.gates[:, d_model:2 * d_model] * y_ret).astype(BF16)
    y = jnp.dot(merged, mp.w_out[...], preferred_element_type=F32)
    yield
    x1 = x_ref[rows, :] + gt_m * (_rms(y) * mp.g_post[...])
    o_ref[rows, :] = x1
    mp.x1[...] = x1


def _up_phases(x1_ref, mod_ref, g_ref, w1_ref, a_ref):
    sh_f, sc_f = mod_ref[3:4, :], mod_ref[4:5, :]
    h = ((_rms(x1_ref[...]) * g_ref[...]) * (1.0 + sc_f) + sh_f).astype(BF16)
    yield
    for c0 in range(0, w1_ref.shape[1], UP_CHUNK):
        a = jnp.maximum(jnp.dot(h, w1_ref[:, c0:c0 + UP_CHUNK], preferred_element_type=F32), 0.0)
        a_ref[:, c0:c0 + UP_CHUNK] = (a * a).astype(BF16)
        yield


WEAVE = "musmmmsumussumussumussusum"


def _weave(order, **gens):
    for c in order:
        if c in gens:
            next(gens[c], None)
    for g in gens.values():
        for _ in g:
            pass


def _mixer_kernel(x_ref, xn_ref, mod_all_ref, pos_ref, posn_ref, invf_ref, g_pre_ref, g_post_ref,
                  w_in_ref, b_gate_ref, w_pool_ref, w_br_ref, w_out_ref, dmat_ref, qdec_ref, kdec_ref,
                  g_ffn_ref, w1_ref, wf2_ref, o_ref, a_ref, wf2_o_ref, state_ref, halo_ref, x1_ref, *staged,
                  tile, steps_per_seq, n_tiles, tile_decay):
    j = pl.program_id(0)
    wf2_o_ref[...] = wf2_ref[...].astype(BF16)
    last = n_tiles - 1

    def batch_and_tile(t):
        return t // steps_per_seq, t % steps_per_seq

    b, s = batch_and_tile(jnp.minimum(j, last))
    b_next, s_next = batch_and_tile(jnp.minimum(j + 1, last))
    b_prev, _ = batch_and_tile(jnp.maximum(j - 1, 0))
    mod_ref, modn_ref, modp_ref = (ModRows(mod_all_ref, r) for r in (b, b_next, b_prev))
    sp = StageParams(g_pre_ref, w_in_ref, b_gate_ref, invf_ref, dmat_ref, qdec_ref, kdec_ref, halo_ref)
    mp = MixParams(g_post_ref, w_pool_ref, w_br_ref, w_out_ref, state_ref, x1_ref)
    n_fields = len(Staged._fields)
    slots = (Staged(*staged[:n_fields]), Staged(*staged[n_fields:]))
    whole = slice(0, tile)

    @pl.when(j == 0)
    def _():
        for _ in _stage_phases(x_ref, whole, mod_ref, pos_ref[pl.ds(b, 1), :], 0, True, sp, slots[0]):
            pass
        x1_ref[...] = jnp.zeros_like(x1_ref)

    @pl.when((s == 0) & (j <= last))
    def _():
        state_ref[...] = jnp.zeros_like(state_ref)

    def mix(slot):
        return _mix_phases(slot, x_ref, whole, o_ref, mod_ref, mp, tile_decay)

    def up():
        return _up_phases(x1_ref, modp_ref, g_ffn_ref, w1_ref, a_ref)

    for parity in range(2):
        @pl.when((j % 2 == parity) & (j < last))
        def _():
            _weave(WEAVE, m=mix(slots[parity]), u=up(),
                   s=_stage_phases(xn_ref, whole, modn_ref, posn_ref[pl.ds(b_next, 1), :], s_next,
                                   s_next == 0, sp, slots[1 - parity]))

    @pl.when(j == last)
    def _():
        _weave(WEAVE, m=mix(slots[last % 2]), u=up())

    @pl.when(j == n_tiles)
    def _():
        _weave(WEAVE, u=up())


def _decay_tables(tile, hd):
    gamma = 1.0 - 2.0 ** (-5.0 - np.arange(RET_HEADS, dtype=np.float64))
    idx = np.arange(tile)
    dist = np.abs(idx[:, None] - idx[None, :])
    visible = (idx[None, :] // CHUNK) <= (idx[:, None] // CHUNK)
    dmat = np.where(visible[None], gamma[:, None, None] ** dist[None], 0.0)
    qdec = np.broadcast_to((gamma[:, None] ** (idx[None, :] + 1))[:, :, None], (RET_HEADS, tile, hd))
    kdec = np.broadcast_to((gamma[:, None] ** (tile - 1 - idx[None, :]))[:, :, None], (RET_HEADS, tile, hd))
    tile_decay = tuple(float(g) ** tile for g in gamma)
    as_f32 = lambda a: jnp.asarray(np.ascontiguousarray(a), dtype=F32)
    return as_f32(dmat), as_f32(qdec), as_f32(kdec), tile_decay


def _const_spec(shape):
    nd = len(shape)
    return pl.BlockSpec(shape, lambda j: (0,) * nd)


def _mixer(x, mod, positions, g_pre, g_post, w_in, b_gate, w_pool, w_br, w_out, g_ffn, w1, w2):
    bsz, seq, d = x.shape
    tile = MIXER_TILE
    pool_width = w_pool.shape[0]
    ret_width = w_br.shape[0]
    d_ff = w1.shape[1]
    hd = ret_width // RET_HEADS
    assert seq % tile == 0 and tile % RET_BLOCK == 0 and RET_BLOCK % CHUNK == 0 and hd == LANES
    assert d_ff % UP_CHUNK == 0
    steps_per_seq = seq // tile
    n_tiles = bsz * steps_per_seq
    w2_rows = w2.shape[0] // n_tiles
    assert w2.shape[0] % (n_tiles * 16) == 0
    dmat, qdec, kdec, tile_decay = _decay_tables(RET_BLOCK, hd)
    half = hd // 2
    invf = (ROPE_BASE ** (-jnp.arange(half, dtype=F32) / half)).reshape(half, 1)

    def tile_at(t):
        return t // steps_per_seq, t % steps_per_seq

    def cur(j):
        return tile_at(jnp.minimum(j, n_tiles - 1))

    def nxt(j):
        return tile_at(jnp.minimum(j + 1, n_tiles - 1))

    def prev(j):
        return tile_at(jnp.maximum(j - 1, 0))

    const = _const_spec
    operands = [
        (x, pl.BlockSpec((None, tile, d), lambda j: (*cur(j), 0))),
        (x, pl.BlockSpec((None, tile, d), lambda j: (*nxt(j), 0))),
        (mod, const(mod.shape)),
        (positions, pl.BlockSpec((bsz, tile), lambda j: (0, cur(j)[1]))),
        (positions, pl.BlockSpec((bsz, tile), lambda j: (0, nxt(j)[1]))),
        (invf, const((half, 1))),
        (g_pre.reshape(1, d), const((1, d))),
        (g_post.reshape(1, d), const((1, d))),
        (w_in, const(w_in.shape)),
        (b_gate.reshape(1, N_BRANCHES * d), const((1, N_BRANCHES * d))),
        (w_pool, const(w_pool.shape)),
        (w_br, const(w_br.shape)),
        (w_out, const(w_out.shape)),
        (dmat, const(dmat.shape)),
        (qdec, const(qdec.shape)),
        (kdec, const(kdec.shape)),
        (g_ffn.reshape(1, d), const((1, d))),
        (w1, const(w1.shape)),
        (w2, pl.BlockSpec((w2_rows, w2.shape[1]), lambda j: (jnp.minimum(j, n_tiles - 1), 0))),
    ]
    staged = [
        pltpu.VMEM((RET_HEADS * (tile // RET_BLOCK), RET_BLOCK, RET_BLOCK), BF16),
        pltpu.VMEM((3, tile, ret_width), BF16),
        pltpu.VMEM((tile, pool_width), BF16),
        pltpu.VMEM((tile, ret_width), F32),
        pltpu.VMEM((tile, N_BRANCHES * d), F32),
    ]
    return pl.pallas_call(
        functools.partial(_mixer_kernel, tile=tile, steps_per_seq=steps_per_seq, n_tiles=n_tiles,
                          tile_decay=tile_decay),
        out_shape=[jax.ShapeDtypeStruct(x.shape, F32), jax.ShapeDtypeStruct((bsz, seq, d_ff), BF16),
                   jax.ShapeDtypeStruct(w2.shape, BF16)],
        grid=(n_tiles + 1,),
        in_specs=[spec for _, spec in operands],
        out_specs=[pl.BlockSpec((None, tile, d), lambda j: (*cur(j), 0)),
                   pl.BlockSpec((None, tile, d_ff), lambda j: (*prev(j), 0)),
                   pl.BlockSpec((w2_rows, w2.shape[1]), lambda j: (jnp.minimum(j, n_tiles - 1), 0))],
        scratch_shapes=[
            pltpu.VMEM((RET_HEADS, hd, hd), F32),
            pltpu.VMEM((POOL_HALO, pool_width), F32),
            pltpu.VMEM((tile, d), F32),
        ] + staged + staged,
        compiler_params=pltpu.CompilerParams(
            dimension_semantics=("arbitrary",),
            vmem_limit_bytes=VMEM_LIMIT_BYTES),
        name="mixer",
    )(*[a for a, _ in operands])


def _ffn_down_kernel(a_ref, x_ref, mod_ref, g_post_ref, w2_ref, o_ref, *, tiles_per_seq):
    gt_f = ModRows(mod_ref, pl.program_id(0) // tiles_per_seq)[5:6, :]
    y = jnp.dot(a_ref[...], w2_ref[...], preferred_element_type=F32)
    o_ref[...] = x_ref[...] + gt_f * (_rms(y) * g_post_ref[...])


def _ffn_down(a, x, mod, g_post, w2):
    bsz, seq, d = x.shape
    tile = FFN_TILE
    assert seq % tile == 0
    tiles_per_seq = seq // tile

    def cur(j):
        return j // tiles_per_seq, j % tiles_per_seq

    return pl.pallas_call(
        functools.partial(_ffn_down_kernel, tiles_per_seq=tiles_per_seq),
        out_shape=jax.ShapeDtypeStruct(x.shape, F32),
        grid=(bsz * tiles_per_seq,),
        in_specs=[
            pl.BlockSpec((None, tile, a.shape[2]), lambda j: (*cur(j), 0)),
            pl.BlockSpec((None, tile, d), lambda j: (*cur(j), 0)),
            _const_spec(mod.shape),
            _const_spec((1, d)),
            _const_spec(w2.shape),
        ],
        out_specs=pl.BlockSpec((None, tile, d), lambda j: (*cur(j), 0)),
        compiler_params=pltpu.CompilerParams(
            dimension_semantics=("arbitrary",),
            vmem_limit_bytes=VMEM_LIMIT_BYTES),
        name="ffn_down",
    )(a, x, mod, g_post.reshape(1, d), w2)


def kernel(x, c, positions, ada_w, ada_b, mix_pre_g, mix_post_g, ffn_pre_g, ffn_post_g, w_in,
           b_branch_gate, pool_w, pool_scale, w_branch_pool, w_branch_ret, w_out, w_ff1, w_ff2):
    depth = ada_w.shape[0]
    for l in range(depth):
        mod, w_pool, w_in_b, w_br_b, w_out_b, w1_b = _prep(
            c, ada_w[l], ada_b[l], pool_w[l], pool_scale[l], w_branch_pool[l],
            (w_in[l], w_branch_ret[l], w_out[l], w_ff1[l]))
        x, a, w2_b = _mixer(x, mod, positions, mix_pre_g[l], mix_post_g[l], w_in_b, b_branch_gate[l],
                            w_pool, w_br_b, w_out_b, ffn_pre_g[l], w1_b, w_ff2[l])
        x = _ffn_down(a, x, mod, ffn_post_g[l], w2_b)
    return x
```

```python
import functools
from typing import Any, NamedTuple

import jax
import jax.numpy as jnp
import numpy as np
from jax import lax
from jax.experimental import pallas as pl
from jax.experimental.pallas import tpu as pltpu

F32 = jnp.float32
BF16 = jnp.bfloat16

CHUNK = 64
POOL_WINDOWS = (2, 4, 8, 16)
POOL_HALO = 16
POOL_ROWS = 64
RET_HEADS = 4
N_BRANCHES = 2
N_MOD = 6
ROPE_BASE = 10000.0
EPS = 1e-6

LANES = 128
MIXER_TILE = 256
RET_BLOCK = 256
FFN_TILE = 1024
FFN_ROWS = 512
UP_CHUNK = 512
ADALN_BLOCK_N = 768
VMEM_LIMIT_BYTES = 56 * 1024 * 1024


def _rms(xf):
    return xf * lax.rsqrt(jnp.mean(xf * xf, axis=-1, keepdims=True) + EPS)


class ModRows:
    def __init__(self, ref, b):
        self.ref, self.b, self.d = ref, b, ref.shape[1] // N_MOD

    def __getitem__(self, idx):
        k = idx[0].start
        return self.ref[pl.ds(self.b, 1), k * self.d:(k + 1) * self.d]


def _prep_kernel(c_ref, w_ref, b_ref, pw_ref, scale_ref, wbp_ref, *refs):
    n_cast = (len(refs) - 2) // 2
    cast_in, (mod_ref, w_pool_ref), cast_out = refs[:n_cast], refs[n_cast:n_cast + 2], refs[n_cast + 2:]
    c = c_ref[...]
    a = (c * jax.nn.sigmoid(c)).astype(BF16)
    mod_ref[...] = jnp.dot(a, w_ref[...].astype(BF16), preferred_element_type=F32) + b_ref[...]
    w_pool_ref[...] = jnp.dot((pw_ref[...] * scale_ref[...]).astype(BF16), wbp_ref[...].astype(BF16),
                              preferred_element_type=F32).astype(BF16)
    for src, dst in zip(cast_in, cast_out):
        dst[...] = src[...].astype(BF16)


def _prep(c, w, b, pool_w, pool_scale, w_bp, to_cast):
    bsz, d = c.shape
    n = w.shape[1]
    n_steps = n // ADALN_BLOCK_N
    groups, gd, _ = pool_w.shape
    spg = n_steps // groups
    rows = gd // spg
    assert n_steps % groups == 0 and gd % spg == 0 and rows % 16 == 0
    assert all(m.shape[0] % (n_steps * 16) == 0 for m in to_cast)
    cast_specs = [pl.BlockSpec((m.shape[0] // n_steps, m.shape[1]), lambda j: (j, 0)) for m in to_cast]
    return pl.pallas_call(
        _prep_kernel,
        out_shape=[jax.ShapeDtypeStruct((bsz, n), F32), jax.ShapeDtypeStruct(w_bp.shape, BF16)]
                  + [jax.ShapeDtypeStruct(m.shape, BF16) for m in to_cast],
        grid=(n_steps,),
        in_specs=[
            pl.BlockSpec((bsz, d), lambda j: (0, 0)),
            pl.BlockSpec((d, ADALN_BLOCK_N), lambda j: (0, j)),
            pl.BlockSpec((1, ADALN_BLOCK_N), lambda j: (0, j)),
            pl.BlockSpec((None, rows, gd), lambda j: (j // spg, j % spg, 0)),
            pl.BlockSpec((1, gd), lambda j: (0, j // spg)),
            pl.BlockSpec((gd, w_bp.shape[1]), lambda j: (j // spg, 0)),
        ] + cast_specs,
        out_specs=[pl.BlockSpec((bsz, ADALN_BLOCK_N), lambda j: (0, j)),
                   pl.BlockSpec((rows, w_bp.shape[1]), lambda j: (j, 0))] + cast_specs,
        compiler_params=pltpu.CompilerParams(dimension_semantics=("arbitrary",)),
        name="prep",
    )(c, w, b.reshape(1, n), pool_w, pool_scale.reshape(1, groups * gd), w_bp, *to_cast)


class Staged(NamedTuple):
    probs: Any
    rot: Any
    pooled: Any
    sg: Any
    gates: Any


class StageParams(NamedTuple):
    g_pre: Any
    w_in: Any
    b_gate: Any
    invf: Any
    dmat: Any
    qdec: Any
    kdec: Any
    halo: Any


class MixParams(NamedTuple):
    g_post: Any
    w_pool: Any
    w_br: Any
    w_out: Any
    state: Any
    x1: Any


def _stage_phases(x_ref, rows, mod_ref, pos, seq_tile, first, sp: StageParams, out: Staged):
    tile = rows.stop - rows.start
    pool_width = out.pooled.shape[1]
    ret_width = out.sg.shape[1]
    d_model = out.gates.shape[1] // N_BRANCHES
    hd = ret_width // RET_HEADS
    sh_m, sc_m = mod_ref[0:1, :], mod_ref[1:2, :]
    h = ((_rms(x_ref[rows, :]) * sp.g_pre[...]) * (1.0 + sc_m) + sh_m).astype(BF16)
    yield

    def proj(lo, width):
        return jnp.dot(h, sp.w_in[:, lo:lo + width], preferred_element_type=F32)

    u = proj(0, pool_width)
    yield

    if first is True:
        halo = jnp.zeros(sp.halo.shape, F32)
    elif first is False:
        halo = sp.halo[...]
    else:
        halo = jnp.where(first, 0.0, sp.halo[...])
    sp.halo[...] = u[tile - POOL_HALO:tile, :]
    t_pos = (lax.broadcasted_iota(jnp.int32, (tile, LANES), 0) + (seq_tile * tile + 1)).astype(F32)
    inv_t = 1.0 / t_pos
    gd = pool_width // len(POOL_WINDOWS)
    for gi, w in enumerate(POOL_WINDOWS):
        cols = slice(gi * gd, (gi + 1) * gd)
        inv_cnt = jnp.where(t_pos < float(w), inv_t, 1.0 / w)
        for r0 in range(0, tile, POOL_ROWS):
            blk = slice(r0, r0 + POOL_ROWS)
            before = halo[:, cols] if r0 == 0 else u[r0 - POOL_HALO:r0, cols]
            s = jnp.concatenate([before, u[blk, cols]], axis=0)
            k = 1
            while k < w:
                s = s + pltpu.roll(s, k, 0)
                k *= 2
            out.pooled[blk, cols] = (s[POOL_HALO:] * inv_cnt[blk] - u[blk, cols]).astype(BF16)
    q = proj(pool_width, ret_width)
    yield
    k_all = proj(pool_width + ret_width, ret_width)
    yield

    ang = sp.invf[...] * pos.astype(F32)
    cos_h, sin_h = jnp.cos(ang), jnp.sin(ang)
    cos_t = jnp.transpose(jnp.concatenate([cos_h, cos_h], axis=0))
    sin_t = jnp.transpose(jnp.concatenate([-sin_h, sin_h], axis=0))
    blk = sp.dmat.shape[1]
    roped = []
    for hh in range(RET_HEADS):
        cols = slice(hh * hd, (hh + 1) * hd)
        qh, kh = q[:, cols], k_all[:, cols]
        qh = (qh * cos_t + pltpu.roll(qh, hd // 2, 1) * sin_t) * (hd ** -0.5)
        kh = kh * cos_t + pltpu.roll(kh, hd // 2, 1) * sin_t
        roped.append((qh.astype(BF16), kh.astype(BF16)))
        for r0 in range(0, tile, blk):
            out.rot[1, r0:r0 + blk, cols] = (qh[r0:r0 + blk] * sp.qdec[hh]).astype(BF16)
            out.rot[2, r0:r0 + blk, cols] = (kh[r0:r0 + blk] * sp.kdec[hh]).astype(BF16)
    out.rot[0] = proj(pool_width + 2 * ret_width, ret_width).astype(BF16)
    yield
    g = proj(pool_width + 3 * ret_width, ret_width)
    out.sg[...] = g * jax.nn.sigmoid(g)
    yield
    for hh, (qh, kh) in enumerate(roped):
        for bi, r0 in enumerate(range(0, tile, blk)):
            scores = lax.dot_general(qh[r0:r0 + blk], kh[r0:r0 + blk], (((1,), (1,)), ((), ())),
                                     preferred_element_type=F32)
            out.probs[hh * (tile // blk) + bi] = (scores * sp.dmat[hh]).astype(BF16)
    yield
    gate_lo = pool_width + 4 * ret_width
    for br in range(N_BRANCHES):
        cols = slice(br * d_model, (br + 1) * d_model)
        out.gates[:, cols] = jax.nn.sigmoid(proj(gate_lo + br * d_model, d_model) + sp.b_gate[:, cols])
        yield


def _mix_phases(st: Staged, x_ref, rows, o_ref, mod_ref, mp: MixParams, tile_decay):
    d_model = x_ref.shape[1]
    ret_width = st.sg.shape[1]
    hd = ret_width // RET_HEADS
    gt_m = mod_ref[2:3, :]

    tile = st.sg.shape[0]
    blk = st.probs.shape[1]
    n_blk = tile // blk
    gated = [[None] * RET_HEADS for _ in range(n_blk)]
    for bi in range(n_blk):
        blk_rows = slice(bi * blk, (bi + 1) * blk)
        for hh in range(RET_HEADS):
            cols = slice(hh * hd, (hh + 1) * hd)
            vh = st.rot[0, blk_rows, cols]
            o = jnp.dot(st.probs[hh * n_blk + bi], vh, preferred_element_type=F32)
            o = o + jnp.dot(st.rot[1, blk_rows, cols], mp.state[hh].astype(BF16),
                            preferred_element_type=F32)
            kv = lax.dot_general(st.rot[2, blk_rows, cols], vh, (((0,), (0,)), ((), ())),
                                 preferred_element_type=F32)
            mp.state[hh] = mp.state[hh] * tile_decay[hh] + kv
            mu = jnp.mean(o, axis=-1, keepdims=True)
            oc = o - mu
            var = jnp.mean(oc * oc, axis=-1, keepdims=True)
            gated[bi][hh] = (st.sg[blk_rows, cols] * (oc * lax.rsqrt(var + EPS))).astype(BF16)
            yield
    y_pool = jnp.dot(st.pooled[...], mp.w_pool[...], preferred_element_type=F32)
    yield
    gated = jnp.concatenate([jnp.concatenate(row, axis=-1) for row in gated], axis=0)
    y_ret = jnp.dot(gated, mp.w_br[...], preferred_element_type=F32)
    yield
    merged = (st.gates[:, 0:d_model] * y_pool + st.gates[:, d_model:2 * d_model] * y_ret).astype(BF16)
    y = jnp.dot(merged, mp.w_out[...], preferred_element_type=F32)
    yield
    x1 = x_ref[rows, :] + gt_m * (_rms(y) * mp.g_post[...])
    o_ref[rows, :] = x1
    mp.x1[...] = x1


def _up_phases(x1_ref, mod_ref, g_ref, w1_ref, a_ref):
    sh_f, sc_f = mod_ref[3:4, :], mod_ref[4:5, :]
    h = ((_rms(x1_ref[...]) * g_ref[...]) * (1.0 + sc_f) + sh_f).astype(BF16)
    yield
    for c0 in range(0, w1_ref.shape[1], UP_CHUNK):
        a = jnp.maximum(jnp.dot(h, w1_ref[:, c0:c0 + UP_CHUNK], preferred_element_type=F32), 0.0)
        a_ref[:, c0:c0 + UP_CHUNK] = (a * a).astype(BF16)
        yield


WEAVE = "musmmmsumussumussumussusum"


def _weave(order, **gens):
    for c in order:
        if c in gens:
            next(gens[c], None)
    for g in gens.values():
        for _ in g:
            pass


def _mixer_kernel(x_ref, xn_ref, mod_all_ref, pos_ref, posn_ref, invf_ref, g_pre_ref, g_post_ref,
                  w_in_ref, b_gate_ref, w_pool_ref, w_br_ref, w_out_ref, dmat_ref, qdec_ref, kdec_ref,
                  g_ffn_ref, w1_ref, wf2_ref, o_ref, a_ref, wf2_o_ref, state_ref, halo_ref, x1_ref, *staged,
                  tile, steps_per_seq, n_tiles, tile_decay):
    j = pl.program_id(0)
    wf2_o_ref[...] = wf2_ref[...].astype(BF16)
    last = n_tiles - 1

    def batch_and_tile(t):
        return t // steps_per_seq, t % steps_per_seq

    b, s = batch_and_tile(jnp.minimum(j, last))
    b_next, s_next = batch_and_tile(jnp.minimum(j + 1, last))
    b_prev, _ = batch_and_tile(jnp.maximum(j - 1, 0))
    mod_ref, modn_ref, modp_ref = (ModRows(mod_all_ref, r) for r in (b, b_next, b_prev))
    sp = StageParams(g_pre_ref, w_in_ref, b_gate_ref, invf_ref, dmat_ref, qdec_ref, kdec_ref, halo_ref)
    mp = MixParams(g_post_ref, w_pool_ref, w_br_ref, w_out_ref, state_ref, x1_ref)
    n_fields = len(Staged._fields)
    slots = (Staged(*staged[:n_fields]), Staged(*staged[n_fields:]))
    whole = slice(0, tile)

    @pl.when(j == 0)
    def _():
        for _ in _stage_phases(x_ref, whole, mod_ref, pos_ref[pl.ds(b, 1), :], 0, True, sp, slots[0]):
            pass
        x1_ref[...] = jnp.zeros_like(x1_ref)

    @pl.when((s == 0) & (j <= last))
    def _():
        state_ref[...] = jnp.zeros_like(state_ref)

    def mix(slot):
        return _mix_phases(slot, x_ref, whole, o_ref, mod_ref, mp, tile_decay)

    def up():
        return _up_phases(x1_ref, modp_ref, g_ffn_ref, w1_ref, a_ref)

    for parity in range(2):
        @pl.when((j % 2 == parity) & (j < last))
        def _():
            _weave(WEAVE, m=mix(slots[parity]), u=up(),
                   s=_stage_phases(xn_ref, whole, modn_ref, posn_ref[pl.ds(b_next, 1), :], s_next,
                                   s_next == 0, sp, slots[1 - parity]))

    @pl.when(j == last)
    def _():
        _weave(WEAVE, m=mix(slots[last % 2]), u=up())

    @pl.when(j == n_tiles)
    def _():
        _weave(WEAVE, u=up())


def _decay_tables(tile, hd):
    gamma = 1.0 - 2.0 ** (-5.0 - np.arange(RET_HEADS, dtype=np.float64))
    idx = np.arange(tile)
    dist = np.abs(idx[:, None] - idx[None, :])
    visible = (idx[None, :] // CHUNK) <= (idx[:, None] // CHUNK)
    dmat = np.where(visible[None], gamma[:, None, None] ** dist[None], 0.0)
    qdec = np.broadcast_to((gamma[:, None] ** (idx[None, :] + 1))[:, :, None], (RET_HEADS, tile, hd))
    kdec = np.broadcast_to((gamma[:, None] ** (tile - 1 - idx[None, :]))[:, :, None], (RET_HEADS, tile, hd))
    tile_decay = tuple(float(g) ** tile for g in gamma)
    as_f32 = lambda a: jnp.asarray(np.ascontiguousarray(a), dtype=F32)
    return as_f32(dmat), as_f32(qdec), as_f32(kdec), tile_decay


def _const_spec(shape):
    nd = len(shape)
    return pl.BlockSpec(shape, lambda j: (0,) * nd)


def _mixer(x, mod, positions, g_pre, g_post, w_in, b_gate, w_pool, w_br, w_out, g_ffn, w1, w2):
    bsz, seq, d = x.shape
    tile = MIXER_TILE
    pool_width = w_pool.shape[0]
    ret_width = w_br.shape[0]
    d_ff = w1.shape[1]
    hd = ret_width // RET_HEADS
    assert seq % tile == 0 and tile % RET_BLOCK == 0 and RET_BLOCK % CHUNK == 0 and hd == LANES
    assert d_ff % UP_CHUNK == 0
    steps_per_seq = seq // tile
    n_tiles = bsz * steps_per_seq
    w2_rows = w2.shape[0] // n_tiles
    assert w2.shape[0] % (n_tiles * 16) == 0
    dmat, qdec, kdec, tile_decay = _decay_tables(RET_BLOCK, hd)
    half = hd // 2
    invf = (ROPE_BASE ** (-jnp.arange(half, dtype=F32) / half)).reshape(half, 1)

    def tile_at(t):
        return t // steps_per_seq, t % steps_per_seq

    def cur(j):
        return tile_at(jnp.minimum(j, n_tiles - 1))

    def nxt(j):
        return tile_at(jnp.minimum(j + 1, n_tiles - 1))

    def prev(j):
        return tile_at(jnp.maximum(j - 1, 0))

    const = _const_spec
    operands = [
        (x, pl.BlockSpec((None, tile, d), lambda j: (*cur(j), 0))),
        (x, pl.BlockSpec((None, tile, d), lambda j: (*nxt(j), 0))),
        (mod, const(mod.shape)),
        (positions, pl.BlockSpec((bsz, tile), lambda j: (0, cur(j)[1]))),
        (positions, pl.BlockSpec((bsz, tile), lambda j: (0, nxt(j)[1]))),
        (invf, const((half, 1))),
        (g_pre.reshape(1, d), const((1, d))),
        (g_post.reshape(1, d), const((1, d))),
        (w_in, const(w_in.shape)),
        (b_gate.reshape(1, N_BRANCHES * d), const((1, N_BRANCHES * d))),
        (w_pool, const(w_pool.shape)),
        (w_br, const(w_br.shape)),
        (w_out, const(w_out.shape)),
        (dmat, const(dmat.shape)),
        (qdec, const(qdec.shape)),
        (kdec, const(kdec.shape)),
        (g_ffn.reshape(1, d), const((1, d))),
        (w1, const(w1.shape)),
        (w2, pl.BlockSpec((w2_rows, w2.shape[1]), lambda j: (jnp.minimum(j, n_tiles - 1), 0))),
    ]
    staged = [
        pltpu.VMEM((RET_HEADS * (tile // RET_BLOCK), RET_BLOCK, RET_BLOCK), BF16),
        pltpu.VMEM((3, tile, ret_width), BF16),
        pltpu.VMEM((tile, pool_width), BF16),
        pltpu.VMEM((tile, ret_width), F32),
        pltpu.VMEM((tile, N_BRANCHES * d), F32),
    ]
    return pl.pallas_call(
        functools.partial(_mixer_kernel, tile=tile, steps_per_seq=steps_per_seq, n_tiles=n_tiles,
                          tile_decay=tile_decay),
        out_shape=[jax.ShapeDtypeStruct(x.shape, F32), jax.ShapeDtypeStruct((bsz, seq, d_ff), BF16),
                   jax.ShapeDtypeStruct(w2.shape, BF16)],
        grid=(n_tiles + 1,),
        in_specs=[spec for _, spec in operands],
        out_specs=[pl.BlockSpec((None, tile, d), lambda j: (*cur(j), 0)),
                   pl.BlockSpec((None, tile, d_ff), lambda j: (*prev(j), 0)),
                   pl.BlockSpec((w2_rows, w2.shape[1]), lambda j: (jnp.minimum(j, n_tiles - 1), 0))],
        scratch_shapes=[
            pltpu.VMEM((RET_HEADS, hd, hd), F32),
            pltpu.VMEM((POOL_HALO, pool_width), F32),
            pltpu.VMEM((tile, d), F32),
        ] + staged + staged,
        compiler_params=pltpu.CompilerParams(
            dimension_semantics=("arbitrary",),
            vmem_limit_bytes=VMEM_LIMIT_BYTES),
        name="mixer",
    )(*[a for a, _ in operands])


def _ffn_down_kernel(a_ref, x_ref, mod_ref, g_post_ref, w2_ref, o_ref, *, tiles_per_seq):
    gt_f = ModRows(mod_ref, pl.program_id(0) // tiles_per_seq)[5:6, :]
    for r0 in range(0, a_ref.shape[0], FFN_ROWS):
        rows = slice(r0, r0 + FFN_ROWS)
        y = jnp.dot(a_ref[rows, :], w2_ref[...], preferred_element_type=F32)
        o_ref[rows, :] = x_ref[rows, :] + gt_f * (_rms(y) * g_post_ref[...])


def _ffn_down(a, x, mod, g_post, w2):
    bsz, seq, d = x.shape
    tile = FFN_TILE
    assert seq % tile == 0
    tiles_per_seq = seq // tile

    def cur(j):
        return j // tiles_per_seq, j % tiles_per_seq

    return pl.pallas_call(
        functools.partial(_ffn_down_kernel, tiles_per_seq=tiles_per_seq),
        out_shape=jax.ShapeDtypeStruct(x.shape, F32),
        grid=(bsz * tiles_per_seq,),
        in_specs=[
            pl.BlockSpec((None, tile, a.shape[2]), lambda j: (*cur(j), 0)),
            pl.BlockSpec((None, tile, d), lambda j: (*cur(j), 0)),
            _const_spec(mod.shape),
            _const_spec((1, d)),
            _const_spec(w2.shape),
        ],
        out_specs=pl.BlockSpec((None, tile, d), lambda j: (*cur(j), 0)),
        compiler_params=pltpu.CompilerParams(
            dimension_semantics=("arbitrary",),
            vmem_limit_bytes=VMEM_LIMIT_BYTES),
        name="ffn_down",
    )(a, x, mod, g_post.reshape(1, d), w2)


def kernel(x, c, positions, ada_w, ada_b, mix_pre_g, mix_post_g, ffn_pre_g, ffn_post_g, w_in,
           b_branch_gate, pool_w, pool_scale, w_branch_pool, w_branch_ret, w_out, w_ff1, w_ff2):
    depth = ada_w.shape[0]
    for l in range(depth):
        mod, w_pool, w_in_b, w_br_b, w_out_b, w1_b = _prep(
            c, ada_w[l], ada_b[l], pool_w[l], pool_scale[l], w_branch_pool[l],
            (w_in[l], w_branch_ret[l], w_out[l], w_ff1[l]))
        x, a, w2_b = _mixer(x, mod, positions, mix_pre_g[l], mix_post_g[l], w_in_b, b_branch_gate[l],
                            w_pool, w_br_b, w_out_b, ffn_pre_g[l], w1_b, w_ff2[l])
        x = _ffn_down(a, x, mod, ffn_post_g[l], w2_b)
    return x
```

```python
import functools
from typing import Any, NamedTuple

import jax
import jax.numpy as jnp
import numpy as np
from jax import lax
from jax.experimental import pallas as pl
from jax.experimental.pallas import tpu as pltpu

F32 = jnp.float32
BF16 = jnp.bfloat16

CHUNK = 64
POOL_WINDOWS = (2, 4, 8, 16)
POOL_HALO = 16
POOL_ROWS = 64
RET_HEADS = 4
N_BRANCHES = 2
N_MOD = 6
ROPE_BASE = 10000.0
EPS = 1e-6

LANES = 128
MIXER_TILE = 256
RET_BLOCK = 256
FFN_TILE = 1024
FFN_ROWS = 512
UP_CHUNK = 512
ADALN_BLOCK_N = 384
VMEM_LIMIT_BYTES = 56 * 1024 * 1024


def _rms(xf):
    return xf * lax.rsqrt(jnp.mean(xf * xf, axis=-1, keepdims=True) + EPS)


class ModRows:
    def __init__(self, ref, b):
        self.ref, self.b, self.d = ref, b, ref.shape[1] // N_MOD

    def __getitem__(self, idx):
        k = idx[0].start
        return self.ref[pl.ds(self.b, 1), k * self.d:(k + 1) * self.d]


def _prep_kernel(c_ref, w_ref, b_ref, pw_ref, scale_ref, wbp_ref, *refs):
    n_cast = (len(refs) - 2) // 2
    cast_in, (mod_ref, w_pool_ref), cast_out = refs[:n_cast], refs[n_cast:n_cast + 2], refs[n_cast + 2:]
    c = c_ref[...]
    a = (c * jax.nn.sigmoid(c)).astype(BF16)
    mod_ref[...] = jnp.dot(a, w_ref[...].astype(BF16), preferred_element_type=F32) + b_ref[...]
    w_pool_ref[...] = jnp.dot((pw_ref[...] * scale_ref[...]).astype(BF16), wbp_ref[...].astype(BF16),
                              preferred_element_type=F32).astype(BF16)
    for src, dst in zip(cast_in, cast_out):
        dst[...] = src[...].astype(BF16)


def _prep(c, w, b, pool_w, pool_scale, w_bp, to_cast):
    bsz, d = c.shape
    n = w.shape[1]
    n_steps = n // ADALN_BLOCK_N
    groups, gd, _ = pool_w.shape
    spg = n_steps // groups
    rows = gd // spg
    assert n_steps % groups == 0 and gd % spg == 0 and rows % 16 == 0
    assert all(m.shape[0] % (n_steps * 16) == 0 for m in to_cast)
    cast_specs = [pl.BlockSpec((m.shape[0] // n_steps, m.shape[1]), lambda j: (j, 0)) for m in to_cast]
    return pl.pallas_call(
        _prep_kernel,
        out_shape=[jax.ShapeDtypeStruct((bsz, n), F32), jax.ShapeDtypeStruct(w_bp.shape, BF16)]
                  + [jax.ShapeDtypeStruct(m.shape, BF16) for m in to_cast],
        grid=(n_steps,),
        in_specs=[
            pl.BlockSpec((bsz, d), lambda j: (0, 0)),
            pl.BlockSpec((d, ADALN_BLOCK_N), lambda j: (0, j)),
            pl.BlockSpec((1, ADALN_BLOCK_N), lambda j: (0, j)),
            pl.BlockSpec((None, rows, gd), lambda j: (j // spg, j % spg, 0)),
            pl.BlockSpec((1, gd), lambda j: (0, j // spg)),
            pl.BlockSpec((gd, w_bp.shape[1]), lambda j: (j // spg, 0)),
        ] + cast_specs,
        out_specs=[pl.BlockSpec((bsz, ADALN_BLOCK_N), lambda j: (0, j)),
                   pl.BlockSpec((rows, w_bp.shape[1]), lambda j: (j, 0))] + cast_specs,
        compiler_params=pltpu.CompilerParams(dimension_semantics=("arbitrary",)),
        name="prep",
    )(c, w, b.reshape(1, n), pool_w, pool_scale.reshape(1, groups * gd), w_bp, *to_cast)


class Staged(NamedTuple):
    probs: Any
    rot: Any
    pooled: Any
    sg: Any
    gates: Any


class StageParams(NamedTuple):
    g_pre: Any
    w_in: Any
    b_gate: Any
    invf: Any
    dmat: Any
    qdec: Any
    kdec: Any
    halo: Any


class MixParams(NamedTuple):
    g_post: Any
    w_pool: Any
    w_br: Any
    w_out: Any
    state: Any
    x1: Any


def _stage_phases(x_ref, rows, mod_ref, pos, seq_tile, first, sp: StageParams, out: Staged):
    tile = rows.stop - rows.start
    pool_width = out.pooled.shape[1]
    ret_width = out.sg.shape[1]
    d_model = out.gates.shape[1] // N_BRANCHES
    hd = ret_width // RET_HEADS
    sh_m, sc_m = mod_ref[0:1, :], mod_ref[1:2, :]
    h = ((_rms(x_ref[rows, :]) * sp.g_pre[...]) * (1.0 + sc_m) + sh_m).astype(BF16)
    yield

    def proj(lo, width):
        return jnp.dot(h, sp.w_in[:, lo:lo + width], preferred_element_type=F32)

    u = proj(0, pool_width)
    yield

    if first is True:
        halo = jnp.zeros(sp.halo.shape, F32)
    elif first is False:
        halo = sp.halo[...]
    else:
        halo = jnp.where(first, 0.0, sp.halo[...])
    sp.halo[...] = u[tile - POOL_HALO:tile, :]
    t_pos = (lax.broadcasted_iota(jnp.int32, (tile, LANES), 0) + (seq_tile * tile + 1)).astype(F32)
    inv_t = 1.0 / t_pos
    gd = pool_width // len(POOL_WINDOWS)
    for gi, w in enumerate(POOL_WINDOWS):
        cols = slice(gi * gd, (gi + 1) * gd)
        inv_cnt = jnp.where(t_pos < float(w), inv_t, 1.0 / w)
        for r0 in range(0, tile, POOL_ROWS):
            blk = slice(r0, r0 + POOL_ROWS)
            before = halo[:, cols] if r0 == 0 else u[r0 - POOL_HALO:r0, cols]
            s = jnp.concatenate([before, u[blk, cols]], axis=0)
            k = 1
            while k < w:
                s = s + pltpu.roll(s, k, 0)
                k *= 2
            out.pooled[blk, cols] = (s[POOL_HALO:] * inv_cnt[blk] - u[blk, cols]).astype(BF16)
    q = proj(pool_width, ret_width)
    yield
    k_all = proj(pool_width + ret_width, ret_width)
    yield

    ang = sp.invf[...] * pos.astype(F32)
    cos_h, sin_h = jnp.cos(ang), jnp.sin(ang)
    cos_t = jnp.transpose(jnp.concatenate([cos_h, cos_h], axis=0))
    sin_t = jnp.transpose(jnp.concatenate([-sin_h, sin_h], axis=0))
    blk = sp.dmat.shape[1]
    roped = []
    for hh in range(RET_HEADS):
        cols = slice(hh * hd, (hh + 1) * hd)
        qh, kh = q[:, cols], k_all[:, cols]
        qh = (qh * cos_t + pltpu.roll(qh, hd // 2, 1) * sin_t) * (hd ** -0.5)
        kh = kh * cos_t + pltpu.roll(kh, hd // 2, 1) * sin_t
        roped.append((qh.astype(BF16), kh.astype(BF16)))
        for r0 in range(0, tile, blk):
            out.rot[1, r0:r0 + blk, cols] = (qh[r0:r0 + blk] * sp.qdec[hh]).astype(BF16)
            out.rot[2, r0:r0 + blk, cols] = (kh[r0:r0 + blk] * sp.kdec[hh]).astype(BF16)
    out.rot[0] = proj(pool_width + 2 * ret_width, ret_width).astype(BF16)
    yield
    g = proj(pool_width + 3 * ret_width, ret_width)
    out.sg[...] = g * jax.nn.sigmoid(g)
    yield
    for hh, (qh, kh) in enumerate(roped):
        for bi, r0 in enumerate(range(0, tile, blk)):
            scores = lax.dot_general(qh[r0:r0 + blk], kh[r0:r0 + blk], (((1,), (1,)), ((), ())),
                                     preferred_element_type=F32)
            out.probs[hh * (tile // blk) + bi] = (scores * sp.dmat[hh]).astype(BF16)
    yield
    gate_lo = pool_width + 4 * ret_width
    for br in range(N_BRANCHES):
        cols = slice(br * d_model, (br + 1) * d_model)
        out.gates[:, cols] = jax.nn.sigmoid(proj(gate_lo + br * d_model, d_model) + sp.b_gate[:, cols])
        yield


def _mix_phases(st: Staged, x_ref, rows, o_ref, mod_ref, mp: MixParams, tile_decay):
    d_model = x_ref.shape[1]
    ret_width = st.sg.shape[1]
    hd = ret_width // RET_HEADS
    gt_m = mod_ref[2:3, :]

    tile = st.sg.shape[0]
    blk = st.probs.shape[1]
    n_blk = tile // blk
    gated = [[None] * RET_HEADS for _ in range(n_blk)]
    for bi in range(n_blk):
        blk_rows = slice(bi * blk, (bi + 1) * blk)
        for hh in range(RET_HEADS):
            cols = slice(hh * hd, (hh + 1) * hd)
            vh = st.rot[0, blk_rows, cols]
            o = jnp.dot(st.probs[hh * n_blk + bi], vh, preferred_element_type=F32)
            o = o + jnp.dot(st.rot[1, blk_rows, cols], mp.state[hh].astype(BF16),
                            preferred_element_type=F32)
            kv = lax.dot_general(st.rot[2, blk_rows, cols], vh, (((0,), (0,)), ((), ())),
                                 preferred_element_type=F32)
            mp.state[hh] = mp.state[hh] * tile_decay[hh] + kv
            mu = jnp.mean(o, axis=-1, keepdims=True)
            oc = o - mu
            var = jnp.mean(oc * oc, axis=-1, keepdims=True)
            gated[bi][hh] = (st.sg[blk_rows, cols] * (oc * lax.rsqrt(var + EPS))).astype(BF16)
            yield
    y_pool = jnp.dot(st.pooled[...], mp.w_pool[...], preferred_element_type=F32)
    yield
    gated = jnp.concatenate([jnp.concatenate(row, axis=-1) for row in gated], axis=0)
    y_ret = jnp.dot(gated, mp.w_br[...], preferred_element_type=F32)
    yield
    merged = (st.gates[:, 0:d_model] * y_pool + st.gates[:, d_model:2 * d_model] * y_ret).astype(BF16)
    y = jnp.dot(merged, mp.w_out[...], preferred_element_type=F32)
    yield
    x1 = x_ref[rows, :] + gt_m * (_rms(y) * mp.g_post[...])
    o_ref[rows, :] = x1
    mp.x1[...] = x1


def _up_phases(x1_ref, mod_ref, g_ref, w1_ref, a_ref):
    sh_f, sc_f = mod_ref[3:4, :], mod_ref[4:5, :]
    h = ((_rms(x1_ref[...]) * g_ref[...]) * (1.0 + sc_f) + sh_f).astype(BF16)
    yield
    for c0 in range(0, w1_ref.shape[1], UP_CHUNK):
        a = jnp.maximum(jnp.dot(h, w1_ref[:, c0:c0 + UP_CHUNK], preferred_element_type=F32), 0.0)
        a_ref[:, c0:c0 + UP_CHUNK] = (a * a).astype(BF16)
        yield


WEAVE = "musmmmsumussumussumussusum"


def _weave(order, **gens):
    for c in order:
        if c in gens:
            next(gens[c], None)
    for g in gens.values():
        for _ in g:
            pass


def _mixer_kernel(x_ref, xn_ref, mod_all_ref, pos_ref, posn_ref, invf_ref, g_pre_ref, g_post_ref,
                  w_in_ref, b_gate_ref, w_pool_ref, w_br_ref, w_out_ref, dmat_ref, qdec_ref, kdec_ref,
                  g_ffn_ref, w1_ref, wf2_ref, o_ref, a_ref, wf2_o_ref, state_ref, halo_ref, x1_ref, *staged,
                  tile, steps_per_seq, n_tiles, tile_decay):
    j = pl.program_id(0)
    wf2_o_ref[...] = wf2_ref[...].astype(BF16)
    last = n_tiles - 1

    def batch_and_tile(t):
        return t // steps_per_seq, t % steps_per_seq

    b, s = batch_and_tile(jnp.minimum(j, last))
    b_next, s_next = batch_and_tile(jnp.minimum(j + 1, last))
    b_prev, _ = batch_and_tile(jnp.maximum(j - 1, 0))
    mod_ref, modn_ref, modp_ref = (ModRows(mod_all_ref, r) for r in (b, b_next, b_prev))
    sp = StageParams(g_pre_ref, w_in_ref, b_gate_ref, invf_ref, dmat_ref, qdec_ref, kdec_ref, halo_ref)
    mp = MixParams(g_post_ref, w_pool_ref, w_br_ref, w_out_ref, state_ref, x1_ref)
    n_fields = len(Staged._fields)
    slots = (Staged(*staged[:n_fields]), Staged(*staged[n_fields:]))
    whole = slice(0, tile)

    @pl.when(j == 0)
    def _():
        for _ in _stage_phases(x_ref, whole, mod_ref, pos_ref[pl.ds(b, 1), :], 0, True, sp, slots[0]):
            pass
        x1_ref[...] = jnp.zeros_like(x1_ref)

    @pl.when((s == 0) & (j <= last))
    def _():
        state_ref[...] = jnp.zeros_like(state_ref)

    def mix(slot):
        return _mix_phases(slot, x_ref, whole, o_ref, mod_ref, mp, tile_decay)

    def up():
        return _up_phases(x1_ref, modp_ref, g_ffn_ref, w1_ref, a_ref)

    for parity in range(2):
        @pl.when((j % 2 == parity) & (j < last))
        def _():
            _weave(WEAVE, m=mix(slots[parity]), u=up(),
                   s=_stage_phases(xn_ref, whole, modn_ref, posn_ref[pl.ds(b_next, 1), :], s_next,
                                   s_next == 0, sp, slots[1 - parity]))

    @pl.when(j == last)
    def _():
        _weave(WEAVE, m=mix(slots[last % 2]), u=up())

    @pl.when(j == n_tiles)
    def _():
        _weave(WEAVE, u=up())


def _decay_tables(tile, hd):
    gamma = 1.0 - 2.0 ** (-5.0 - np.arange(RET_HEADS, dtype=np.float64))
    idx = np.arange(tile)
    dist = np.abs(idx[:, None] - idx[None, :])
    visible = (idx[None, :] // CHUNK) <= (idx[:, None] // CHUNK)
    dmat = np.where(visible[None], gamma[:, None, None] ** dist[None], 0.0)
    qdec = np.broadcast_to((gamma[:, None] ** (idx[None, :] + 1))[:, :, None], (RET_HEADS, tile, hd))
    kdec = np.broadcast_to((gamma[:, None] ** (tile - 1 - idx[None, :]))[:, :, None], (RET_HEADS, tile, hd))
    tile_decay = tuple(float(g) ** tile for g in gamma)
    as_f32 = lambda a: jnp.asarray(np.ascontiguousarray(a), dtype=F32)
    return as_f32(dmat), as_f32(qdec), as_f32(kdec), tile_decay


def _const_spec(shape):
    nd = len(shape)
    return pl.BlockSpec(shape, lambda j: (0,) * nd)


def _mixer(x, mod, positions, g_pre, g_post, w_in, b_gate, w_pool, w_br, w_out, g_ffn, w1, w2):
    bsz, seq, d = x.shape
    tile = MIXER_TILE
    pool_width = w_pool.shape[0]
    ret_width = w_br.shape[0]
    d_ff = w1.shape[1]
    hd = ret_width // RET_HEADS
    assert seq % tile == 0 and tile % RET_BLOCK == 0 and RET_BLOCK % CHUNK == 0 and hd == LANES
    assert d_ff % UP_CHUNK == 0
    steps_per_seq = seq // tile
    n_tiles = bsz * steps_per_seq
    w2_rows = w2.shape[0] // n_tiles
    assert w2.shape[0] % (n_tiles * 16) == 0
    dmat, qdec, kdec, tile_decay = _decay_tables(RET_BLOCK, hd)
    half = hd // 2
    invf = (ROPE_BASE ** (-jnp.arange(half, dtype=F32) / half)).reshape(half, 1)

    def tile_at(t):
        return t // steps_per_seq, t % steps_per_seq

    def cur(j):
        return tile_at(jnp.minimum(j, n_tiles - 1))

    def nxt(j):
        return tile_at(jnp.minimum(j + 1, n_tiles - 1))

    def prev(j):
        return tile_at(jnp.maximum(j - 1, 0))

    const = _const_spec
    operands = [
        (x, pl.BlockSpec((None, tile, d), lambda j: (*cur(j), 0))),
        (x, pl.BlockSpec((None, tile, d), lambda j: (*nxt(j), 0))),
        (mod, const(mod.shape)),
        (positions, pl.BlockSpec((bsz, tile), lambda j: (0, cur(j)[1]))),
        (positions, pl.BlockSpec((bsz, tile), lambda j: (0, nxt(j)[1]))),
        (invf, const((half, 1))),
        (g_pre.reshape(1, d), const((1, d))),
        (g_post.reshape(1, d), const((1, d))),
        (w_in, const(w_in.shape)),
        (b_gate.reshape(1, N_BRANCHES * d), const((1, N_BRANCHES * d))),
        (w_pool, const(w_pool.shape)),
        (w_br, const(w_br.shape)),
        (w_out, const(w_out.shape)),
        (dmat, const(dmat.shape)),
        (qdec, const(qdec.shape)),
        (kdec, const(kdec.shape)),
        (g_ffn.reshape(1, d), const((1, d))),
        (w1, const(w1.shape)),
        (w2, pl.BlockSpec((w2_rows, w2.shape[1]), lambda j: (jnp.minimum(j, n_tiles - 1), 0))),
    ]
    staged = [
        pltpu.VMEM((RET_HEADS * (tile // RET_BLOCK), RET_BLOCK, RET_BLOCK), BF16),
        pltpu.VMEM((3, tile, ret_width), BF16),
        pltpu.VMEM((tile, pool_width), BF16),
        pltpu.VMEM((tile, ret_width), F32),
        pltpu.VMEM((tile, N_BRANCHES * d), F32),
    ]
    return pl.pallas_call(
        functools.partial(_mixer_kernel, tile=tile, steps_per_seq=steps_per_seq, n_tiles=n_tiles,
                          tile_decay=tile_decay),
        out_shape=[jax.ShapeDtypeStruct(x.shape, F32), jax.ShapeDtypeStruct((bsz, seq, d_ff), BF16),
                   jax.ShapeDtypeStruct(w2.shape, BF16)],
        grid=(n_tiles + 1,),
        in_specs=[spec for _, spec in operands],
        out_specs=[pl.BlockSpec((None, tile, d), lambda j: (*cur(j), 0)),
                   pl.BlockSpec((None, tile, d_ff), lambda j: (*prev(j), 0)),
                   pl.BlockSpec((w2_rows, w2.shape[1]), lambda j: (jnp.minimum(j, n_tiles - 1), 0))],
        scratch_shapes=[
            pltpu.VMEM((RET_HEADS, hd, hd), F32),
            pltpu.VMEM((POOL_HALO, pool_width), F32),
            pltpu.VMEM((tile, d), F32),
        ] + staged + staged,
        compiler_params=pltpu.CompilerParams(
            dimension_semantics=("arbitrary",),
            vmem_limit_bytes=VMEM_LIMIT_BYTES),
        name="mixer",
    )(*[a for a, _ in operands])


def _ffn_down_kernel(a_ref, x_ref, mod_ref, g_post_ref, w2_ref, o_ref, *, tiles_per_seq):
    gt_f = ModRows(mod_ref, pl.program_id(0) // tiles_per_seq)[5:6, :]
    for r0 in range(0, a_ref.shape[0], FFN_ROWS):
        rows = slice(r0, r0 + FFN_ROWS)
        y = jnp.dot(a_ref[rows, :], w2_ref[...], preferred_element_type=F32)
        o_ref[rows, :] = x_ref[rows, :] + gt_f * (_rms(y) * g_post_ref[...])


def _ffn_down(a, x, mod, g_post, w2):
    bsz, seq, d = x.shape
    tile = FFN_TILE
    assert seq % tile == 0
    tiles_per_seq = seq // tile

    def cur(j):
        return j // tiles_per_seq, j % tiles_per_seq

    return pl.pallas_call(
        functools.partial(_ffn_down_kernel, tiles_per_seq=tiles_per_seq),
        out_shape=jax.ShapeDtypeStruct(x.shape, F32),
        grid=(bsz * tiles_per_seq,),
        in_specs=[
            pl.BlockSpec((None, tile, a.shape[2]), lambda j: (*cur(j), 0)),
            pl.BlockSpec((None, tile, d), lambda j: (*cur(j), 0)),
            _const_spec(mod.shape),
            _const_spec((1, d)),
            _const_spec(w2.shape),
        ],
        out_specs=pl.BlockSpec((None, tile, d), lambda j: (*cur(j), 0)),
        compiler_params=pltpu.CompilerParams(
            dimension_semantics=("arbitrary",),
            vmem_limit_bytes=VMEM_LIMIT_BYTES),
        name="ffn_down",
    )(a, x, mod, g_post.reshape(1, d), w2)


def kernel(x, c, positions, ada_w, ada_b, mix_pre_g, mix_post_g, ffn_pre_g, ffn_post_g, w_in,
           b_branch_gate, pool_w, pool_scale, w_branch_pool, w_branch_ret, w_out, w_ff1, w_ff2):
    depth = ada_w.shape[0]
    for l in range(depth):
        mod, w_pool, w_in_b, w_br_b, w_out_b, w1_b = _prep(
            c, ada_w[l], ada_b[l], pool_w[l], pool_scale[l], w_branch_pool[l],
            (w_in[l], w_branch_ret[l], w_out[l], w_ff1[l]))
        x, a, w2_b = _mixer(x, mod, positions, mix_pre_g[l], mix_post_g[l], w_in_b, b_branch_gate[l],
                            w_pool, w_br_b, w_out_b, ffn_pre_g[l], w1_b, w_ff2[l])
        x = _ffn_down(a, x, mod, ffn_post_g[l], w2_b)
    return x
```

```python
import functools
from typing import Any, NamedTuple

import jax
import jax.numpy as jnp
import numpy as np
from jax import lax
from jax.experimental import pallas as pl
from jax.experimental.pallas import tpu as pltpu

F32 = jnp.float32
BF16 = jnp.bfloat16

CHUNK = 64
POOL_WINDOWS = (2, 4, 8, 16)
POOL_HALO = 16
POOL_ROWS = 64
RET_HEADS = 4
N_BRANCHES = 2
N_MOD = 6
ROPE_BASE = 10000.0
EPS = 1e-6

LANES = 128
MIXER_TILE = 256
RET_BLOCK = 256
FFN_TILE = 1024
FFN_ROWS = 512
UP_CHUNK = 512
ADALN_BLOCK_N = 768
VMEM_LIMIT_BYTES = 56 * 1024 * 1024


def _rms(xf):
    return xf * lax.rsqrt(jnp.mean(xf * xf, axis=-1, keepdims=True) + EPS)


class ModRows:
    def __init__(self, ref, b):
        self.ref, self.b, self.d = ref, b, ref.shape[1] // N_MOD

    def __getitem__(self, idx):
        k = idx[0].start
        return self.ref[pl.ds(self.b, 1), k * self.d:(k + 1) * self.d]


def _prep_kernel(c_ref, w_ref, b_ref, pw_ref, scale_ref, wbp_ref, *refs):
    n_cast = (len(refs) - 2) // 2
    cast_in, (mod_ref, w_pool_ref), cast_out = refs[:n_cast], refs[n_cast:n_cast + 2], refs[n_cast + 2:]
    c = c_ref[...]
    a = (c * jax.nn.sigmoid(c)).astype(BF16)
    mod_ref[...] = jnp.dot(a, w_ref[...].astype(BF16), preferred_element_type=F32) + b_ref[...]
    w_pool_ref[...] = jnp.dot((pw_ref[...] * scale_ref[...]).astype(BF16), wbp_ref[...].astype(BF16),
                              preferred_element_type=F32).astype(BF16)
    for src, dst in zip(cast_in, cast_out):
        dst[...] = src[...].astype(BF16)


def _prep(c, w, b, pool_w, pool_scale, w_bp, to_cast):
    bsz, d = c.shape
    n = w.shape[1]
    n_steps = n // ADALN_BLOCK_N
    groups, gd, _ = pool_w.shape
    spg = n_steps // groups
    rows = gd // spg
    assert n_steps % groups == 0 and gd % spg == 0 and rows % 16 == 0
    assert all(m.shape[0] % (n_steps * 16) == 0 for m in to_cast)
    cast_specs = [pl.BlockSpec((m.shape[0] // n_steps, m.shape[1]), lambda j: (j, 0)) for m in to_cast]
    return pl.pallas_call(
        _prep_kernel,
        out_shape=[jax.ShapeDtypeStruct((bsz, n), F32), jax.ShapeDtypeStruct(w_bp.shape, BF16)]
                  + [jax.ShapeDtypeStruct(m.shape, BF16) for m in to_cast],
        grid=(n_steps,),
        in_specs=[
            pl.BlockSpec((bsz, d), lambda j: (0, 0)),
            pl.BlockSpec((d, ADALN_BLOCK_N), lambda j: (0, j)),
            pl.BlockSpec((1, ADALN_BLOCK_N), lambda j: (0, j)),
            pl.BlockSpec((None, rows, gd), lambda j: (j // spg, j % spg, 0)),
            pl.BlockSpec((1, gd), lambda j: (0, j // spg)),
            pl.BlockSpec((gd, w_bp.shape[1]), lambda j: (j // spg, 0)),
        ] + cast_specs,
        out_specs=[pl.BlockSpec((bsz, ADALN_BLOCK_N), lambda j: (0, j)),
                   pl.BlockSpec((rows, w_bp.shape[1]), lambda j: (j, 0))] + cast_specs,
        compiler_params=pltpu.CompilerParams(dimension_semantics=("arbitrary",)),
        name="prep",
    )(c, w, b.reshape(1, n), pool_w, pool_scale.reshape(1, groups * gd), w_bp, *to_cast)


class Staged(NamedTuple):
    probs: Any
    rot: Any
    pooled: Any
    sg: Any
    gates: Any


class StageParams(NamedTuple):
    g_pre: Any
    w_in: Any
    b_gate: Any
    invf: Any
    dmat: Any
    qdec: Any
    kdec: Any
    halo: Any


class MixParams(NamedTuple):
    g_post: Any
    w_pool: Any
    w_br: Any
    w_out: Any
    state: Any
    x1: Any


def _stage_phases(x_ref, rows, mod_ref, pos, seq_tile, first, sp: StageParams, out: Staged):
    tile = rows.stop - rows.start
    pool_width = out.pooled.shape[1]
    ret_width = out.sg.shape[1]
    d_model = out.gates.shape[1] // N_BRANCHES
    hd = ret_width // RET_HEADS
    sh_m, sc_m = mod_ref[0:1, :], mod_ref[1:2, :]
    h = ((_rms(x_ref[rows, :]) * sp.g_pre[...]) * (1.0 + sc_m) + sh_m).astype(BF16)
    yield

    def proj(lo, width):
        return jnp.dot(h, sp.w_in[:, lo:lo + width], preferred_element_type=F32)

    u = proj(0, pool_width)
    yield

    if first is True:
        halo = jnp.zeros(sp.halo.shape, F32)
    elif first is False:
        halo = sp.halo[...]
    else:
        halo = jnp.where(first, 0.0, sp.halo[...])
    sp.halo[...] = u[tile - POOL_HALO:tile, :]
    t_pos = (lax.broadcasted_iota(jnp.int32, (tile, LANES), 0) + (seq_tile * tile + 1)).astype(F32)
    inv_t = 1.0 / t_pos
    gd = pool_width // len(POOL_WINDOWS)
    for gi, w in enumerate(POOL_WINDOWS):
        cols = slice(gi * gd, (gi + 1) * gd)
        inv_cnt = jnp.where(t_pos < float(w), inv_t, 1.0 / w)
        for r0 in range(0, tile, POOL_ROWS):
            blk = slice(r0, r0 + POOL_ROWS)
            before = halo[:, cols] if r0 == 0 else u[r0 - POOL_HALO:r0, cols]
            s = jnp.concatenate([before, u[blk, cols]], axis=0)
            k = 1
            while k < w:
                s = s + pltpu.roll(s, k, 0)
                k *= 2
            out.pooled[blk, cols] = (s[POOL_HALO:] * inv_cnt[blk] - u[blk, cols]).astype(BF16)
    q = proj(pool_width, ret_width)
    yield
    k_all = proj(pool_width + ret_width, ret_width)
    yield

    ang = sp.invf[...] * pos.astype(F32)
    cos_h, sin_h = jnp.cos(ang), jnp.sin(ang)
    cos_t = jnp.transpose(jnp.concatenate([cos_h, cos_h], axis=0))
    sin_t = jnp.transpose(jnp.concatenate([-sin_h, sin_h], axis=0))
    blk = sp.dmat.shape[1]
    roped = []
    for hh in range(RET_HEADS):
        cols = slice(hh * hd, (hh + 1) * hd)
        qh, kh = q[:, cols], k_all[:, cols]
        qh = (qh * cos_t + pltpu.roll(qh, hd // 2, 1) * sin_t) * (hd ** -0.5)
        kh = kh * cos_t + pltpu.roll(kh, hd // 2, 1) * sin_t
        roped.append((qh.astype(BF16), kh.astype(BF16)))
        for r0 in range(0, tile, blk):
            out.rot[1, r0:r0 + blk, cols] = (qh[r0:r0 + blk] * sp.qdec[hh]).astype(BF16)
            out.rot[2, r0:r0 + blk, cols] = (kh[r0:r0 + blk] * sp.kdec[hh]).astype(BF16)
    out.rot[0] = proj(pool_width + 2 * ret_width, ret_width).astype(BF16)
    yield
    g = proj(pool_width + 3 * ret_width, ret_width)
    out.sg[...] = g * jax.nn.sigmoid(g)
    yield
    for hh, (qh, kh) in enumerate(roped):
        for bi, r0 in enumerate(range(0, tile, blk)):
            scores = lax.dot_general(qh[r0:r0 + blk], kh[r0:r0 + blk], (((1,), (1,)), ((), ())),
                                     preferred_element_type=F32)
            out.probs[hh * (tile // blk) + bi] = (scores * sp.dmat[hh]).astype(BF16)
    yield
    gate_lo = pool_width + 4 * ret_width
    for br in range(N_BRANCHES):
        cols = slice(br * d_model, (br + 1) * d_model)
        out.gates[:, cols] = jax.nn.sigmoid(proj(gate_lo + br * d_model, d_model) + sp.b_gate[:, cols])
        yield


def _mix_phases(st: Staged, x_ref, rows, o_ref, mod_ref, mp: MixParams, tile_decay):
    d_model = x_ref.shape[1]
    ret_width = st.sg.shape[1]
    hd = ret_width // RET_HEADS
    gt_m = mod_ref[2:3, :]

    tile = st.sg.shape[0]
    blk = st.probs.shape[1]
    n_blk = tile // blk
    gated = [[None] * RET_HEADS for _ in range(n_blk)]
    for bi in range(n_blk):
        blk_rows = slice(bi * blk, (bi + 1) * blk)
        for hh in range(RET_HEADS):
            cols = slice(hh * hd, (hh + 1) * hd)
            vh = st.rot[0, blk_rows, cols]
            o = jnp.dot(st.probs[hh * n_blk + bi], vh, preferred_element_type=F32)
            o = o + jnp.dot(st.rot[1, blk_rows, cols], mp.state[hh].astype(BF16),
                            preferred_element_type=F32)
            kv = lax.dot_general(st.rot[2, blk_rows, cols], vh, (((0,), (0,)), ((), ())),
                                 preferred_element_type=F32)
            mp.state[hh] = mp.state[hh] * tile_decay[hh] + kv
            mu = jnp.mean(o, axis=-1, keepdims=True)
            oc = o - mu
            var = jnp.mean(oc * oc, axis=-1, keepdims=True)
            gated[bi][hh] = (st.sg[blk_rows, cols] * (oc * lax.rsqrt(var + EPS))).astype(BF16)
            yield
    y_pool = jnp.dot(st.pooled[...], mp.w_pool[...], preferred_element_type=F32)
    yield
    gated = jnp.concatenate([jnp.concatenate(row, axis=-1) for row in gated], axis=0)
    y_ret = jnp.dot(gated, mp.w_br[...], preferred_element_type=F32)
    yield
    merged = (st.gates[:, 0:d_model] * y_pool + st.gates[:, d_model:2 * d_model] * y_ret).astype(BF16)
    y = jnp.dot(merged, mp.w_out[...], preferred_element_type=F32)
    yield
    x1 = x_ref[rows, :] + gt_m * (_rms(y) * mp.g_post[...])
    o_ref[rows, :] = x1
    mp.x1[...] = x1


def _up_phases(x1_ref, mod_ref, g_ref, w1_ref, a_ref):
    sh_f, sc_f = mod_ref[3:4, :], mod_ref[4:5, :]
    h = ((_rms(x1_ref[...]) * g_ref[...]) * (1.0 + sc_f) + sh_f).astype(BF16)
    yield
    for c0 in range(0, w1_ref.shape[1], UP_CHUNK):
        a = jnp.maximum(jnp.dot(h, w1_ref[:, c0:c0 + UP_CHUNK], preferred_element_type=F32), 0.0)
        a_ref[:, c0:c0 + UP_CHUNK] = (a * a).astype(BF16)
        yield


WEAVE = "musmmmsumussumussumussusum"


def _weave(order, **gens):
    for c in order:
        if c in gens:
            next(gens[c], None)
    for g in gens.values():
        for _ in g:
            pass


def _mixer_kernel(x_ref, xn_ref, mod_all_ref, pos_ref, posn_ref, invf_ref, g_pre_ref, g_post_ref,
                  w_in_ref, b_gate_ref, w_pool_ref, w_br_ref, w_out_ref, dmat_ref, qdec_ref, kdec_ref,
                  g_ffn_ref, w1_ref, wf2_ref, o_ref, a_ref, wf2_o_ref, state_ref, halo_ref, x1_ref, *staged,
                  tile, steps_per_seq, n_tiles, tile_decay):
    j = pl.program_id(0)
    wf2_o_ref[...] = wf2_ref[...].astype(BF16)
    last = n_tiles - 1

    def batch_and_tile(t):
        return t // steps_per_seq, t % steps_per_seq

    b, s = batch_and_tile(jnp.minimum(j, last))
    b_next, s_next = batch_and_tile(jnp.minimum(j + 1, last))
    b_prev, _ = batch_and_tile(jnp.maximum(j - 1, 0))
    mod_ref, modn_ref, modp_ref = (ModRows(mod_all_ref, r) for r in (b, b_next, b_prev))
    sp = StageParams(g_pre_ref, w_in_ref, b_gate_ref, invf_ref, dmat_ref, qdec_ref, kdec_ref, halo_ref)
    mp = MixParams(g_post_ref, w_pool_ref, w_br_ref, w_out_ref, state_ref, x1_ref)
    n_fields = len(Staged._fields)
    slots = (Staged(*staged[:n_fields]), Staged(*staged[n_fields:]))
    whole = slice(0, tile)

    @pl.when(j == 0)
    def _():
        for _ in _stage_phases(x_ref, whole, mod_ref, pos_ref[pl.ds(b, 1), :], 0, True, sp, slots[0]):
            pass
        x1_ref[...] = jnp.zeros_like(x1_ref)

    @pl.when((s == 0) & (j <= last))
    def _():
        state_ref[...] = jnp.zeros_like(state_ref)

    def mix(slot):
        return _mix_phases(slot, x_ref, whole, o_ref, mod_ref, mp, tile_decay)

    def up():
        return _up_phases(x1_ref, modp_ref, g_ffn_ref, w1_ref, a_ref)

    for parity in range(2):
        @pl.when((j % 2 == parity) & (j < last))
        def _():
            _weave(WEAVE, m=mix(slots[parity]), u=up(),
                   s=_stage_phases(xn_ref, whole, modn_ref, posn_ref[pl.ds(b_next, 1), :], s_next,
                                   s_next == 0, sp, slots[1 - parity]))

    @pl.when(j == last)
    def _():
        _weave(WEAVE, m=mix(slots[last % 2]), u=up())

    @pl.when(j == n_tiles)
    def _():
        _weave(WEAVE, u=up())


def _decay_tables(tile, hd):
    gamma = 1.0 - 2.0 ** (-5.0 - np.arange(RET_HEADS, dtype=np.float64))
    idx = np.arange(tile)
    dist = np.abs(idx[:, None] - idx[None, :])
    visible = (idx[None, :] // CHUNK) <= (idx[:, None] // CHUNK)
    dmat = np.where(visible[None], gamma[:, None, None] ** dist[None], 0.0)
    qdec = np.broadcast_to((gamma[:, None] ** (idx[None, :] + 1))[:, :, None], (RET_HEADS, tile, hd))
    kdec = np.broadcast_to((gamma[:, None] ** (tile - 1 - idx[None, :]))[:, :, None], (RET_HEADS, tile, hd))
    tile_decay = tuple(float(g) ** tile for g in gamma)
    as_f32 = lambda a: jnp.asarray(np.ascontiguousarray(a), dtype=F32)
    return as_f32(dmat), as_f32(qdec), as_f32(kdec), tile_decay


def _const_spec(shape):
    nd = len(shape)
    return pl.BlockSpec(shape, lambda j: (0,) * nd)


def _mixer(x, mod, positions, g_pre, g_post, w_in, b_gate, w_pool, w_br, w_out, g_ffn, w1, w2):
    bsz, seq, d = x.shape
    tile = MIXER_TILE
    pool_width = w_pool.shape[0]
    ret_width = w_br.shape[0]
    d_ff = w1.shape[1]
    hd = ret_width // RET_HEADS
    assert seq % tile == 0 and tile % RET_BLOCK == 0 and RET_BLOCK % CHUNK == 0 and hd == LANES
    assert d_ff % UP_CHUNK == 0
    steps_per_seq = seq // tile
    n_tiles = bsz * steps_per_seq
    w2_rows = w2.shape[0] // n_tiles
    assert w2.shape[0] % (n_tiles * 16) == 0
    dmat, qdec, kdec, tile_decay = _decay_tables(RET_BLOCK, hd)
    half = hd // 2
    invf = (ROPE_BASE ** (-jnp.arange(half, dtype=F32) / half)).reshape(half, 1)

    def tile_at(t):
        return t // steps_per_seq, t % steps_per_seq

    def cur(j):
        return tile_at(jnp.minimum(j, n_tiles - 1))

    def nxt(j):
        return tile_at(jnp.minimum(j + 1, n_tiles - 1))

    def prev(j):
        return tile_at(jnp.maximum(j - 1, 0))

    const = _const_spec
    operands = [
        (x, pl.BlockSpec((None, tile, d), lambda j: (*cur(j), 0))),
        (x, pl.BlockSpec((None, tile, d), lambda j: (*nxt(j), 0))),
        (mod, const(mod.shape)),
        (positions, pl.BlockSpec((bsz, tile), lambda j: (0, cur(j)[1]))),
        (positions, pl.BlockSpec((bsz, tile), lambda j: (0, nxt(j)[1]))),
        (invf, const((half, 1))),
        (g_pre.reshape(1, d), const((1, d))),
        (g_post.reshape(1, d), const((1, d))),
        (w_in, const(w_in.shape)),
        (b_gate.reshape(1, N_BRANCHES * d), const((1, N_BRANCHES * d))),
        (w_pool, const(w_pool.shape)),
        (w_br, const(w_br.shape)),
        (w_out, const(w_out.shape)),
        (dmat, const(dmat.shape)),
        (qdec, const(qdec.shape)),
        (kdec, const(kdec.shape)),
        (g_ffn.reshape(1, d), const((1, d))),
        (w1, const(w1.shape)),
        (w2, pl.BlockSpec((w2_rows, w2.shape[1]), lambda j: (jnp.minimum(j, n_tiles - 1), 0))),
    ]
    staged = [
        pltpu.VMEM((RET_HEADS * (tile // RET_BLOCK), RET_BLOCK, RET_BLOCK), BF16),
        pltpu.VMEM((3, tile, ret_width), BF16),
        pltpu.VMEM((tile, pool_width), BF16),
        pltpu.VMEM((tile, ret_width), F32),
        pltpu.VMEM((tile, N_BRANCHES * d), F32),
    ]
    return pl.pallas_call(
        functools.partial(_mixer_kernel, tile=tile, steps_per_seq=steps_per_seq, n_tiles=n_tiles,
                          tile_decay=tile_decay),
        out_shape=[jax.ShapeDtypeStruct(x.shape, F32), jax.ShapeDtypeStruct((bsz, seq, d_ff), BF16),
                   jax.ShapeDtypeStruct(w2.shape, BF16)],
        grid=(n_tiles + 1,),
        in_specs=[spec for _, spec in operands],
        out_specs=[pl.BlockSpec((None, tile, d), lambda j: (*cur(j), 0)),
                   pl.BlockSpec((None, tile, d_ff), lambda j: (*prev(j), 0)),
                   pl.BlockSpec((w2_rows, w2.shape[1]), lambda j: (jnp.minimum(j, n_tiles - 1), 0))],
        scratch_shapes=[
            pltpu.VMEM((RET_HEADS, hd, hd), F32),
            pltpu.VMEM((POOL_HALO, pool_width), F32),
            pltpu.VMEM((tile, d), F32),
        ] + staged + staged,
        compiler_params=pltpu.CompilerParams(
            dimension_semantics=("arbitrary",),
            vmem_limit_bytes=VMEM_LIMIT_BYTES),
        name="mixer",
    )(*[a for a, _ in operands])


def _ffn_down_kernel(a_ref, x_ref, mod_ref, g_post_ref, w2_ref, o_ref, *, tiles_per_seq):
    gt_f = ModRows(mod_ref, pl.program_id(0) // tiles_per_seq)[5:6, :]
    for r0 in range(0, a_ref.shape[0], FFN_ROWS):
        rows = slice(r0, r0 + FFN_ROWS)
        y = jnp.dot(a_ref[rows, :], w2_ref[...], preferred_element_type=F32)
        o_ref[rows, :] = x_ref[rows, :] + gt_f * (_rms(y) * g_post_ref[...])


def _ffn_down(a, x, mod, g_post, w2):
    bsz, seq, d = x.shape
    tile = FFN_TILE
    assert seq % tile == 0
    tiles_per_seq = seq // tile

    def cur(j):
        return j // tiles_per_seq, j % tiles_per_seq

    return pl.pallas_call(
        functools.partial(_ffn_down_kernel, tiles_per_seq=tiles_per_seq),
        out_shape=jax.ShapeDtypeStruct(x.shape, F32),
        grid=(bsz * tiles_per_seq,),
        in_specs=[
            pl.BlockSpec((None, tile, a.shape[2]), lambda j: (*cur(j), 0)),
            pl.BlockSpec((None, tile, d), lambda j: (*cur(j), 0)),
            _const_spec(mod.shape),
            _const_spec((1, d)),
            _const_spec(w2.shape),
        ],
        out_specs=pl.BlockSpec((None, tile, d), lambda j: (*cur(j), 0)),
        compiler_params=pltpu.CompilerParams(
            dimension_semantics=("arbitrary",),
            vmem_limit_bytes=VMEM_LIMIT_BYTES),
        name="ffn_down",
    )(a, x, mod, g_post.reshape(1, d), w2)


def kernel(x, c, positions, ada_w, ada_b, mix_pre_g, mix_post_g, ffn_pre_g, ffn_post_g, w_in,
           b_branch_gate, pool_w, pool_scale, w_branch_pool, w_branch_ret, w_out, w_ff1, w_ff2):
    depth = ada_w.shape[0]
    for l in range(depth):
        mod, w_pool, w_in_b, w_br_b, w_out_b, w1_b = _prep(
            c, ada_w[l], ada_b[l], pool_w[l], pool_scale[l], w_branch_pool[l],
            (w_in[l], w_branch_ret[l], w_out[l], w_ff1[l]))
        x, a, w2_b = _mixer(x, mod, positions, mix_pre_g[l], mix_post_g[l], w_in_b, b_branch_gate[l],
                            w_pool, w_br_b, w_out_b, ffn_pre_g[l], w1_b, w_ff2[l])
        x = _ffn_down(a, x, mod, ffn_post_g[l], w2_b)
    return x
```

```python
import functools
from typing import Any, NamedTuple

import jax
import jax.numpy as jnp
import numpy as np
from jax import lax
from jax.experimental import pallas as pl
from jax.experimental.pallas import tpu as pltpu

F32 = jnp.float32
BF16 = jnp.bfloat16

CHUNK = 64
POOL_WINDOWS = (2, 4, 8, 16)
POOL_HALO = 16
POOL_ROWS = 64
RET_HEADS = 4
N_BRANCHES = 2
N_MOD = 6
ROPE_BASE = 10000.0
EPS = 1e-6

LANES = 128
MIXER_TILE = 256
RET_BLOCK = 256
FFN_TILE = 1024
FFN_ROWS = 512
UP_CHUNK = 512
ADALN_BLOCK_N = 1536
VMEM_LIMIT_BYTES = 56 * 1024 * 1024


def _rms(xf):
    return xf * lax.rsqrt(jnp.mean(xf * xf, axis=-1, keepdims=True) + EPS)


class ModRows:
    def __init__(self, ref, b):
        self.ref, self.b, self.d = ref, b, ref.shape[1] // N_MOD

    def __getitem__(self, idx):
        k = idx[0].start
        return self.ref[pl.ds(self.b, 1), k * self.d:(k + 1) * self.d]


def _prep_kernel(c_ref, w_ref, b_ref, pw_ref, scale_ref, wbp_ref, *refs):
    n_cast = (len(refs) - 2) // 2
    cast_in, (mod_ref, w_pool_ref), cast_out = refs[:n_cast], refs[n_cast:n_cast + 2], refs[n_cast + 2:]
    c = c_ref[...]
    a = (c * jax.nn.sigmoid(c)).astype(BF16)
    mod_ref[...] = jnp.dot(a, w_ref[...].astype(BF16), preferred_element_type=F32) + b_ref[...]
    w_pool_ref[...] = jnp.dot((pw_ref[...] * scale_ref[...]).astype(BF16), wbp_ref[...].astype(BF16),
                              preferred_element_type=F32).astype(BF16)
    for src, dst in zip(cast_in, cast_out):
        dst[...] = src[...].astype(BF16)


def _prep(c, w, b, pool_w, pool_scale, w_bp, to_cast):
    bsz, d = c.shape
    n = w.shape[1]
    n_steps = n // ADALN_BLOCK_N
    groups, gd, _ = pool_w.shape
    spg = n_steps // groups
    rows = gd // spg
    assert n_steps % groups == 0 and gd % spg == 0 and rows % 16 == 0
    assert all(m.shape[0] % (n_steps * 16) == 0 for m in to_cast)
    cast_specs = [pl.BlockSpec((m.shape[0] // n_steps, m.shape[1]), lambda j: (j, 0)) for m in to_cast]
    return pl.pallas_call(
        _prep_kernel,
        out_shape=[jax.ShapeDtypeStruct((bsz, n), F32), jax.ShapeDtypeStruct(w_bp.shape, BF16)]
                  + [jax.ShapeDtypeStruct(m.shape, BF16) for m in to_cast],
        grid=(n_steps,),
        in_specs=[
            pl.BlockSpec((bsz, d), lambda j: (0, 0)),
            pl.BlockSpec((d, ADALN_BLOCK_N), lambda j: (0, j)),
            pl.BlockSpec((1, ADALN_BLOCK_N), lambda j: (0, j)),
            pl.BlockSpec((None, rows, gd), lambda j: (j // spg, j % spg, 0)),
            pl.BlockSpec((1, gd), lambda j: (0, j // spg)),
            pl.BlockSpec((gd, w_bp.shape[1]), lambda j: (j // spg, 0)),
        ] + cast_specs,
        out_specs=[pl.BlockSpec((bsz, ADALN_BLOCK_N), lambda j: (0, j)),
                   pl.BlockSpec((rows, w_bp.shape[1]), lambda j: (j, 0))] + cast_specs,
        compiler_params=pltpu.CompilerParams(dimension_semantics=("arbitrary",)),
        name="prep",
    )(c, w, b.reshape(1, n), pool_w, pool_scale.reshape(1, groups * gd), w_bp, *to_cast)


class Staged(NamedTuple):
    probs: Any
    rot: Any
    pooled: Any
    sg: Any
    gates: Any


class StageParams(NamedTuple):
    g_pre: Any
    w_in: Any
    b_gate: Any
    invf: Any
    dmat: Any
    qdec: Any
    kdec: Any
    halo: Any


class MixParams(NamedTuple):
    g_post: Any
    w_pool: Any
    w_br: Any
    w_out: Any
    state: Any
    x1: Any


def _stage_phases(x_ref, rows, mod_ref, pos, seq_tile, first, sp: StageParams, out: Staged):
    tile = rows.stop - rows.start
    pool_width = out.pooled.shape[1]
    ret_width = out.sg.shape[1]
    d_model = out.gates.shape[1] // N_BRANCHES
    hd = ret_width // RET_HEADS
    sh_m, sc_m = mod_ref[0:1, :], mod_ref[1:2, :]
    h = ((_rms(x_ref[rows, :]) * sp.g_pre[...]) * (1.0 + sc_m) + sh_m).astype(BF16)
    yield

    def proj(lo, width):
        return jnp.dot(h, sp.w_in[:, lo:lo + width], preferred_element_type=F32)

    u = proj(0, pool_width)
    yield

    if first is True:
        halo = jnp.zeros(sp.halo.shape, F32)
    elif first is False:
        halo = sp.halo[...]
    else:
        halo = jnp.where(first, 0.0, sp.halo[...])
    sp.halo[...] = u[tile - POOL_HALO:tile, :]
    t_pos = (lax.broadcasted_iota(jnp.int32, (tile, LANES), 0) + (seq_tile * tile + 1)).astype(F32)
    inv_t = 1.0 / t_pos
    gd = pool_width // len(POOL_WINDOWS)
    for gi, w in enumerate(POOL_WINDOWS):
        cols = slice(gi * gd, (gi + 1) * gd)
        inv_cnt = jnp.where(t_pos < float(w), inv_t, 1.0 / w)
        for r0 in range(0, tile, POOL_ROWS):
            blk = slice(r0, r0 + POOL_ROWS)
            before = halo[:, cols] if r0 == 0 else u[r0 - POOL_HALO:r0, cols]
            s = jnp.concatenate([before, u[blk, cols]], axis=0)
            k = 1
            while k < w:
                s = s + pltpu.roll(s, k, 0)
                k *= 2
            out.pooled[blk, cols] = (s[POOL_HALO:] * inv_cnt[blk] - u[blk, cols]).astype(BF16)
    q = proj(pool_width, ret_width)
    yield
    k_all = proj(pool_width + ret_width, ret_width)
    yield

    ang = sp.invf[...] * pos.astype(F32)
    cos_h, sin_h = jnp.cos(ang), jnp.sin(ang)
    cos_t = jnp.transpose(jnp.concatenate([cos_h, cos_h], axis=0))
    sin_t = jnp.transpose(jnp.concatenate([-sin_h, sin_h], axis=0))
    blk = sp.dmat.shape[1]
    roped = []
    for hh in range(RET_HEADS):
        cols = slice(hh * hd, (hh + 1) * hd)
        qh, kh = q[:, cols], k_all[:, cols]
        qh = (qh * cos_t + pltpu.roll(qh, hd // 2, 1) * sin_t) * (hd ** -0.5)
        kh = kh * cos_t + pltpu.roll(kh, hd // 2, 1) * sin_t
        roped.append((qh.astype(BF16), kh.astype(BF16)))
        for r0 in range(0, tile, blk):
            out.rot[1, r0:r0 + blk, cols] = (qh[r0:r0 + blk] * sp.qdec[hh]).astype(BF16)
            out.rot[2, r0:r0 + blk, cols] = (kh[r0:r0 + blk] * sp.kdec[hh]).astype(BF16)
    out.rot[0] = proj(pool_width + 2 * ret_width, ret_width).astype(BF16)
    yield
    g = proj(pool_width + 3 * ret_width, ret_width)
    out.sg[...] = g * jax.nn.sigmoid(g)
    yield
    for hh, (qh, kh) in enumerate(roped):
        for bi, r0 in enumerate(range(0, tile, blk)):
            scores = lax.dot_general(qh[r0:r0 + blk], kh[r0:r0 + blk], (((1,), (1,)), ((), ())),
                                     preferred_element_type=F32)
            out.probs[hh * (tile // blk) + bi] = (scores * sp.dmat[hh]).astype(BF16)
    yield
    gate_lo = pool_width + 4 * ret_width
    for br in range(N_BRANCHES):
        cols = slice(br * d_model, (br + 1) * d_model)
        out.gates[:, cols] = jax.nn.sigmoid(proj(gate_lo + br * d_model, d_model) + sp.b_gate[:, cols])
        yield


def _mix_phases(st: Staged, x_ref, rows, o_ref, mod_ref, mp: MixParams, tile_decay):
    d_model = x_ref.shape[1]
    ret_width = st.sg.shape[1]
    hd = ret_width // RET_HEADS
    gt_m = mod_ref[2:3, :]

    tile = st.sg.shape[0]
    blk = st.probs.shape[1]
    n_blk = tile // blk
    gated = [[None] * RET_HEADS for _ in range(n_blk)]
    for bi in range(n_blk):
        blk_rows = slice(bi * blk, (bi + 1) * blk)
        for hh in range(RET_HEADS):
            cols = slice(hh * hd, (hh + 1) * hd)
            vh = st.rot[0, blk_rows, cols]
            o = jnp.dot(st.probs[hh * n_blk + bi], vh, preferred_element_type=F32)
            o = o + jnp.dot(st.rot[1, blk_rows, cols], mp.state[hh].astype(BF16),
                            preferred_element_type=F32)
            kv = lax.dot_general(st.rot[2, blk_rows, cols], vh, (((0,), (0,)), ((), ())),
                                 preferred_element_type=F32)
            mp.state[hh] = mp.state[hh] * tile_decay[hh] + kv
            mu = jnp.mean(o, axis=-1, keepdims=True)
            oc = o - mu
            var = jnp.mean(oc * oc, axis=-1, keepdims=True)
            gated[bi][hh] = (st.sg[blk_rows, cols] * (oc * lax.rsqrt(var + EPS))).astype(BF16)
            yield
    y_pool = jnp.dot(st.pooled[...], mp.w_pool[...], preferred_element_type=F32)
    yield
    gated = jnp.concatenate([jnp.concatenate(row, axis=-1) for row in gated], axis=0)
    y_ret = jnp.dot(gated, mp.w_br[...], preferred_element_type=F32)
    yield
    merged = (st.gates[:, 0:d_model] * y_pool + st.gates[:, d_model:2 * d_model] * y_ret).astype(BF16)
    y = jnp.dot(merged, mp.w_out[...], preferred_element_type=F32)
    yield
    x1 = x_ref[rows, :] + gt_m * (_rms(y) * mp.g_post[...])
    o_ref[rows, :] = x1
    mp.x1[...] = x1


def _up_phases(x1_ref, mod_ref, g_ref, w1_ref, a_ref):
    sh_f, sc_f = mod_ref[3:4, :], mod_ref[4:5, :]
    h = ((_rms(x1_ref[...]) * g_ref[...]) * (1.0 + sc_f) + sh_f).astype(BF16)
    yield
    for c0 in range(0, w1_ref.shape[1], UP_CHUNK):
        a = jnp.maximum(jnp.dot(h, w1_ref[:, c0:c0 + UP_CHUNK], preferred_element_type=F32), 0.0)
        a_ref[:, c0:c0 + UP_CHUNK] = (a * a).astype(BF16)
        yield


WEAVE = "musmmmsumussumussumussusum"


def _weave(order, **gens):
    for c in order:
        if c in gens:
            next(gens[c], None)
    for g in gens.values():
        for _ in g:
            pass


def _mixer_kernel(x_ref, xn_ref, mod_all_ref, pos_ref, posn_ref, invf_ref, g_pre_ref, g_post_ref,
                  w_in_ref, b_gate_ref, w_pool_ref, w_br_ref, w_out_ref, dmat_ref, qdec_ref, kdec_ref,
                  g_ffn_ref, w1_ref, wf2_ref, o_ref, a_ref, wf2_o_ref, state_ref, halo_ref, x1_ref, *staged,
                  tile, steps_per_seq, n_tiles, tile_decay):
    j = pl.program_id(0)
    wf2_o_ref[...] = wf2_ref[...].astype(BF16)
    last = n_tiles - 1

    def batch_and_tile(t):
        return t // steps_per_seq, t % steps_per_seq

    b, s = batch_and_tile(jnp.minimum(j, last))
    b_next, s_next = batch_and_tile(jnp.minimum(j + 1, last))
    b_prev, _ = batch_and_tile(jnp.maximum(j - 1, 0))
    mod_ref, modn_ref, modp_ref = (ModRows(mod_all_ref, r) for r in (b, b_next, b_prev))
    sp = StageParams(g_pre_ref, w_in_ref, b_gate_ref, invf_ref, dmat_ref, qdec_ref, kdec_ref, halo_ref)
    mp = MixParams(g_post_ref, w_pool_ref, w_br_ref, w_out_ref, state_ref, x1_ref)
    n_fields = len(Staged._fields)
    slots = (Staged(*staged[:n_fields]), Staged(*staged[n_fields:]))
    whole = slice(0, tile)

    @pl.when(j == 0)
    def _():
        for _ in _stage_phases(x_ref, whole, mod_ref, pos_ref[pl.ds(b, 1), :], 0, True, sp, slots[0]):
            pass
        x1_ref[...] = jnp.zeros_like(x1_ref)

    @pl.when((s == 0) & (j <= last))
    def _():
        state_ref[...] = jnp.zeros_like(state_ref)

    def mix(slot):
        return _mix_phases(slot, x_ref, whole, o_ref, mod_ref, mp, tile_decay)

    def up():
        return _up_phases(x1_ref, modp_ref, g_ffn_ref, w1_ref, a_ref)

    for parity in range(2):
        @pl.when((j % 2 == parity) & (j < last))
        def _():
            _weave(WEAVE, m=mix(slots[parity]), u=up(),
                   s=_stage_phases(xn_ref, whole, modn_ref, posn_ref[pl.ds(b_next, 1), :], s_next,
                                   s_next == 0, sp, slots[1 - parity]))

    @pl.when(j == last)
    def _():
        _weave(WEAVE, m=mix(slots[last % 2]), u=up())

    @pl.when(j == n_tiles)
    def _():
        _weave(WEAVE, u=up())


def _decay_tables(tile, hd):
    gamma = 1.0 - 2.0 ** (-5.0 - np.arange(RET_HEADS, dtype=np.float64))
    idx = np.arange(tile)
    dist = np.abs(idx[:, None] - idx[None, :])
    visible = (idx[None, :] // CHUNK) <= (idx[:, None] // CHUNK)
    dmat = np.where(visible[None], gamma[:, None, None] ** dist[None], 0.0)
    qdec = np.broadcast_to((gamma[:, None] ** (idx[None, :] + 1))[:, :, None], (RET_HEADS, tile, hd))
    kdec = np.broadcast_to((gamma[:, None] ** (tile - 1 - idx[None, :]))[:, :, None], (RET_HEADS, tile, hd))
    tile_decay = tuple(float(g) ** tile for g in gamma)
    as_f32 = lambda a: jnp.asarray(np.ascontiguousarray(a), dtype=F32)
    return as_f32(dmat), as_f32(qdec), as_f32(kdec), tile_decay


def _const_spec(shape):
    nd = len(shape)
    return pl.BlockSpec(shape, lambda j: (0,) * nd)


def _mixer(x, mod, positions, g_pre, g_post, w_in, b_gate, w_pool, w_br, w_out, g_ffn, w1, w2):
    bsz, seq, d = x.shape
    tile = MIXER_TILE
    pool_width = w_pool.shape[0]
    ret_width = w_br.shape[0]
    d_ff = w1.shape[1]
    hd = ret_width // RET_HEADS
    assert seq % tile == 0 and tile % RET_BLOCK == 0 and RET_BLOCK % CHUNK == 0 and hd == LANES
    assert d_ff % UP_CHUNK == 0
    steps_per_seq = seq // tile
    n_tiles = bsz * steps_per_seq
    w2_rows = w2.shape[0] // n_tiles
    assert w2.shape[0] % (n_tiles * 16) == 0
    dmat, qdec, kdec, tile_decay = _decay_tables(RET_BLOCK, hd)
    half = hd // 2
    invf = (ROPE_BASE ** (-jnp.arange(half, dtype=F32) / half)).reshape(half, 1)

    def tile_at(t):
        return t // steps_per_seq, t % steps_per_seq

    def cur(j):
        return tile_at(jnp.minimum(j, n_tiles - 1))

    def nxt(j):
        return tile_at(jnp.minimum(j + 1, n_tiles - 1))

    def prev(j):
        return tile_at(jnp.maximum(j - 1, 0))

    const = _const_spec
    operands = [
        (x, pl.BlockSpec((None, tile, d), lambda j: (*cur(j), 0))),
        (x, pl.BlockSpec((None, tile, d), lambda j: (*nxt(j), 0))),
        (mod, const(mod.shape)),
        (positions, pl.BlockSpec((bsz, tile), lambda j: (0, cur(j)[1]))),
        (positions, pl.BlockSpec((bsz, tile), lambda j: (0, nxt(j)[1]))),
        (invf, const((half, 1))),
        (g_pre.reshape(1, d), const((1, d))),
        (g_post.reshape(1, d), const((1, d))),
        (w_in, const(w_in.shape)),
        (b_gate.reshape(1, N_BRANCHES * d), const((1, N_BRANCHES * d))),
        (w_pool, const(w_pool.shape)),
        (w_br, const(w_br.shape)),
        (w_out, const(w_out.shape)),
        (dmat, const(dmat.shape)),
        (qdec, const(qdec.shape)),
        (kdec, const(kdec.shape)),
        (g_ffn.reshape(1, d), const((1, d))),
        (w1, const(w1.shape)),
        (w2, pl.BlockSpec((w2_rows, w2.shape[1]), lambda j: (jnp.minimum(j, n_tiles - 1), 0))),
    ]
    staged = [
        pltpu.VMEM((RET_HEADS * (tile // RET_BLOCK), RET_BLOCK, RET_BLOCK), BF16),
        pltpu.VMEM((3, tile, ret_width), BF16),
        pltpu.VMEM((tile, pool_width), BF16),
        pltpu.VMEM((tile, ret_width), F32),
        pltpu.VMEM((tile, N_BRANCHES * d), F32),
    ]
    return pl.pallas_call(
        functools.partial(_mixer_kernel, tile=tile, steps_per_seq=steps_per_seq, n_tiles=n_tiles,
                          tile_decay=tile_decay),
        out_shape=[jax.ShapeDtypeStruct(x.shape, F32), jax.ShapeDtypeStruct((bsz, seq, d_ff), BF16),
                   jax.ShapeDtypeStruct(w2.shape, BF16)],
        grid=(n_tiles + 1,),
        in_specs=[spec for _, spec in operands],
        out_specs=[pl.BlockSpec((None, tile, d), lambda j: (*cur(j), 0)),
                   pl.BlockSpec((None, tile, d_ff), lambda j: (*prev(j), 0)),
                   pl.BlockSpec((w2_rows, w2.shape[1]), lambda j: (jnp.minimum(j, n_tiles - 1), 0))],
        scratch_shapes=[
            pltpu.VMEM((RET_HEADS, hd, hd), F32),
            pltpu.VMEM((POOL_HALO, pool_width), F32),
            pltpu.VMEM((tile, d), F32),
        ] + staged + staged,
        compiler_params=pltpu.CompilerParams(
            dimension_semantics=("arbitrary",),
            vmem_limit_bytes=VMEM_LIMIT_BYTES),
        name="mixer",
    )(*[a for a, _ in operands])


def _ffn_down_kernel(a_ref, x_ref, mod_ref, g_post_ref, w2_ref, o_ref, *, tiles_per_seq):
    gt_f = ModRows(mod_ref, pl.program_id(0) // tiles_per_seq)[5:6, :]
    for r0 in range(0, a_ref.shape[0], FFN_ROWS):
        rows = slice(r0, r0 + FFN_ROWS)
        y = jnp.dot(a_ref[rows, :], w2_ref[...], preferred_element_type=F32)
        o_ref[rows, :] = x_ref[rows, :] + gt_f * (_rms(y) * g_post_ref[...])


def _ffn_down(a, x, mod, g_post, w2):
    bsz, seq, d = x.shape
    tile = FFN_TILE
    assert seq % tile == 0
    tiles_per_seq = seq // tile

    def cur(j):
        return j // tiles_per_seq, j % tiles_per_seq

    return pl.pallas_call(
        functools.partial(_ffn_down_kernel, tiles_per_seq=tiles_per_seq),
        out_shape=jax.ShapeDtypeStruct(x.shape, F32),
        grid=(bsz * tiles_per_seq,),
        in_specs=[
            pl.BlockSpec((None, tile, a.shape[2]), lambda j: (*cur(j), 0)),
            pl.BlockSpec((None, tile, d), lambda j: (*cur(j), 0)),
            _const_spec(mod.shape),
            _const_spec((1, d)),
            _const_spec(w2.shape),
        ],
        out_specs=pl.BlockSpec((None, tile, d), lambda j: (*cur(j), 0)),
        compiler_params=pltpu.CompilerParams(
            dimension_semantics=("arbitrary",),
            vmem_limit_bytes=VMEM_LIMIT_BYTES),
        name="ffn_down",
    )(a, x, mod, g_post.reshape(1, d), w2)


def kernel(x, c, positions, ada_w, ada_b, mix_pre_g, mix_post_g, ffn_pre_g, ffn_post_g, w_in,
           b_branch_gate, pool_w, pool_scale, w_branch_pool, w_branch_ret, w_out, w_ff1, w_ff2):
    depth = ada_w.shape[0]
    for l in range(depth):
        mod, w_pool, w_in_b, w_br_b, w_out_b, w1_b = _prep(
            c, ada_w[l], ada_b[l], pool_w[l], pool_scale[l], w_branch_pool[l],
            (w_in[l], w_branch_ret[l], w_out[l], w_ff1[l]))
        x, a, w2_b = _mixer(x, mod, positions, mix_pre_g[l], mix_post_g[l], w_in_b, b_branch_gate[l],
                            w_pool, w_br_b, w_out_b, ffn_pre_g[l], w1_b, w_ff2[l])
        x = _ffn_down(a, x, mod, ffn_post_g[l], w2_b)
    return x
```

```python
import functools
from typing import Any, NamedTuple

import jax
import jax.numpy as jnp
import numpy as np
from jax import lax
from jax.experimental import pallas as pl
from jax.experimental.pallas import tpu as pltpu

F32 = jnp.float32
BF16 = jnp.bfloat16

CHUNK = 64
POOL_WINDOWS = (2, 4, 8, 16)
POOL_HALO = 16
POOL_ROWS = 64
RET_HEADS = 4
N_BRANCHES = 2
N_MOD = 6
ROPE_BASE = 10000.0
EPS = 1e-6

LANES = 128
MIXER_TILE = 256
RET_BLOCK = 256
FFN_TILE = 1024
FFN_ROWS = 512
UP_CHUNK = 512
ADALN_BLOCK_N = 1536
VMEM_LIMIT_BYTES = 56 * 1024 * 1024


def _rms(xf):
    return xf * lax.rsqrt(jnp.mean(xf * xf, axis=-1, keepdims=True) + EPS)


class ModRows:
    def __init__(self, ref, b):
        self.ref, self.b, self.d = ref, b, ref.shape[1] // N_MOD

    def __getitem__(self, idx):
        k = idx[0].start
        return self.ref[pl.ds(self.b, 1), k * self.d:(k + 1) * self.d]


def _prep_kernel(c_ref, w_ref, b_ref, pw_ref, scale_ref, wbp_ref, *refs):
    n_cast = (len(refs) - 2) // 2
    cast_in, (mod_ref, w_pool_ref), cast_out = refs[:n_cast], refs[n_cast:n_cast + 2], refs[n_cast + 2:]
    c = c_ref[...]
    a = (c * jax.nn.sigmoid(c)).astype(BF16)
    mod_ref[...] = jnp.dot(a, w_ref[...].astype(BF16), preferred_element_type=F32) + b_ref[...]
    w_pool_ref[...] = jnp.dot((pw_ref[...] * scale_ref[...]).astype(BF16), wbp_ref[...].astype(BF16),
                              preferred_element_type=F32).astype(BF16)
    for src, dst in zip(cast_in, cast_out):
        dst[...] = src[...].astype(BF16)


def _prep(c, w, b, pool_w, pool_scale, w_bp, to_cast):
    bsz, d = c.shape
    n = w.shape[1]
    n_steps = n // ADALN_BLOCK_N
    groups, gd, _ = pool_w.shape
    spg = n_steps // groups
    rows = gd // spg
    assert n_steps % groups == 0 and gd % spg == 0 and rows % 16 == 0
    assert all(m.shape[0] % (n_steps * 16) == 0 for m in to_cast)
    cast_specs = [pl.BlockSpec((m.shape[0] // n_steps, m.shape[1]), lambda j: (j, 0)) for m in to_cast]
    return pl.pallas_call(
        _prep_kernel,
        out_shape=[jax.ShapeDtypeStruct((bsz, n), F32), jax.ShapeDtypeStruct(w_bp.shape, BF16)]
                  + [jax.ShapeDtypeStruct(m.shape, BF16) for m in to_cast],
        grid=(n_steps,),
        in_specs=[
            pl.BlockSpec((bsz, d), lambda j: (0, 0)),
            pl.BlockSpec((d, ADALN_BLOCK_N), lambda j: (0, j)),
            pl.BlockSpec((1, ADALN_BLOCK_N), lambda j: (0, j)),
            pl.BlockSpec((None, rows, gd), lambda j: (j // spg, j % spg, 0)),
            pl.BlockSpec((1, gd), lambda j: (0, j // spg)),
            pl.BlockSpec((gd, w_bp.shape[1]), lambda j: (j // spg, 0)),
        ] + cast_specs,
        out_specs=[pl.BlockSpec((bsz, ADALN_BLOCK_N), lambda j: (0, j)),
                   pl.BlockSpec((rows, w_bp.shape[1]), lambda j: (j, 0))] + cast_specs,
        compiler_params=pltpu.CompilerParams(dimension_semantics=("arbitrary",)),
        name="prep",
    )(c, w, b.reshape(1, n), pool_w, pool_scale.reshape(1, groups * gd), w_bp, *to_cast)


class Staged(NamedTuple):
    probs: Any
    rot: Any
    pooled: Any
    sg: Any
    gates: Any
    x: Any


class StageParams(NamedTuple):
    g_pre: Any
    w_in: Any
    b_gate: Any
    invf: Any
    dmat: Any
    qdec: Any
    kdec: Any
    halo: Any


class MixParams(NamedTuple):
    g_post: Any
    w_pool: Any
    w_br: Any
    w_out: Any
    state: Any
    x1: Any


def _stage_phases(x_ref, rows, mod_ref, pos, seq_tile, first, sp: StageParams, out: Staged):
    tile = rows.stop - rows.start
    pool_width = out.pooled.shape[1]
    ret_width = out.sg.shape[1]
    d_model = out.gates.shape[1] // N_BRANCHES
    hd = ret_width // RET_HEADS
    sh_m, sc_m = mod_ref[0:1, :], mod_ref[1:2, :]
    x = x_ref[rows, :]
    out.x[...] = x
    h = ((_rms(x) * sp.g_pre[...]) * (1.0 + sc_m) + sh_m).astype(BF16)
    yield

    def proj(lo, width):
        return jnp.dot(h, sp.w_in[:, lo:lo + width], preferred_element_type=F32)

    u = proj(0, pool_width)
    yield

    if first is True:
        halo = jnp.zeros(sp.halo.shape, F32)
    elif first is False:
        halo = sp.halo[...]
    else:
        halo = jnp.where(first, 0.0, sp.halo[...])
    sp.halo[...] = u[tile - POOL_HALO:tile, :]
    t_pos = (lax.broadcasted_iota(jnp.int32, (tile, LANES), 0) + (seq_tile * tile + 1)).astype(F32)
    inv_t = 1.0 / t_pos
    gd = pool_width // len(POOL_WINDOWS)
    for gi, w in enumerate(POOL_WINDOWS):
        cols = slice(gi * gd, (gi + 1) * gd)
        inv_cnt = jnp.where(t_pos < float(w), inv_t, 1.0 / w)
        for r0 in range(0, tile, POOL_ROWS):
            blk = slice(r0, r0 + POOL_ROWS)
            before = halo[:, cols] if r0 == 0 else u[r0 - POOL_HALO:r0, cols]
            s = jnp.concatenate([before, u[blk, cols]], axis=0)
            k = 1
            while k < w:
                s = s + pltpu.roll(s, k, 0)
                k *= 2
            out.pooled[blk, cols] = (s[POOL_HALO:] * inv_cnt[blk] - u[blk, cols]).astype(BF16)
    q = proj(pool_width, ret_width)
    yield
    k_all = proj(pool_width + ret_width, ret_width)
    yield

    ang = sp.invf[...] * pos.astype(F32)
    cos_h, sin_h = jnp.cos(ang), jnp.sin(ang)
    cos_t = jnp.transpose(jnp.concatenate([cos_h, cos_h], axis=0))
    sin_t = jnp.transpose(jnp.concatenate([-sin_h, sin_h], axis=0))
    blk = sp.dmat.shape[1]
    roped = []
    for hh in range(RET_HEADS):
        cols = slice(hh * hd, (hh + 1) * hd)
        qh, kh = q[:, cols], k_all[:, cols]
        qh = (qh * cos_t + pltpu.roll(qh, hd // 2, 1) * sin_t) * (hd ** -0.5)
        kh = kh * cos_t + pltpu.roll(kh, hd // 2, 1) * sin_t
        roped.append((qh.astype(BF16), kh.astype(BF16)))
        for r0 in range(0, tile, blk):
            out.rot[1, r0:r0 + blk, cols] = (qh[r0:r0 + blk] * sp.qdec[hh]).astype(BF16)
            out.rot[2, r0:r0 + blk, cols] = (kh[r0:r0 + blk] * sp.kdec[hh]).astype(BF16)
    out.rot[0] = proj(pool_width + 2 * ret_width, ret_width).astype(BF16)
    yield
    g = proj(pool_width + 3 * ret_width, ret_width)
    out.sg[...] = g * jax.nn.sigmoid(g)
    yield
    for hh, (qh, kh) in enumerate(roped):
        for bi, r0 in enumerate(range(0, tile, blk)):
            scores = lax.dot_general(qh[r0:r0 + blk], kh[r0:r0 + blk], (((1,), (1,)), ((), ())),
                                     preferred_element_type=F32)
            out.probs[hh * (tile // blk) + bi] = (scores * sp.dmat[hh]).astype(BF16)
    yield
    gate_lo = pool_width + 4 * ret_width
    for br in range(N_BRANCHES):
        cols = slice(br * d_model, (br + 1) * d_model)
        out.gates[:, cols] = jax.nn.sigmoid(proj(gate_lo + br * d_model, d_model) + sp.b_gate[:, cols])
        yield


def _mix_phases(st: Staged, x_ref, rows, o_ref, mod_ref, mp: MixParams, tile_decay):
    d_model = x_ref.shape[1]
    ret_width = st.sg.shape[1]
    hd = ret_width // RET_HEADS
    gt_m = mod_ref[2:3, :]

    tile = st.sg.shape[0]
    blk = st.probs.shape[1]
    n_blk = tile // blk
    gated = [[None] * RET_HEADS for _ in range(n_blk)]
    for bi in range(n_blk):
        blk_rows = slice(bi * blk, (bi + 1) * blk)
        for hh in range(RET_HEADS):
            cols = slice(hh * hd, (hh + 1) * hd)
            vh = st.rot[0, blk_rows, cols]
            o = jnp.dot(st.probs[hh * n_blk + bi], vh, preferred_element_type=F32)
            o = o + jnp.dot(st.rot[1, blk_rows, cols], mp.state[hh].astype(BF16),
                            preferred_element_type=F32)
            kv = lax.dot_general(st.rot[2, blk_rows, cols], vh, (((0,), (0,)), ((), ())),
                                 preferred_element_type=F32)
            mp.state[hh] = mp.state[hh] * tile_decay[hh] + kv
            mu = jnp.mean(o, axis=-1, keepdims=True)
            oc = o - mu
            var = jnp.mean(oc * oc, axis=-1, keepdims=True)
            gated[bi][hh] = (st.sg[blk_rows, cols] * (oc * lax.rsqrt(var + EPS))).astype(BF16)
            yield
    y_pool = jnp.dot(st.pooled[...], mp.w_pool[...], preferred_element_type=F32)
    yield
    gated = jnp.concatenate([jnp.concatenate(row, axis=-1) for row in gated], axis=0)
    y_ret = jnp.dot(gated, mp.w_br[...], preferred_element_type=F32)
    yield
    merged = (st.gates[:, 0:d_model] * y_pool + st.gates[:, d_model:2 * d_model] * y_ret).astype(BF16)
    y = jnp.dot(merged, mp.w_out[...], preferred_element_type=F32)
    yield
    x1 = st.x[...] + gt_m * (_rms(y) * mp.g_post[...])
    o_ref[rows, :] = x1
    mp.x1[...] = x1


def _up_phases(x1_ref, mod_ref, g_ref, w1_ref, a_ref):
    sh_f, sc_f = mod_ref[3:4, :], mod_ref[4:5, :]
    h = ((_rms(x1_ref[...]) * g_ref[...]) * (1.0 + sc_f) + sh_f).astype(BF16)
    yield
    for c0 in range(0, w1_ref.shape[1], UP_CHUNK):
        a = jnp.maximum(jnp.dot(h, w1_ref[:, c0:c0 + UP_CHUNK], preferred_element_type=F32), 0.0)
        a_ref[:, c0:c0 + UP_CHUNK] = (a * a).astype(BF16)
        yield


WEAVE = "musmmmsumussumussumussusum"


def _weave(order, **gens):
    for c in order:
        if c in gens:
            next(gens[c], None)
    for g in gens.values():
        for _ in g:
            pass


def _mixer_kernel(x_ref, xn_ref, mod_all_ref, pos_ref, posn_ref, invf_ref, g_pre_ref, g_post_ref,
                  w_in_ref, b_gate_ref, w_pool_ref, w_br_ref, w_out_ref, dmat_ref, qdec_ref, kdec_ref,
                  g_ffn_ref, w1_ref, wf2_ref, o_ref, a_ref, wf2_o_ref, state_ref, halo_ref, x1_ref, *staged,
                  tile, steps_per_seq, n_tiles, tile_decay):
    j = pl.program_id(0)
    wf2_o_ref[...] = wf2_ref[...].astype(BF16)
    last = n_tiles - 1

    def batch_and_tile(t):
        return t // steps_per_seq, t % steps_per_seq

    b, s = batch_and_tile(jnp.minimum(j, last))
    b_next, s_next = batch_and_tile(jnp.minimum(j + 1, last))
    b_prev, _ = batch_and_tile(jnp.maximum(j - 1, 0))
    mod_ref, modn_ref, modp_ref = (ModRows(mod_all_ref, r) for r in (b, b_next, b_prev))
    sp = StageParams(g_pre_ref, w_in_ref, b_gate_ref, invf_ref, dmat_ref, qdec_ref, kdec_ref, halo_ref)
    mp = MixParams(g_post_ref, w_pool_ref, w_br_ref, w_out_ref, state_ref, x1_ref)
    n_fields = len(Staged._fields)
    slots = (Staged(*staged[:n_fields]), Staged(*staged[n_fields:]))
    whole = slice(0, tile)

    @pl.when(j == 0)
    def _():
        for _ in _stage_phases(x_ref, whole, mod_ref, pos_ref[pl.ds(b, 1), :], 0, True, sp, slots[0]):
            pass
        x1_ref[...] = jnp.zeros_like(x1_ref)

    @pl.when((s == 0) & (j <= last))
    def _():
        state_ref[...] = jnp.zeros_like(state_ref)

    def mix(slot):
        return _mix_phases(slot, x_ref, whole, o_ref, mod_ref, mp, tile_decay)

    def up():
        return _up_phases(x1_ref, modp_ref, g_ffn_ref, w1_ref, a_ref)

    for parity in range(2):
        @pl.when((j % 2 == parity) & (j < last))
        def _():
            _weave(WEAVE, m=mix(slots[parity]), u=up(),
                   s=_stage_phases(xn_ref, whole, modn_ref, posn_ref[pl.ds(b_next, 1), :], s_next,
                                   s_next == 0, sp, slots[1 - parity]))

    @pl.when(j == last)
    def _():
        _weave(WEAVE, m=mix(slots[last % 2]), u=up())

    @pl.when(j == n_tiles)
    def _():
        _weave(WEAVE, u=up())


def _decay_tables(tile, hd):
    gamma = 1.0 - 2.0 ** (-5.0 - np.arange(RET_HEADS, dtype=np.float64))
    idx = np.arange(tile)
    dist = np.abs(idx[:, None] - idx[None, :])
    visible = (idx[None, :] // CHUNK) <= (idx[:, None] // CHUNK)
    dmat = np.where(visible[None], gamma[:, None, None] ** dist[None], 0.0)
    qdec = np.broadcast_to((gamma[:, None] ** (idx[None, :] + 1))[:, :, None], (RET_HEADS, tile, hd))
    kdec = np.broadcast_to((gamma[:, None] ** (tile - 1 - idx[None, :]))[:, :, None], (RET_HEADS, tile, hd))
    tile_decay = tuple(float(g) ** tile for g in gamma)
    as_f32 = lambda a: jnp.asarray(np.ascontiguousarray(a), dtype=F32)
    return as_f32(dmat), as_f32(qdec), as_f32(kdec), tile_decay


def _const_spec(shape):
    nd = len(shape)
    return pl.BlockSpec(shape, lambda j: (0,) * nd)


def _mixer(x, mod, positions, g_pre, g_post, w_in, b_gate, w_pool, w_br, w_out, g_ffn, w1, w2):
    bsz, seq, d = x.shape
    tile = MIXER_TILE
    pool_width = w_pool.shape[0]
    ret_width = w_br.shape[0]
    d_ff = w1.shape[1]
    hd = ret_width // RET_HEADS
    assert seq % tile == 0 and tile % RET_BLOCK == 0 and RET_BLOCK % CHUNK == 0 and hd == LANES
    assert d_ff % UP_CHUNK == 0
    steps_per_seq = seq // tile
    n_tiles = bsz * steps_per_seq
    w2_rows = w2.shape[0] // n_tiles
    assert w2.shape[0] % (n_tiles * 16) == 0
    dmat, qdec, kdec, tile_decay = _decay_tables(RET_BLOCK, hd)
    half = hd // 2
    invf = (ROPE_BASE ** (-jnp.arange(half, dtype=F32) / half)).reshape(half, 1)

    def tile_at(t):
        return t // steps_per_seq, t % steps_per_seq

    def cur(j):
        return tile_at(jnp.minimum(j, n_tiles - 1))

    def nxt(j):
        return tile_at(jnp.minimum(j + 1, n_tiles - 1))

    def prev(j):
        return tile_at(jnp.maximum(j - 1, 0))

    const = _const_spec
    operands = [
        (x, pl.BlockSpec((None, tile, d), lambda j: (0, 0, 0))),
        (x, pl.BlockSpec((None, tile, d), lambda j: (*nxt(j), 0))),
        (mod, const(mod.shape)),
        (positions, pl.BlockSpec((bsz, tile), lambda j: (0, cur(j)[1]))),
        (positions, pl.BlockSpec((bsz, tile), lambda j: (0, nxt(j)[1]))),
        (invf, const((half, 1))),
        (g_pre.reshape(1, d), const((1, d))),
        (g_post.reshape(1, d), const((1, d))),
        (w_in, const(w_in.shape)),
        (b_gate.reshape(1, N_BRANCHES * d), const((1, N_BRANCHES * d))),
        (w_pool, const(w_pool.shape)),
        (w_br, const(w_br.shape)),
        (w_out, const(w_out.shape)),
        (dmat, const(dmat.shape)),
        (qdec, const(qdec.shape)),
        (kdec, const(kdec.shape)),
        (g_ffn.reshape(1, d), const((1, d))),
        (w1, const(w1.shape)),
        (w2, pl.BlockSpec((w2_rows, w2.shape[1]), lambda j: (jnp.minimum(j, n_tiles - 1), 0))),
    ]
    staged = [
        pltpu.VMEM((RET_HEADS * (tile // RET_BLOCK), RET_BLOCK, RET_BLOCK), BF16),
        pltpu.VMEM((3, tile, ret_width), BF16),
        pltpu.VMEM((tile, pool_width), BF16),
        pltpu.VMEM((tile, ret_width), F32),
        pltpu.VMEM((tile, N_BRANCHES * d), F32),
        pltpu.VMEM((tile, d), F32),
    ]
    return pl.pallas_call(
        functools.partial(_mixer_kernel, tile=tile, steps_per_seq=steps_per_seq, n_tiles=n_tiles,
                          tile_decay=tile_decay),
        out_shape=[jax.ShapeDtypeStruct(x.shape, F32), jax.ShapeDtypeStruct((bsz, seq, d_ff), BF16),
                   jax.ShapeDtypeStruct(w2.shape, BF16)],
        grid=(n_tiles + 1,),
        in_specs=[spec for _, spec in operands],
        out_specs=[pl.BlockSpec((None, tile, d), lambda j: (*cur(j), 0)),
                   pl.BlockSpec((None, tile, d_ff), lambda j: (*prev(j), 0)),
                   pl.BlockSpec((w2_rows, w2.shape[1]), lambda j: (jnp.minimum(j, n_tiles - 1), 0))],
        scratch_shapes=[
            pltpu.VMEM((RET_HEADS, hd, hd), F32),
            pltpu.VMEM((POOL_HALO, pool_width), F32),
            pltpu.VMEM((tile, d), F32),
        ] + staged + staged,
        compiler_params=pltpu.CompilerParams(
            dimension_semantics=("arbitrary",),
            vmem_limit_bytes=VMEM_LIMIT_BYTES),
        name="mixer",
    )(*[a for a, _ in operands])


def _ffn_down_kernel(a_ref, x_ref, mod_ref, g_post_ref, w2_ref, o_ref, *, tiles_per_seq):
    gt_f = ModRows(mod_ref, pl.program_id(0) // tiles_per_seq)[5:6, :]
    for r0 in range(0, a_ref.shape[0], FFN_ROWS):
        rows = slice(r0, r0 + FFN_ROWS)
        y = jnp.dot(a_ref[rows, :], w2_ref[...], preferred_element_type=F32)
        o_ref[rows, :] = x_ref[rows, :] + gt_f * (_rms(y) * g_post_ref[...])


def _ffn_down(a, x, mod, g_post, w2):
    bsz, seq, d = x.shape
    tile = FFN_TILE
    assert seq % tile == 0
    tiles_per_seq = seq // tile

    def cur(j):
        return j // tiles_per_seq, j % tiles_per_seq

    return pl.pallas_call(
        functools.partial(_ffn_down_kernel, tiles_per_seq=tiles_per_seq),
        out_shape=jax.ShapeDtypeStruct(x.shape, F32),
        grid=(bsz * tiles_per_seq,),
        in_specs=[
            pl.BlockSpec((None, tile, a.shape[2]), lambda j: (*cur(j), 0)),
            pl.BlockSpec((None, tile, d), lambda j: (*cur(j), 0)),
            _const_spec(mod.shape),
            _const_spec((1, d)),
            _const_spec(w2.shape),
        ],
        out_specs=pl.BlockSpec((None, tile, d), lambda j: (*cur(j), 0)),
        compiler_params=pltpu.CompilerParams(
            dimension_semantics=("arbitrary",),
            vmem_limit_bytes=VMEM_LIMIT_BYTES),
        name="ffn_down",
    )(a, x, mod, g_post.reshape(1, d), w2)


def kernel(x, c, positions, ada_w, ada_b, mix_pre_g, mix_post_g, ffn_pre_g, ffn_post_g, w_in,
           b_branch_gate, pool_w, pool_scale, w_branch_pool, w_branch_ret, w_out, w_ff1, w_ff2):
    depth = ada_w.shape[0]
    for l in range(depth):
        mod, w_pool, w_in_b, w_br_b, w_out_b, w1_b = _prep(
            c, ada_w[l], ada_b[l], pool_w[l], pool_scale[l], w_branch_pool[l],
            (w_in[l], w_branch_ret[l], w_out[l], w_ff1[l]))
        x, a, w2_b = _mixer(x, mod, positions, mix_pre_g[l], mix_post_g[l], w_in_b, b_branch_gate[l],
                            w_pool, w_br_b, w_out_b, ffn_pre_g[l], w1_b, w_ff2[l])
        x = _ffn_down(a, x, mod, ffn_post_g[l], w2_b)
    return x
```

```python
import functools
from typing import Any, NamedTuple

import jax
import jax.numpy as jnp
import numpy as np
from jax import lax
from jax.experimental import pallas as pl
from jax.experimental.pallas import tpu as pltpu

F32 = jnp.float32
BF16 = jnp.bfloat16

CHUNK = 64
POOL_WINDOWS = (2, 4, 8, 16)
POOL_HALO = 16
POOL_ROWS = 64
RET_HEADS = 4
N_BRANCHES = 2
N_MOD = 6
ROPE_BASE = 10000.0
EPS = 1e-6

LANES = 128
BF16_TILE_ROWS = 16
MIXER_TILE = 256
RET_BLOCK = 256
FFN_TILE = 1024
FFN_ROWS = 512
UP_CHUNK = 512
ADALN_BLOCK_N = 1536
VMEM_LIMIT_BYTES = 56 * 1024 * 1024


def _rms(xf):
    return xf * lax.rsqrt(jnp.mean(xf * xf, axis=-1, keepdims=True) + EPS)


class ModRows:
    def __init__(self, ref, b):
        self.ref, self.b, self.d = ref, b, ref.shape[1] // N_MOD

    def __getitem__(self, idx):
        k = idx[0].start
        return self.ref[pl.ds(self.b, 1), k * self.d:(k + 1) * self.d]


def _prep_kernel(c_ref, w_ref, b_ref, pw_ref, scale_ref, wbp_ref, *refs):
    n_cast = (len(refs) - 2) // 2
    cast_in, (mod_ref, w_pool_ref), cast_out = refs[:n_cast], refs[n_cast:n_cast + 2], refs[n_cast + 2:]
    c = c_ref[...]
    a = (c * jax.nn.sigmoid(c)).astype(BF16)
    mod_ref[...] = jnp.dot(a, w_ref[...].astype(BF16), preferred_element_type=F32) + b_ref[...]
    w_pool_ref[...] = jnp.dot((pw_ref[...] * scale_ref[...]).astype(BF16), wbp_ref[...].astype(BF16),
                              preferred_element_type=F32).astype(BF16)
    for src, dst in zip(cast_in, cast_out):
        dst[...] = src[...].astype(BF16)


def _prep(c, w, b, pool_w, pool_scale, w_bp, to_cast):
    bsz, d = c.shape
    n = w.shape[1]
    n_steps = n // ADALN_BLOCK_N
    groups, gd, _ = pool_w.shape
    spg = n_steps // groups
    rows = gd // spg
    assert n_steps % groups == 0 and gd % spg == 0 and rows % BF16_TILE_ROWS == 0
    assert all(m.shape[0] % (n_steps * BF16_TILE_ROWS) == 0 for m in to_cast)
    cast_specs = [pl.BlockSpec((m.shape[0] // n_steps, m.shape[1]), lambda j: (j, 0)) for m in to_cast]
    return pl.pallas_call(
        _prep_kernel,
        out_shape=[jax.ShapeDtypeStruct((bsz, n), F32), jax.ShapeDtypeStruct(w_bp.shape, BF16)]
                  + [jax.ShapeDtypeStruct(m.shape, BF16) for m in to_cast],
        grid=(n_steps,),
        in_specs=[
            pl.BlockSpec((bsz, d), lambda j: (0, 0)),
            pl.BlockSpec((d, ADALN_BLOCK_N), lambda j: (0, j)),
            pl.BlockSpec((1, ADALN_BLOCK_N), lambda j: (0, j)),
            pl.BlockSpec((None, rows, gd), lambda j: (j // spg, j % spg, 0)),
            pl.BlockSpec((1, gd), lambda j: (0, j // spg)),
            pl.BlockSpec((gd, w_bp.shape[1]), lambda j: (j // spg, 0)),
        ] + cast_specs,
        out_specs=[pl.BlockSpec((bsz, ADALN_BLOCK_N), lambda j: (0, j)),
                   pl.BlockSpec((rows, w_bp.shape[1]), lambda j: (j, 0))] + cast_specs,
        compiler_params=pltpu.CompilerParams(dimension_semantics=("arbitrary",)),
        name="prep",
    )(c, w, b.reshape(1, n), pool_w, pool_scale.reshape(1, groups * gd), w_bp, *to_cast)


class Staged(NamedTuple):
    probs: Any
    rot: Any
    pooled: Any
    sg: Any
    gates: Any


class StageParams(NamedTuple):
    g_pre: Any
    w_in: Any
    b_gate: Any
    invf: Any
    dmat: Any
    qdec: Any
    kdec: Any
    halo: Any


class MixParams(NamedTuple):
    g_post: Any
    w_pool: Any
    w_br: Any
    w_out: Any
    state: Any
    x1: Any


def _stage_phases(x_ref, rows, mod_ref, pos, seq_tile, first, sp: StageParams, out: Staged):
    tile = rows.stop - rows.start
    pool_width = out.pooled.shape[1]
    ret_width = out.sg.shape[1]
    d_model = out.gates.shape[1] // N_BRANCHES
    hd = ret_width // RET_HEADS
    sh_m, sc_m = mod_ref[0:1, :], mod_ref[1:2, :]
    h = ((_rms(x_ref[rows, :]) * sp.g_pre[...]) * (1.0 + sc_m) + sh_m).astype(BF16)
    yield

    def proj(lo, width):
        return jnp.dot(h, sp.w_in[:, lo:lo + width], preferred_element_type=F32)

    u = proj(0, pool_width)
    yield

    if first is True:
        halo = jnp.zeros(sp.halo.shape, F32)
    elif first is False:
        halo = sp.halo[...]
    else:
        halo = jnp.where(first, 0.0, sp.halo[...])
    sp.halo[...] = u[tile - POOL_HALO:tile, :]
    t_pos = (lax.broadcasted_iota(jnp.int32, (tile, LANES), 0) + (seq_tile * tile + 1)).astype(F32)
    inv_t = 1.0 / t_pos
    gd = pool_width // len(POOL_WINDOWS)
    for gi, w in enumerate(POOL_WINDOWS):
        cols = slice(gi * gd, (gi + 1) * gd)
        inv_cnt = jnp.where(t_pos < float(w), inv_t, 1.0 / w)
        for r0 in range(0, tile, POOL_ROWS):
            blk = slice(r0, r0 + POOL_ROWS)
            before = halo[:, cols] if r0 == 0 else u[r0 - POOL_HALO:r0, cols]
            s = jnp.concatenate([before, u[blk, cols]], axis=0)
            k = 1
            while k < w:
                s = s + pltpu.roll(s, k, 0)
                k *= 2
            out.pooled[blk, cols] = (s[POOL_HALO:] * inv_cnt[blk] - u[blk, cols]).astype(BF16)
    q = proj(pool_width, ret_width)
    yield
    k_all = proj(pool_width + ret_width, ret_width)
    yield

    ang = sp.invf[...] * pos.astype(F32)
    cos_h, sin_h = jnp.cos(ang), jnp.sin(ang)
    cos_t = jnp.transpose(jnp.concatenate([cos_h, cos_h], axis=0))
    sin_t = jnp.transpose(jnp.concatenate([-sin_h, sin_h], axis=0))
    blk = sp.dmat.shape[1]
    roped = []
    for hh in range(RET_HEADS):
        cols = slice(hh * hd, (hh + 1) * hd)
        qh, kh = q[:, cols], k_all[:, cols]
        qh = (qh * cos_t + pltpu.roll(qh, hd // 2, 1) * sin_t) * (hd ** -0.5)
        kh = kh * cos_t + pltpu.roll(kh, hd // 2, 1) * sin_t
        roped.append((qh.astype(BF16), kh.astype(BF16)))
        for r0 in range(0, tile, blk):
            out.rot[1, r0:r0 + blk, cols] = (qh[r0:r0 + blk] * sp.qdec[hh]).astype(BF16)
            out.rot[2, r0:r0 + blk, cols] = (kh[r0:r0 + blk] * sp.kdec[hh]).astype(BF16)
    out.rot[0] = proj(pool_width + 2 * ret_width, ret_width).astype(BF16)
    yield
    g = proj(pool_width + 3 * ret_width, ret_width)
    out.sg[...] = g * jax.nn.sigmoid(g)
    yield
    for hh, (qh, kh) in enumerate(roped):
        for bi, r0 in enumerate(range(0, tile, blk)):
            scores = lax.dot_general(qh[r0:r0 + blk], kh[r0:r0 + blk], (((1,), (1,)), ((), ())),
                                     preferred_element_type=F32)
            out.probs[hh * (tile // blk) + bi] = (scores * sp.dmat[hh]).astype(BF16)
    yield
    gate_lo = pool_width + 4 * ret_width
    for br in range(N_BRANCHES):
        cols = slice(br * d_model, (br + 1) * d_model)
        out.gates[:, cols] = jax.nn.sigmoid(proj(gate_lo + br * d_model, d_model) + sp.b_gate[:, cols])
        yield


def _mix_phases(st: Staged, x_ref, rows, o_ref, mod_ref, mp: MixParams, tile_decay):
    d_model = x_ref.shape[1]
    ret_width = st.sg.shape[1]
    hd = ret_width // RET_HEADS
    gt_m = mod_ref[2:3, :]

    tile = st.sg.shape[0]
    blk = st.probs.shape[1]
    n_blk = tile // blk
    gated = [[None] * RET_HEADS for _ in range(n_blk)]
    for bi in range(n_blk):
        blk_rows = slice(bi * blk, (bi + 1) * blk)
        for hh in range(RET_HEADS):
            cols = slice(hh * hd, (hh + 1) * hd)
            vh = st.rot[0, blk_rows, cols]
            o = jnp.dot(st.probs[hh * n_blk + bi], vh, preferred_element_type=F32)
            o = o + jnp.dot(st.rot[1, blk_rows, cols], mp.state[hh].astype(BF16),
                            preferred_element_type=F32)
            kv = lax.dot_general(st.rot[2, blk_rows, cols], vh, (((0,), (0,)), ((), ())),
                                 preferred_element_type=F32)
            mp.state[hh] = mp.state[hh] * tile_decay[hh] + kv
            mu = jnp.mean(o, axis=-1, keepdims=True)
            oc = o - mu
            var = jnp.mean(oc * oc, axis=-1, keepdims=True)
            gated[bi][hh] = (st.sg[blk_rows, cols] * (oc * lax.rsqrt(var + EPS))).astype(BF16)
            yield
    y_pool = jnp.dot(st.pooled[...], mp.w_pool[...], preferred_element_type=F32)
    yield
    gated = jnp.concatenate([jnp.concatenate(row, axis=-1) for row in gated], axis=0)
    y_ret = jnp.dot(gated, mp.w_br[...], preferred_element_type=F32)
    yield
    merged = (st.gates[:, 0:d_model] * y_pool + st.gates[:, d_model:2 * d_model] * y_ret).astype(BF16)
    y = jnp.dot(merged, mp.w_out[...], preferred_element_type=F32)
    yield
    x1 = x_ref[rows, :] + gt_m * (_rms(y) * mp.g_post[...])
    o_ref[rows, :] = x1
    mp.x1[...] = x1


def _up_phases(x1_ref, mod_ref, g_ref, w1_ref, a_ref):
    sh_f, sc_f = mod_ref[3:4, :], mod_ref[4:5, :]
    h = ((_rms(x1_ref[...]) * g_ref[...]) * (1.0 + sc_f) + sh_f).astype(BF16)
    yield
    for c0 in range(0, w1_ref.shape[1], UP_CHUNK):
        a = jnp.maximum(jnp.dot(h, w1_ref[:, c0:c0 + UP_CHUNK], preferred_element_type=F32), 0.0)
        a_ref[:, c0:c0 + UP_CHUNK] = (a * a).astype(BF16)
        yield


WEAVE = "musmmmsumussumussumussusum"


def _weave(order, **gens):
    for c in order:
        if c in gens:
            next(gens[c], None)
    for g in gens.values():
        for _ in g:
            pass


def _mixer_kernel(x_ref, xn_ref, mod_all_ref, pos_ref, posn_ref, invf_ref, g_pre_ref, g_post_ref,
                  w_in_ref, b_gate_ref, w_pool_ref, w_br_ref, w_out_ref, dmat_ref, qdec_ref, kdec_ref,
                  g_ffn_ref, w1_ref, wf2_ref, o_ref, a_ref, wf2_o_ref, state_ref, halo_ref, x1_ref, *staged,
                  tile, steps_per_seq, n_tiles, tile_decay):
    j = pl.program_id(0)
    wf2_o_ref[...] = wf2_ref[...].astype(BF16)
    last = n_tiles - 1

    def batch_and_tile(t):
        return t // steps_per_seq, t % steps_per_seq

    b, s = batch_and_tile(jnp.minimum(j, last))
    b_next, s_next = batch_and_tile(jnp.minimum(j + 1, last))
    b_prev, _ = batch_and_tile(jnp.maximum(j - 1, 0))
    mod_ref, modn_ref, modp_ref = (ModRows(mod_all_ref, r) for r in (b, b_next, b_prev))
    sp = StageParams(g_pre_ref, w_in_ref, b_gate_ref, invf_ref, dmat_ref, qdec_ref, kdec_ref, halo_ref)
    mp = MixParams(g_post_ref, w_pool_ref, w_br_ref, w_out_ref, state_ref, x1_ref)
    n_fields = len(Staged._fields)
    slots = (Staged(*staged[:n_fields]), Staged(*staged[n_fields:]))
    whole = slice(0, tile)

    @pl.when(j == 0)
    def _():
        for _ in _stage_phases(x_ref, whole, mod_ref, pos_ref[pl.ds(b, 1), :], 0, True, sp, slots[0]):
            pass
        x1_ref[...] = jnp.zeros_like(x1_ref)

    @pl.when((s == 0) & (j <= last))
    def _():
        state_ref[...] = jnp.zeros_like(state_ref)

    def mix(slot):
        return _mix_phases(slot, x_ref, whole, o_ref, mod_ref, mp, tile_decay)

    def up():
        return _up_phases(x1_ref, modp_ref, g_ffn_ref, w1_ref, a_ref)

    for parity in range(2):
        @pl.when((j % 2 == parity) & (j < last))
        def _():
            _weave(WEAVE, m=mix(slots[parity]), u=up(),
                   s=_stage_phases(xn_ref, whole, modn_ref, posn_ref[pl.ds(b_next, 1), :], s_next,
                                   s_next == 0, sp, slots[1 - parity]))

    @pl.when(j == last)
    def _():
        _weave(WEAVE, m=mix(slots[last % 2]), u=up())

    @pl.when(j == n_tiles)
    def _():
        _weave(WEAVE, u=up())


def _decay_tables(tile, hd):
    gamma = 1.0 - 2.0 ** (-5.0 - np.arange(RET_HEADS, dtype=np.float64))
    idx = np.arange(tile)
    dist = np.abs(idx[:, None] - idx[None, :])
    visible = (idx[None, :] // CHUNK) <= (idx[:, None] // CHUNK)
    dmat = np.where(visible[None], gamma[:, None, None] ** dist[None], 0.0)
    qdec = np.broadcast_to((gamma[:, None] ** (idx[None, :] + 1))[:, :, None], (RET_HEADS, tile, hd))
    kdec = np.broadcast_to((gamma[:, None] ** (tile - 1 - idx[None, :]))[:, :, None], (RET_HEADS, tile, hd))
    tile_decay = tuple(float(g) ** tile for g in gamma)
    as_f32 = lambda a: jnp.asarray(np.ascontiguousarray(a), dtype=F32)
    return as_f32(dmat), as_f32(qdec), as_f32(kdec), tile_decay


def _const_spec(shape):
    nd = len(shape)
    return pl.BlockSpec(shape, lambda j: (0,) * nd)


def _mixer(x, mod, positions, g_pre, g_post, w_in, b_gate, w_pool, w_br, w_out, g_ffn, w1, w2):
    bsz, seq, d = x.shape
    tile = MIXER_TILE
    pool_width = w_pool.shape[0]
    ret_width = w_br.shape[0]
    d_ff = w1.shape[1]
    hd = ret_width // RET_HEADS
    assert seq % tile == 0 and tile % RET_BLOCK == 0 and RET_BLOCK % CHUNK == 0 and hd == LANES
    assert d_ff % UP_CHUNK == 0
    steps_per_seq = seq // tile
    n_tiles = bsz * steps_per_seq
    w2_rows = w2.shape[0] // n_tiles
    assert w2.shape[0] % (n_tiles * BF16_TILE_ROWS) == 0
    dmat, qdec, kdec, tile_decay = _decay_tables(RET_BLOCK, hd)
    half = hd // 2
    invf = (ROPE_BASE ** (-jnp.arange(half, dtype=F32) / half)).reshape(half, 1)

    def tile_at(t):
        return t // steps_per_seq, t % steps_per_seq

    def cur(j):
        return tile_at(jnp.minimum(j, n_tiles - 1))

    def nxt(j):
        return tile_at(jnp.minimum(j + 1, n_tiles - 1))

    def prev(j):
        return tile_at(jnp.maximum(j - 1, 0))

    const = _const_spec
    operands = [
        (x, pl.BlockSpec((None, tile, d), lambda j: (*cur(j), 0))),
        (x, pl.BlockSpec((None, tile, d), lambda j: (*nxt(j), 0))),
        (mod, const(mod.shape)),
        (positions, pl.BlockSpec((bsz, tile), lambda j: (0, cur(j)[1]))),
        (positions, pl.BlockSpec((bsz, tile), lambda j: (0, nxt(j)[1]))),
        (invf, const((half, 1))),
        (g_pre.reshape(1, d), const((1, d))),
        (g_post.reshape(1, d), const((1, d))),
        (w_in, const(w_in.shape)),
        (b_gate.reshape(1, N_BRANCHES * d), const((1, N_BRANCHES * d))),
        (w_pool, const(w_pool.shape)),
        (w_br, const(w_br.shape)),
        (w_out, const(w_out.shape)),
        (dmat, const(dmat.shape)),
        (qdec, const(qdec.shape)),
        (kdec, const(kdec.shape)),
        (g_ffn.reshape(1, d), const((1, d))),
        (w1, const(w1.shape)),
        (w2, pl.BlockSpec((w2_rows, w2.shape[1]), lambda j: (jnp.minimum(j, n_tiles - 1), 0))),
    ]
    staged = [
        pltpu.VMEM((RET_HEADS * (tile // RET_BLOCK), RET_BLOCK, RET_BLOCK), BF16),
        pltpu.VMEM((3, tile, ret_width), BF16),
        pltpu.VMEM((tile, pool_width), BF16),
        pltpu.VMEM((tile, ret_width), F32),
        pltpu.VMEM((tile, N_BRANCHES * d), F32),
    ]
    return pl.pallas_call(
        functools.partial(_mixer_kernel, tile=tile, steps_per_seq=steps_per_seq, n_tiles=n_tiles,
                          tile_decay=tile_decay),
        out_shape=[jax.ShapeDtypeStruct(x.shape, F32), jax.ShapeDtypeStruct((bsz, seq, d_ff), BF16),
                   jax.ShapeDtypeStruct(w2.shape, BF16)],
        grid=(n_tiles + 1,),
        in_specs=[spec for _, spec in operands],
        out_specs=[pl.BlockSpec((None, tile, d), lambda j: (*cur(j), 0)),
                   pl.BlockSpec((None, tile, d_ff), lambda j: (*prev(j), 0)),
                   pl.BlockSpec((w2_rows, w2.shape[1]), lambda j: (jnp.minimum(j, n_tiles - 1), 0))],
        scratch_shapes=[
            pltpu.VMEM((RET_HEADS, hd, hd), F32),
            pltpu.VMEM((POOL_HALO, pool_width), F32),
            pltpu.VMEM((tile, d), F32),
        ] + staged + staged,
        compiler_params=pltpu.CompilerParams(
            dimension_semantics=("arbitrary",),
            vmem_limit_bytes=VMEM_LIMIT_BYTES),
        name="mixer",
    )(*[a for a, _ in operands])


def _ffn_down_kernel(a_ref, x_ref, mod_ref, g_post_ref, w2_ref, o_ref, *, tiles_per_seq):
    gt_f = ModRows(mod_ref, pl.program_id(0) // tiles_per_seq)[5:6, :]
    for r0 in range(0, a_ref.shape[0], FFN_ROWS):
        rows = slice(r0, r0 + FFN_ROWS)
        y = jnp.dot(a_ref[rows, :], w2_ref[...], preferred_element_type=F32)
        o_ref[rows, :] = x_ref[rows, :] + gt_f * (_rms(y) * g_post_ref[...])


def _ffn_down(a, x, mod, g_post, w2):
    bsz, seq, d = x.shape
    tile = FFN_TILE
    assert seq % tile == 0
    tiles_per_seq = seq // tile

    def cur(j):
        return j // tiles_per_seq, j % tiles_per_seq

    return pl.pallas_call(
        functools.partial(_ffn_down_kernel, tiles_per_seq=tiles_per_seq),
        out_shape=jax.ShapeDtypeStruct(x.shape, F32),
        grid=(bsz * tiles_per_seq,),
        in_specs=[
            pl.BlockSpec((None, tile, a.shape[2]), lambda j: (*cur(j), 0)),
            pl.BlockSpec((None, tile, d), lambda j: (*cur(j), 0)),
            _const_spec(mod.shape),
            _const_spec((1, d)),
            _const_spec(w2.shape),
        ],
        out_specs=pl.BlockSpec((None, tile, d), lambda j: (*cur(j), 0)),
        compiler_params=pltpu.CompilerParams(
            dimension_semantics=("arbitrary",),
            vmem_limit_bytes=VMEM_LIMIT_BYTES),
        name="ffn_down",
    )(a, x, mod, g_post.reshape(1, d), w2)


def kernel(x, c, positions, ada_w, ada_b, mix_pre_g, mix_post_g, ffn_pre_g, ffn_post_g, w_in,
           b_branch_gate, pool_w, pool_scale, w_branch_pool, w_branch_ret, w_out, w_ff1, w_ff2):
    depth = ada_w.shape[0]
    for l in range(depth):
        mod, w_pool, w_in_b, w_br_b, w_out_b, w1_b = _prep(
            c, ada_w[l], ada_b[l], pool_w[l], pool_scale[l], w_branch_pool[l],
            (w_in[l], w_branch_ret[l], w_out[l], w_ff1[l]))
        x, a, w2_b = _mixer(x, mod, positions, mix_pre_g[l], mix_post_g[l], w_in_b, b_branch_gate[l],
                            w_pool, w_br_b, w_out_b, ffn_pre_g[l], w1_b, w_ff2[l])
        x = _ffn_down(a, x, mod, ffn_post_g[l], w2_b)
    return x
```

```python
import functools
from typing import Any, NamedTuple

import jax
import jax.numpy as jnp
import numpy as np
from jax import lax
from jax.experimental import pallas as pl
from jax.experimental.pallas import tpu as pltpu

F32 = jnp.float32
BF16 = jnp.bfloat16

CHUNK = 64
POOL_WINDOWS = (2, 4, 8, 16)
POOL_HALO = 16
POOL_ROWS = 64
RET_HEADS = 4
N_BRANCHES = 2
N_MOD = 6
ROPE_BASE = 10000.0
EPS = 1e-6

LANES = 128
BF16_TILE_ROWS = 16
MIXER_TILE = 256
RET_BLOCK = 256
FFN_TILE = 1024
FFN_ROWS = 512
UP_CHUNK = 512
ADALN_BLOCK_N = 1536
VMEM_LIMIT_BYTES = 56 * 1024 * 1024


def _rms(xf):
    return xf * lax.rsqrt(jnp.mean(xf * xf, axis=-1, keepdims=True) + EPS)


class ModRows:
    def __init__(self, ref, b):
        self.ref, self.b, self.d = ref, b, ref.shape[1] // N_MOD

    def __getitem__(self, idx):
        k = idx[0].start
        return self.ref[pl.ds(self.b, 1), k * self.d:(k + 1) * self.d]


def _prep_kernel(c_ref, w_ref, b_ref, pw_ref, scale_ref, wbp_ref, *refs):
    n_cast = (len(refs) - 2) // 2
    cast_in, (mod_ref, w_pool_ref), cast_out = refs[:n_cast], refs[n_cast:n_cast + 2], refs[n_cast + 2:]
    c = c_ref[...]
    a = (c * jax.nn.sigmoid(c)).astype(BF16)
    mod_ref[...] = jnp.dot(a, w_ref[...].astype(BF16), preferred_element_type=F32) + b_ref[...]
    w_pool_ref[...] = jnp.dot((pw_ref[...] * scale_ref[...]).astype(BF16), wbp_ref[...].astype(BF16),
                              preferred_element_type=F32).astype(BF16)
    for src, dst in zip(cast_in, cast_out):
        dst[...] = src[...].astype(BF16)


def _prep(c, w, b, pool_w, pool_scale, w_bp, to_cast):
    bsz, d = c.shape
    n = w.shape[1]
    n_steps = n // ADALN_BLOCK_N
    groups, gd, _ = pool_w.shape
    spg = n_steps // groups
    rows = gd // spg
    assert n_steps % groups == 0 and gd % spg == 0 and rows % BF16_TILE_ROWS == 0
    assert all(m.shape[0] % (n_steps * BF16_TILE_ROWS) == 0 for m in to_cast)
    cast_specs = [pl.BlockSpec((m.shape[0] // n_steps, m.shape[1]), lambda j: (j, 0)) for m in to_cast]
    return pl.pallas_call(
        _prep_kernel,
        out_shape=[jax.ShapeDtypeStruct((bsz, n), F32), jax.ShapeDtypeStruct(w_bp.shape, BF16)]
                  + [jax.ShapeDtypeStruct(m.shape, BF16) for m in to_cast],
        grid=(n_steps,),
        in_specs=[
            pl.BlockSpec((bsz, d), lambda j: (0, 0)),
            pl.BlockSpec((d, ADALN_BLOCK_N), lambda j: (0, j)),
            pl.BlockSpec((1, ADALN_BLOCK_N), lambda j: (0, j)),
            pl.BlockSpec((None, rows, gd), lambda j: (j // spg, j % spg, 0)),
            pl.BlockSpec((1, gd), lambda j: (0, j // spg)),
            pl.BlockSpec((gd, w_bp.shape[1]), lambda j: (j // spg, 0)),
        ] + cast_specs,
        out_specs=[pl.BlockSpec((bsz, ADALN_BLOCK_N), lambda j: (0, j)),
                   pl.BlockSpec((rows, w_bp.shape[1]), lambda j: (j, 0))] + cast_specs,
        compiler_params=pltpu.CompilerParams(dimension_semantics=("arbitrary",)),
        name="prep",
    )(c, w, b.reshape(1, n), pool_w, pool_scale.reshape(1, groups * gd), w_bp, *to_cast)


class Staged(NamedTuple):
    probs: Any
    rot: Any
    pooled: Any
    sg: Any
    gates: Any


class StageParams(NamedTuple):
    g_pre: Any
    w_in: Any
    b_gate: Any
    invf: Any
    dmat: Any
    qdec: Any
    kdec: Any
    halo: Any


class MixParams(NamedTuple):
    g_post: Any
    w_pool: Any
    w_br: Any
    w_out: Any
    state: Any
    x1: Any


def _stage_phases(x_ref, rows, mod_ref, pos, seq_tile, first, sp: StageParams, out: Staged):
    tile = rows.stop - rows.start
    pool_width = out.pooled.shape[1]
    ret_width = out.sg.shape[1]
    d_model = out.gates.shape[1] // N_BRANCHES
    hd = ret_width // RET_HEADS
    sh_m, sc_m = mod_ref[0:1, :], mod_ref[1:2, :]
    h = ((_rms(x_ref[rows, :]) * sp.g_pre[...]) * (1.0 + sc_m) + sh_m).astype(BF16)
    yield

    def proj(lo, width):
        return jnp.dot(h, sp.w_in[:, lo:lo + width], preferred_element_type=F32)

    u = proj(0, pool_width)
    yield

    if first is True:
        halo = jnp.zeros(sp.halo.shape, F32)
    elif first is False:
        halo = sp.halo[...]
    else:
        halo = jnp.where(first, 0.0, sp.halo[...])
    sp.halo[...] = u[tile - POOL_HALO:tile, :]
    t_pos = (lax.broadcasted_iota(jnp.int32, (tile, LANES), 0) + (seq_tile * tile + 1)).astype(F32)
    inv_t = 1.0 / t_pos
    gd = pool_width // len(POOL_WINDOWS)
    for gi, w in enumerate(POOL_WINDOWS):
        cols = slice(gi * gd, (gi + 1) * gd)
        inv_cnt = jnp.where(t_pos < float(w), inv_t, 1.0 / w)
        for r0 in range(0, tile, POOL_ROWS):
            blk = slice(r0, r0 + POOL_ROWS)
            before = halo[:, cols] if r0 == 0 else u[r0 - POOL_HALO:r0, cols]
            s = jnp.concatenate([before, u[blk, cols]], axis=0)
            k = 1
            while k < w:
                s = s + pltpu.roll(s, k, 0)
                k *= 2
            out.pooled[blk, cols] = (s[POOL_HALO:] * inv_cnt[blk] - u[blk, cols]).astype(BF16)
    q = proj(pool_width, ret_width)
    yield
    k_all = proj(pool_width + ret_width, ret_width)
    yield

    ang = sp.invf[...] * pos.astype(F32)
    cos_h, sin_h = jnp.cos(ang), jnp.sin(ang)
    cos_t = jnp.transpose(jnp.concatenate([cos_h, cos_h], axis=0))
    sin_t = jnp.transpose(jnp.concatenate([-sin_h, sin_h], axis=0))
    blk = sp.dmat.shape[1]
    roped = []
    for hh in range(RET_HEADS):
        cols = slice(hh * hd, (hh + 1) * hd)
        qh, kh = q[:, cols], k_all[:, cols]
        qh = (qh * cos_t + pltpu.roll(qh, hd // 2, 1) * sin_t) * (hd ** -0.5)
        kh = kh * cos_t + pltpu.roll(kh, hd // 2, 1) * sin_t
        roped.append((qh.astype(BF16), kh.astype(BF16)))
        for r0 in range(0, tile, blk):
            out.rot[1, r0:r0 + blk, cols] = (qh[r0:r0 + blk] * sp.qdec[hh]).astype(BF16)
            out.rot[2, r0:r0 + blk, cols] = (kh[r0:r0 + blk] * sp.kdec[hh]).astype(BF16)
    out.rot[0] = proj(pool_width + 2 * ret_width, ret_width).astype(BF16)
    yield
    g = proj(pool_width + 3 * ret_width, ret_width)
    out.sg[...] = g * jax.nn.sigmoid(g)
    yield
    for hh, (qh, kh) in enumerate(roped):
        for bi, r0 in enumerate(range(0, tile, blk)):
            scores = lax.dot_general(qh[r0:r0 + blk], kh[r0:r0 + blk], (((1,), (1,)), ((), ())),
                                     preferred_element_type=F32)
            out.probs[hh * (tile // blk) + bi] = (scores * sp.dmat[hh]).astype(BF16)
    yield
    gate_lo = pool_width + 4 * ret_width
    for br in range(N_BRANCHES):
        cols = slice(br * d_model, (br + 1) * d_model)
        out.gates[:, cols] = jax.nn.sigmoid(proj(gate_lo + br * d_model, d_model) + sp.b_gate[:, cols])
        yield


def _mix_phases(st: Staged, x_ref, rows, o_ref, mod_ref, mp: MixParams, tile_decay):
    d_model = x_ref.shape[1]
    ret_width = st.sg.shape[1]
    hd = ret_width // RET_HEADS
    gt_m = mod_ref[2:3, :]

    y_pool = jnp.dot(st.pooled[...], mp.w_pool[...], preferred_element_type=F32)
    yield
    tile = st.sg.shape[0]
    blk = st.probs.shape[1]
    n_blk = tile // blk
    gated = [[None] * RET_HEADS for _ in range(n_blk)]
    for bi in range(n_blk):
        blk_rows = slice(bi * blk, (bi + 1) * blk)
        for hh in range(RET_HEADS):
            cols = slice(hh * hd, (hh + 1) * hd)
            vh = st.rot[0, blk_rows, cols]
            o = jnp.dot(st.probs[hh * n_blk + bi], vh, preferred_element_type=F32)
            o = o + jnp.dot(st.rot[1, blk_rows, cols], mp.state[hh].astype(BF16),
                            preferred_element_type=F32)
            kv = lax.dot_general(st.rot[2, blk_rows, cols], vh, (((0,), (0,)), ((), ())),
                                 preferred_element_type=F32)
            mp.state[hh] = mp.state[hh] * tile_decay[hh] + kv
            mu = jnp.mean(o, axis=-1, keepdims=True)
            oc = o - mu
            var = jnp.mean(oc * oc, axis=-1, keepdims=True)
            gated[bi][hh] = (st.sg[blk_rows, cols] * (oc * lax.rsqrt(var + EPS))).astype(BF16)
            yield
    gated = jnp.concatenate([jnp.concatenate(row, axis=-1) for row in gated], axis=0)
    y_ret = jnp.dot(gated, mp.w_br[...], preferred_element_type=F32)
    yield
    merged = (st.gates[:, 0:d_model] * y_pool + st.gates[:, d_model:2 * d_model] * y_ret).astype(BF16)
    y = jnp.dot(merged, mp.w_out[...], preferred_element_type=F32)
    yield
    x1 = x_ref[rows, :] + gt_m * (_rms(y) * mp.g_post[...])
    o_ref[rows, :] = x1
    mp.x1[...] = x1


def _up_phases(x1_ref, mod_ref, g_ref, w1_ref, a_ref):
    sh_f, sc_f = mod_ref[3:4, :], mod_ref[4:5, :]
    h = ((_rms(x1_ref[...]) * g_ref[...]) * (1.0 + sc_f) + sh_f).astype(BF16)
    yield
    for c0 in range(0, w1_ref.shape[1], UP_CHUNK):
        a = jnp.maximum(jnp.dot(h, w1_ref[:, c0:c0 + UP_CHUNK], preferred_element_type=F32), 0.0)
        a_ref[:, c0:c0 + UP_CHUNK] = (a * a).astype(BF16)
        yield


WEAVE = "musmmmsumussumussumussusum"


def _weave(order, **gens):
    for c in order:
        if c in gens:
            next(gens[c], None)
    for g in gens.values():
        for _ in g:
            pass


def _mixer_kernel(x_ref, xn_ref, mod_all_ref, pos_ref, posn_ref, invf_ref, g_pre_ref, g_post_ref,
                  w_in_ref, b_gate_ref, w_pool_ref, w_br_ref, w_out_ref, dmat_ref, qdec_ref, kdec_ref,
                  g_ffn_ref, w1_ref, wf2_ref, o_ref, a_ref, wf2_o_ref, state_ref, halo_ref, x1_ref, *staged,
                  tile, steps_per_seq, n_tiles, tile_decay):
    j = pl.program_id(0)
    wf2_o_ref[...] = wf2_ref[...].astype(BF16)
    last = n_tiles - 1

    def batch_and_tile(t):
        return t // steps_per_seq, t % steps_per_seq

    b, s = batch_and_tile(jnp.minimum(j, last))
    b_next, s_next = batch_and_tile(jnp.minimum(j + 1, last))
    b_prev, _ = batch_and_tile(jnp.maximum(j - 1, 0))
    mod_ref, modn_ref, modp_ref = (ModRows(mod_all_ref, r) for r in (b, b_next, b_prev))
    sp = StageParams(g_pre_ref, w_in_ref, b_gate_ref, invf_ref, dmat_ref, qdec_ref, kdec_ref, halo_ref)
    mp = MixParams(g_post_ref, w_pool_ref, w_br_ref, w_out_ref, state_ref, x1_ref)
    n_fields = len(Staged._fields)
    slots = (Staged(*staged[:n_fields]), Staged(*staged[n_fields:]))
    whole = slice(0, tile)

    @pl.when(j == 0)
    def _():
        for _ in _stage_phases(x_ref, whole, mod_ref, pos_ref[pl.ds(b, 1), :], 0, True, sp, slots[0]):
            pass
        x1_ref[...] = jnp.zeros_like(x1_ref)

    @pl.when((s == 0) & (j <= last))
    def _():
        state_ref[...] = jnp.zeros_like(state_ref)

    def mix(slot):
        return _mix_phases(slot, x_ref, whole, o_ref, mod_ref, mp, tile_decay)

    def up():
        return _up_phases(x1_ref, modp_ref, g_ffn_ref, w1_ref, a_ref)

    for parity in range(2):
        @pl.when((j % 2 == parity) & (j < last))
        def _():
            _weave(WEAVE, m=mix(slots[parity]), u=up(),
                   s=_stage_phases(xn_ref, whole, modn_ref, posn_ref[pl.ds(b_next, 1), :], s_next,
                                   s_next == 0, sp, slots[1 - parity]))

    @pl.when(j == last)
    def _():
        _weave(WEAVE, m=mix(slots[last % 2]), u=up())

    @pl.when(j == n_tiles)
    def _():
        _weave(WEAVE, u=up())


def _decay_tables(tile, hd):
    gamma = 1.0 - 2.0 ** (-5.0 - np.arange(RET_HEADS, dtype=np.float64))
    idx = np.arange(tile)
    dist = np.abs(idx[:, None] - idx[None, :])
    visible = (idx[None, :] // CHUNK) <= (idx[:, None] // CHUNK)
    dmat = np.where(visible[None], gamma[:, None, None] ** dist[None], 0.0)
    qdec = np.broadcast_to((gamma[:, None] ** (idx[None, :] + 1))[:, :, None], (RET_HEADS, tile, hd))
    kdec = np.broadcast_to((gamma[:, None] ** (tile - 1 - idx[None, :]))[:, :, None], (RET_HEADS, tile, hd))
    tile_decay = tuple(float(g) ** tile for g in gamma)
    as_f32 = lambda a: jnp.asarray(np.ascontiguousarray(a), dtype=F32)
    return as_f32(dmat), as_f32(qdec), as_f32(kdec), tile_decay


def _const_spec(shape):
    nd = len(shape)
    return pl.BlockSpec(shape, lambda j: (0,) * nd)


def _mixer(x, mod, positions, g_pre, g_post, w_in, b_gate, w_pool, w_br, w_out, g_ffn, w1, w2):
    bsz, seq, d = x.shape
    tile = MIXER_TILE
    pool_width = w_pool.shape[0]
    ret_width = w_br.shape[0]
    d_ff = w1.shape[1]
    hd = ret_width // RET_HEADS
    assert seq % tile == 0 and tile % RET_BLOCK == 0 and RET_BLOCK % CHUNK == 0 and hd == LANES
    assert d_ff % UP_CHUNK == 0
    steps_per_seq = seq // tile
    n_tiles = bsz * steps_per_seq
    w2_rows = w2.shape[0] // n_tiles
    assert w2.shape[0] % (n_tiles * BF16_TILE_ROWS) == 0
    dmat, qdec, kdec, tile_decay = _decay_tables(RET_BLOCK, hd)
    half = hd // 2
    invf = (ROPE_BASE ** (-jnp.arange(half, dtype=F32) / half)).reshape(half, 1)

    def tile_at(t):
        return t // steps_per_seq, t % steps_per_seq

    def cur(j):
        return tile_at(jnp.minimum(j, n_tiles - 1))

    def nxt(j):
        return tile_at(jnp.minimum(j + 1, n_tiles - 1))

    def prev(j):
        return tile_at(jnp.maximum(j - 1, 0))

    const = _const_spec
    operands = [
        (x, pl.BlockSpec((None, tile, d), lambda j: (*cur(j), 0))),
        (x, pl.BlockSpec((None, tile, d), lambda j: (*nxt(j), 0))),
        (mod, const(mod.shape)),
        (positions, pl.BlockSpec((bsz, tile), lambda j: (0, cur(j)[1]))),
        (positions, pl.BlockSpec((bsz, tile), lambda j: (0, nxt(j)[1]))),
        (invf, const((half, 1))),
        (g_pre.reshape(1, d), const((1, d))),
        (g_post.reshape(1, d), const((1, d))),
        (w_in, const(w_in.shape)),
        (b_gate.reshape(1, N_BRANCHES * d), const((1, N_BRANCHES * d))),
        (w_pool, const(w_pool.shape)),
        (w_br, const(w_br.shape)),
        (w_out, const(w_out.shape)),
        (dmat, const(dmat.shape)),
        (qdec, const(qdec.shape)),
        (kdec, const(kdec.shape)),
        (g_ffn.reshape(1, d), const((1, d))),
        (w1, const(w1.shape)),
        (w2, pl.BlockSpec((w2_rows, w2.shape[1]), lambda j: (jnp.minimum(j, n_tiles - 1), 0))),
    ]
    staged = [
        pltpu.VMEM((RET_HEADS * (tile // RET_BLOCK), RET_BLOCK, RET_BLOCK), BF16),
        pltpu.VMEM((3, tile, ret_width), BF16),
        pltpu.VMEM((tile, pool_width), BF16),
        pltpu.VMEM((tile, ret_width), F32),
        pltpu.VMEM((tile, N_BRANCHES * d), F32),
    ]
    return pl.pallas_call(
        functools.partial(_mixer_kernel, tile=tile, steps_per_seq=steps_per_seq, n_tiles=n_tiles,
                          tile_decay=tile_decay),
        out_shape=[jax.ShapeDtypeStruct(x.shape, F32), jax.ShapeDtypeStruct((bsz, seq, d_ff), BF16),
                   jax.ShapeDtypeStruct(w2.shape, BF16)],
        grid=(n_tiles + 1,),
        in_specs=[spec for _, spec in operands],
        out_specs=[pl.BlockSpec((None, tile, d), lambda j: (*cur(j), 0)),
                   pl.BlockSpec((None, tile, d_ff), lambda j: (*prev(j), 0)),
                   pl.BlockSpec((w2_rows, w2.shape[1]), lambda j: (jnp.minimum(j, n_tiles - 1), 0))],
        scratch_shapes=[
            pltpu.VMEM((RET_HEADS, hd, hd), F32),
            pltpu.VMEM((POOL_HALO, pool_width), F32),
            pltpu.VMEM((tile, d), F32),
        ] + staged + staged,
        compiler_params=pltpu.CompilerParams(
            dimension_semantics=("arbitrary",),
            vmem_limit_bytes=VMEM_LIMIT_BYTES),
        name="mixer",
    )(*[a for a, _ in operands])


def _ffn_down_kernel(a_ref, x_ref, mod_ref, g_post_ref, w2_ref, o_ref, *, tiles_per_seq):
    gt_f = ModRows(mod_ref, pl.program_id(0) // tiles_per_seq)[5:6, :]
    for r0 in range(0, a_ref.shape[0], FFN_ROWS):
        rows = slice(r0, r0 + FFN_ROWS)
        y = jnp.dot(a_ref[rows, :], w2_ref[...], preferred_element_type=F32)
        o_ref[rows, :] = x_ref[rows, :] + gt_f * (_rms(y) * g_post_ref[...])


def _ffn_down(a, x, mod, g_post, w2):
    bsz, seq, d = x.shape
    tile = FFN_TILE
    assert seq % tile == 0
    tiles_per_seq = seq // tile

    def cur(j):
        return j // tiles_per_seq, j % tiles_per_seq

    return pl.pallas_call(
        functools.partial(_ffn_down_kernel, tiles_per_seq=tiles_per_seq),
        out_shape=jax.ShapeDtypeStruct(x.shape, F32),
        grid=(bsz * tiles_per_seq,),
        in_specs=[
            pl.BlockSpec((None, tile, a.shape[2]), lambda j: (*cur(j), 0)),
            pl.BlockSpec((None, tile, d), lambda j: (*cur(j), 0)),
            _const_spec(mod.shape),
            _const_spec((1, d)),
            _const_spec(w2.shape),
        ],
        out_specs=pl.BlockSpec((None, tile, d), lambda j: (*cur(j), 0)),
        compiler_params=pltpu.CompilerParams(
            dimension_semantics=("arbitrary",),
            vmem_limit_bytes=VMEM_LIMIT_BYTES),
        name="ffn_down",
    )(a, x, mod, g_post.reshape(1, d), w2)


def kernel(x, c, positions, ada_w, ada_b, mix_pre_g, mix_post_g, ffn_pre_g, ffn_post_g, w_in,
           b_branch_gate, pool_w, pool_scale, w_branch_pool, w_branch_ret, w_out, w_ff1, w_ff2):
    depth = ada_w.shape[0]
    for l in range(depth):
        mod, w_pool, w_in_b, w_br_b, w_out_b, w1_b = _prep(
            c, ada_w[l], ada_b[l], pool_w[l], pool_scale[l], w_branch_pool[l],
            (w_in[l], w_branch_ret[l], w_out[l], w_ff1[l]))
        x, a, w2_b = _mixer(x, mod, positions, mix_pre_g[l], mix_post_g[l], w_in_b, b_branch_gate[l],
                            w_pool, w_br_b, w_out_b, ffn_pre_g[l], w1_b, w_ff2[l])
        x = _ffn_down(a, x, mod, ffn_post_g[l], w2_b)
    return x
```
